```python
import jax
import jax.numpy as jnp
from jax import lax
import numpy as np

D_MODEL = 1024
BATCH = 4
SEQ = 8192
DEPTH = 1

PLE_DIM = 256
D_FF = 2816
N_HEADS = 8
N_KV_HEADS = 2
HEAD_DIM = 64
HEADS_PER_GROUP = N_HEADS // N_KV_HEADS
NSA_WIDTH = N_HEADS * HEAD_DIM
KV_WIDTH = N_KV_HEADS * HEAD_DIM
S5_WIDTH = D_MODEL - NSA_WIDTH
S5_GROUP = 16
S5_GROUPS = S5_WIDTH // S5_GROUP
S5_STATE = 64
CMP_LEN = 32
CMP_STRIDE = 16
CMP_HIDDEN = 256
SEL_BLOCK = 64
SEL_TOPK = 16
WINDOW = 512
Q_BLOCK = 128
ROPE_THETA = 10000.0
RMS_EPS = 1e-6
NEG = -1e30
BIG = 1e9
IN_COLS = NSA_WIDTH + 6 * KV_WIDTH + 3 * N_HEADS + S5_WIDTH

kernel_name = "hybrid_nsa_s5_macaron_block"


def rmsnorm(x, g):
    xf = x.astype(jnp.float32)
    y = xf * lax.rsqrt(jnp.mean(xf * xf, axis=-1, keepdims=True) + RMS_EPS)
    return (y * g.astype(jnp.float32)).astype(x.dtype)


def swiglu(x, w1, w3, w2):
    return (jax.nn.silu(x @ w1) * (x @ w3)) @ w2


def rope(x, pos):
    half = HEAD_DIM // 2
    inv = ROPE_THETA ** (-jnp.arange(half, dtype=jnp.float32) / half)
    ang = pos[:, None] * inv[None, :]
    cos = jnp.cos(ang)[None, :, None, :]
    sin = jnp.sin(ang)[None, :, None, :]
    xf = x.astype(jnp.float32)
    x1, x2 = xf[..., :half], xf[..., half:]
    return jnp.concatenate([x1 * cos - x2 * sin, x1 * sin + x2 * cos], axis=-1).astype(x.dtype)


def masked_softmax(s, mask):
    s = jnp.where(mask, s, NEG)
    m = jnp.max(s, axis=-1, keepdims=True)
    e = jnp.where(mask, jnp.exp(s - m), 0.0)
    return e / jnp.maximum(jnp.sum(e, axis=-1, keepdims=True), 1e-30)


def compress(kv, pe, w1, w2):
    B, L = kv.shape[:2]
    n_cmp = (L - CMP_LEN) // CMP_STRIDE + 1
    tok = jnp.arange(n_cmp)[:, None] * CMP_STRIDE + jnp.arange(CMP_LEN)[None, :]
    blk = kv[:, tok] + pe[None, None, :, None, :]
    blk = jnp.transpose(blk, (0, 1, 3, 2, 4)).reshape(B, n_cmp, N_KV_HEADS, CMP_LEN * HEAD_DIM)
    return jax.nn.gelu(blk @ w1) @ w2


def nsa_attention(q, k_cmp, v_cmp, k_slc, v_slc, k_win, v_win, gates, pe_k, pe_v, wk1, wk2, wv1, wv2):
    B, L = q.shape[:2]
    G, HPG, DK = N_KV_HEADS, HEADS_PER_GROUP, HEAD_DIM
    n_cmp = (L - CMP_LEN) // CMP_STRIDE + 1
    n_sel = L // SEL_BLOCK
    n_qb = L // Q_BLOCK
    top_k = min(SEL_TOPK, n_sel)
    scale = DK ** -0.5

    kc = compress(k_cmp, pe_k, wk1, wk2)
    vc = compress(v_cmp, pe_v, wv1, wv2)
    c_start = jnp.arange(n_cmp) * CMP_STRIDE
    c_end = c_start + CMP_LEN - 1
    s_start = jnp.arange(n_sel) * SEL_BLOCK
    overlap = ((c_start[:, None] < s_start[None, :] + SEL_BLOCK)
               & (c_start[:, None] + CMP_LEN > s_start[None, :])).astype(jnp.float32)

    ks_blk = jnp.transpose(k_slc.reshape(B, n_sel, SEL_BLOCK, G, DK), (0, 3, 1, 2, 4))
    vs_blk = jnp.transpose(v_slc.reshape(B, n_sel, SEL_BLOCK, G, DK), (0, 3, 1, 2, 4))
    kw_pad = jnp.pad(k_win, ((0, 0), (WINDOW, 0), (0, 0), (0, 0)))
    vw_pad = jnp.pad(v_win, ((0, 0), (WINDOW, 0), (0, 0), (0, 0)))
    b_ix = jnp.arange(B)[:, None, None, None]
    g_ix = jnp.arange(G)[None, :, None, None]
    blk_ids = jnp.arange(n_sel)

    q_blocks = jnp.transpose(q.reshape(B, n_qb, Q_BLOCK, G, HPG, DK), (1, 0, 2, 3, 4, 5))
    g_blocks = jnp.transpose(gates.reshape(B, n_qb, Q_BLOCK, G, HPG, 3), (1, 0, 2, 3, 4, 5))

    def one_block(args):
        c, qc, gc = args
        t = c * Q_BLOCK + jnp.arange(Q_BLOCK)
        s = jnp.einsum('bqghd,bcgd->bghqc', qc, kc).astype(jnp.float32) * scale
        p_cmp = masked_softmax(s, c_end[None, :] <= t[:, None])
        o_cmp = jnp.einsum('bghqc,bcgd->bqghd', p_cmp.astype(vc.dtype), vc)
        imp = jnp.einsum('bghqc,cs->bgqs', p_cmp, overlap)
        cur = (t // SEL_BLOCK)[:, None]
        valid = blk_ids[None, :] <= cur
        forced = (blk_ids[None, :] == 0) | (blk_ids[None, :] == cur) | (blk_ids[None, :] == cur - 1)
        score = jnp.where(valid & forced, BIG, jnp.where(valid, imp, -BIG))
        _, idx = lax.top_k(score, top_k)
        k_sel = ks_blk[b_ix, g_ix, idx]
        v_sel = vs_blk[b_ix, g_ix, idx].reshape(B, G, Q_BLOCK, top_k * SEL_BLOCK, DK)
        s = jnp.einsum('bqghd,bgqkrd->bghqkr', qc, k_sel).astype(jnp.float32) * scale
        s = s.reshape(B, G, HPG, Q_BLOCK, top_k * SEL_BLOCK)
        kpos = (idx[..., None] * SEL_BLOCK + jnp.arange(SEL_BLOCK)).reshape(B, G, 1, Q_BLOCK, top_k * SEL_BLOCK)
        p_slc = masked_softmax(s, kpos <= t[:, None])
        o_slc = jnp.einsum('bghqn,bgqnd->bqghd', p_slc.astype(v_sel.dtype), v_sel)
        start = c * Q_BLOCK
        k_w = lax.dynamic_slice_in_dim(kw_pad, start, WINDOW + Q_BLOCK, axis=1)
        v_w = lax.dynamic_slice_in_dim(vw_pad, start, WINDOW + Q_BLOCK, axis=1)
        spos = start - WINDOW + jnp.arange(WINDOW + Q_BLOCK)
        dist = t[:, None] - spos[None, :]
        wmask = (spos[None, :] >= 0) & (dist >= 0) & (dist < WINDOW)
        s = jnp.einsum('bqghd,bsgd->bghqs', qc, k_w).astype(jnp.float32) * scale
        p_win = masked_softmax(s, wmask)
        o_win = jnp.einsum('bghqs,bsgd->bqghd', p_win.astype(v_w.dtype), v_w)
        o = gc[..., 0:1] * o_cmp + gc[..., 1:2] * o_slc + gc[..., 2:3] * o_win
        return o.astype(qc.dtype)

    out = lax.map(one_block, (jnp.arange(n_qb), q_blocks, g_blocks))
    return jnp.transpose(out, (1, 0, 2, 3, 4, 5)).reshape(B, L, NSA_WIDTH)


def s5_mixer(u, a_re, a_im, log_dt, b_re, b_im, c_re, c_im, d_skip, w_glu, b_glu):
    B, L = u.shape[:2]
    f32 = jnp.float32
    uf = u.astype(f32).reshape(B, L, S5_GROUPS, S5_GROUP)
    lam = lax.complex(a_re.astype(f32), a_im.astype(f32))
    dt = jnp.exp(log_dt.astype(f32))[:, None]
    a_bar = jnp.exp(lam * dt)
    b_coef = (a_bar - 1.0) / lam
    bu_re = jnp.einsum('blgi,gni->blgn', uf, b_re.astype(f32))
    bu_im = jnp.einsum('blgi,gni->blgn', uf, b_im.astype(f32))
    bu = lax.complex(bu_re, bu_im) * b_coef
    a_seq = jnp.broadcast_to(a_bar, (1, L) + a_bar.shape)

    def combine(e1, e2):
        a1, x1 = e1
        a2, x2 = e2
        return a2 * a1, a2 * x1 + x2

    _, state = lax.associative_scan(combine, (a_seq, bu), axis=1)
    y = (jnp.einsum('blgn,gon->blgo', jnp.real(state), c_re.astype(f32))
         - jnp.einsum('blgn,gon->blgo', jnp.imag(state), c_im.astype(f32))
         + d_skip.astype(f32) * uf)
    y = jax.nn.gelu(y.reshape(B, L, S5_WIDTH))
    out = y * jax.nn.sigmoid(y @ w_glu.astype(f32) + b_glu.astype(f32))
    return out.astype(u.dtype)


def setup_inputs(seed: int = 0) -> dict:
    key = jax.random.key(seed)
    ks = jax.random.split(key, 40)
    f32 = jnp.float32

    def nrm(k, shape, fan_in):
        return jax.random.normal(k, shape, f32) * (fan_in ** -0.5)

    def gain(k, shape):
        return 1.0 + 0.05 * jax.random.normal(k, shape, f32)

    n_idx = jnp.arange(S5_STATE, dtype=f32)
    return {
        "x": jax.random.normal(ks[0], (BATCH, SEQ, D_MODEL), f32),
        "p": jax.random.normal(ks[1], (DEPTH, BATCH, SEQ, PLE_DIM), f32),
        "norm_ffn1": gain(ks[2], (DEPTH, D_MODEL)),
        "ffn1_w1": nrm(ks[3], (DEPTH, D_MODEL, D_FF), D_MODEL),
        "ffn1_w3": nrm(ks[4], (DEPTH, D_MODEL, D_FF), D_MODEL),
        "ffn1_w2": nrm(ks[5], (DEPTH, D_FF, D_MODEL), D_FF),
        "norm_mix": gain(ks[6], (DEPTH, D_MODEL)),
        "w_in": nrm(ks[7], (DEPTH, D_MODEL, IN_COLS), D_MODEL),
        "cmp_pe_k": 0.1 * jax.random.normal(ks[8], (DEPTH, CMP_LEN, HEAD_DIM), f32),
        "cmp_pe_v": 0.1 * jax.random.normal(ks[9], (DEPTH, CMP_LEN, HEAD_DIM), f32),
        "cmp_wk1": nrm(ks[10], (DEPTH, CMP_LEN * HEAD_DIM, CMP_HIDDEN), CMP_LEN * HEAD_DIM),
        "cmp_wk2": nrm(ks[11], (DEPTH, CMP_HIDDEN, HEAD_DIM), CMP_HIDDEN),
        "cmp_wv1": nrm(ks[12], (DEPTH, CMP_LEN * HEAD_DIM, CMP_HIDDEN), CMP_LEN * HEAD_DIM),
        "cmp_wv2": nrm(ks[13], (DEPTH, CMP_HIDDEN, HEAD_DIM), CMP_HIDDEN),
        "s5_a_re": -0.5 + 0.01 * jax.random.normal(ks[14], (DEPTH, S5_GROUPS, S5_STATE), f32),
        "s5_a_im": jnp.pi * n_idx + 0.01 * jax.random.normal(ks[15], (DEPTH, S5_GROUPS, S5_STATE), f32),
        "s5_log_dt": jax.random.uniform(ks[16], (DEPTH, S5_GROUPS), f32, jnp.log(0.001), jnp.log(0.1)),
        "s5_b_re": nrm(ks[17], (DEPTH, S5_GROUPS, S5_STATE, S5_GROUP), 2 * S5_GROUP),
        "s5_b_im": nrm(ks[18], (DEPTH, S5_GROUPS, S5_STATE, S5_GROUP), 2 * S5_GROUP),
        "s5_c_re": nrm(ks[19], (DEPTH, S5_GROUPS, S5_GROUP, S5_STATE), S5_STATE),
        "s5_c_im": nrm(ks[20], (DEPTH, S5_GROUPS, S5_GROUP, S5_STATE), S5_STATE),
        "s5_d": jax.random.normal(ks[21], (DEPTH, S5_GROUPS, S5_GROUP), f32),
        "s5_w_glu": nrm(ks[22], (DEPTH, S5_WIDTH, S5_WIDTH), S5_WIDTH),
        "s5_b_glu": 0.01 * jax.random.normal(ks[23], (DEPTH, S5_WIDTH), f32),
        "w_out": nrm(ks[24], (DEPTH, D_MODEL, D_MODEL), D_MODEL),
        "norm_ffn2": gain(ks[25], (DEPTH, D_MODEL)),
        "ffn2_w1": nrm(ks[26], (DEPTH, D_MODEL, D_FF), D_MODEL),
        "ffn2_w3": nrm(ks[27], (DEPTH, D_MODEL, D_FF), D_MODEL),
        "ffn2_w2": nrm(ks[28], (DEPTH, D_FF, D_MODEL), D_FF),
        "norm_ple": gain(ks[29], (DEPTH, D_MODEL)),
        "w_ple_gate": nrm(ks[30], (DEPTH, D_MODEL, D_MODEL), D_MODEL),
        "w_ple": nrm(ks[31], (DEPTH, PLE_DIM, D_MODEL), PLE_DIM),
        "norm_final": gain(ks[32], (D_MODEL,)),
    }


def reference(x, p, norm_ffn1, ffn1_w1, ffn1_w3, ffn1_w2, norm_mix, w_in,
              cmp_pe_k, cmp_pe_v, cmp_wk1, cmp_wk2, cmp_wv1, cmp_wv2,
              s5_a_re, s5_a_im, s5_log_dt, s5_b_re, s5_b_im, s5_c_re, s5_c_im, s5_d,
              s5_w_glu, s5_b_glu, w_out, norm_ffn2, ffn2_w1, ffn2_w3, ffn2_w2,
              norm_ple, w_ple_gate, w_ple, norm_final):
    B, L = x.shape[:2]
    pos = jnp.arange(L, dtype=jnp.float32)
    split_at = [NSA_WIDTH + k * KV_WIDTH for k in range(7)] + [NSA_WIDTH + 6 * KV_WIDTH + 3 * N_HEADS]
    h = x
    for i in range(DEPTH):
        h = h + 0.5 * swiglu(rmsnorm(h, norm_ffn1[i]), ffn1_w1[i], ffn1_w3[i], ffn1_w2[i])
        z = rmsnorm(h, norm_mix[i]) @ w_in[i]
        q, kc, vc, ksl, vsl, kw, vw, g, u = jnp.split(z, split_at, axis=-1)
        kvshape = (B, L, N_KV_HEADS, HEAD_DIM)
        q = rope(q.reshape(B, L, N_HEADS, HEAD_DIM), pos)
        kc = rope(kc.reshape(kvshape), pos)
        ksl = rope(ksl.reshape(kvshape), pos)
        kw = rope(kw.reshape(kvshape), pos)
        gates = jax.nn.sigmoid(g.reshape(B, L, N_HEADS, 3))
        o_nsa = nsa_attention(q, kc, vc.reshape(kvshape), ksl, vsl.reshape(kvshape), kw, vw.reshape(kvshape),
                              gates, cmp_pe_k[i], cmp_pe_v[i], cmp_wk1[i], cmp_wk2[i], cmp_wv1[i], cmp_wv2[i])
        o_s5 = s5_mixer(u, s5_a_re[i], s5_a_im[i], s5_log_dt[i], s5_b_re[i], s5_b_im[i],
                        s5_c_re[i], s5_c_im[i], s5_d[i], s5_w_glu[i], s5_b_glu[i])
        h = h + jnp.concatenate([o_nsa, o_s5], axis=-1) @ w_out[i]
        h = h + 0.5 * swiglu(rmsnorm(h, norm_ffn2[i]), ffn2_w1[i], ffn2_w3[i], ffn2_w2[i])
        gate = jax.nn.sigmoid(rmsnorm(h, norm_ple[i]) @ w_ple_gate[i])
        h = h + gate * (p[i] @ w_ple[i])
    return rmsnorm(h, norm_final)
```

```python
import functools

import jax
import jax.numpy as jnp
import numpy as np
from jax import lax
from jax.experimental import pallas as pl
from jax.experimental.pallas import tpu as pltpu

D_MODEL = 1024
PLE_DIM = 256
D_FF = 2816
N_HEADS = 8
N_KV_HEADS = 2
HEAD_DIM = 64
HEADS_PER_GROUP = N_HEADS // N_KV_HEADS
NSA_WIDTH = N_HEADS * HEAD_DIM
KV_WIDTH = N_KV_HEADS * HEAD_DIM
S5_WIDTH = D_MODEL - NSA_WIDTH
S5_GROUP = 16
S5_GROUPS = S5_WIDTH // S5_GROUP
S5_STATE = 64
S5_COLS = S5_GROUPS * S5_STATE
CMP_LEN = 32
CMP_STRIDE = 16
CMP_HIDDEN = 256
SEL_BLOCK = 64
SEL_TOPK = 16
WINDOW = 512
ROPE_THETA = 10000.0
RMS_EPS = 1e-6
NEG = -1e30
BIG = 1e9

LANES = 128
MAX_SEL_BLOCKS = LANES
VMEM_LIMIT = 52 * 1024 * 1024

TM_DENSE = 512
FF_CHUNK = 256
TQ = 128
TK_SEL = 512
S5_TILE = 256
S5_CHUNK = 64

F32 = jnp.float32
BF16 = jnp.bfloat16


def _dot(a, b):
    return jnp.dot(a, b, preferred_element_type=F32)


def _dot_nt(a, b):
    return lax.dot_general(a, b, (((1,), (1,)), ((), ())), preferred_element_type=F32)


def _rmsnorm(x, g):
    ms = jnp.mean(x * x, axis=-1, keepdims=True)
    return x * lax.rsqrt(ms + RMS_EPS) * g


def _gelu_tanh(x):
    return 0.5 * x * (1.0 + jnp.tanh(np.sqrt(2.0 / np.pi).astype(np.float32) * (x + 0.044715 * (x * x * x))))


def _params(n_grid):
    return pltpu.CompilerParams(dimension_semantics=("arbitrary",) * n_grid, vmem_limit_bytes=VMEM_LIMIT)


def _resident():
    return pl.BlockSpec(memory_space=pltpu.VMEM)


def _ffn_body(*refs, with_mix):
    if with_mix:
        h_ref, on_ref, os_ref, won_ref, wos_ref, g_ref, w1_ref, w3_ref, w2_ref, o_ref = refs
        h = h_ref[...] + _dot(on_ref[...], won_ref[...]) + _dot(os_ref[...], wos_ref[...])
    else:
        h_ref, g_ref, w1_ref, w3_ref, w2_ref, o_ref = refs
        h = h_ref[...]
    xn = _rmsnorm(h, g_ref[...]).astype(BF16)
    acc = jnp.zeros(h.shape, F32)
    for j in range(D_FF // FF_CHUNK):
        sl = slice(j * FF_CHUNK, (j + 1) * FF_CHUNK)
        a = _dot(xn, w1_ref[:, sl])
        b = _dot(xn, w3_ref[:, sl])
        act = (a * jax.nn.sigmoid(a) * b).astype(BF16)
        acc = acc + _dot(act, w2_ref[sl, :])
    o_ref[...] = h + 0.5 * acc


def _ffn(h, g, w1, w3, w2, mix=None):
    t = h.shape[0]
    tm = min(TM_DENSE, t)
    row = lambda w: pl.BlockSpec((tm, w), lambda i: (i, 0))
    args, specs = [h], [row(D_MODEL)]
    if mix is not None:
        o_nsa, o_s5, wo_n, wo_s = mix
        args += [o_nsa, o_s5, wo_n, wo_s]
        specs += [row(NSA_WIDTH), row(S5_WIDTH), _resident(), _resident()]
    args += [g, w1, w3, w2]
    specs += [_resident()] * 4
    return pl.pallas_call(
        functools.partial(_ffn_body, with_mix=mix is not None),
        grid=(t // tm,),
        in_specs=specs,
        out_specs=row(D_MODEL),
        out_shape=jax.ShapeDtypeStruct((t, D_MODEL), F32),
        compiler_params=_params(1),
        name="ffn_mix" if mix is not None else "ffn",
    )(*args)


_C_Q = 0
_C_CMP = NSA_WIDTH
_C_SLC = _C_CMP + 2 * KV_WIDTH
_C_WIN = _C_SLC + 2 * KV_WIDTH
_C_U = _C_WIN + 2 * KV_WIDTH
_C_G = _C_U + S5_WIDTH
_C_END = _C_G + N_KV_HEADS * LANES


def _rope_pair(x, cos, sin_signed, first_half):
    fwd = pltpu.roll(x, HEAD_DIM // 2, 1)
    bwd = pltpu.roll(x, LANES - HEAD_DIM // 2, 1)
    return x * cos + jnp.where(first_half, bwd, fwd) * sin_signed


def _proj_body(h_ref, g_ref, w_ref, cos_ref, sin_ref,
               q_ref, kc_ref, vc_ref, ks_ref, vs_ref, kw_ref, vw_ref, gt_ref, u_ref, *, tiles_per_seq):
    tm = h_ref.shape[0]
    xn = _rmsnorm(h_ref[...], g_ref[...]).astype(BF16)
    cos = cos_ref[...]
    sin = sin_ref[...]
    lane = lax.broadcasted_iota(jnp.int32, (tm, LANES), 1)
    first_half = (lane % HEAD_DIM) < (HEAD_DIM // 2)
    rope = lambda x: _rope_pair(x, cos, sin, first_half)

    scale = HEAD_DIM ** -0.5
    for j in range(NSA_WIDTH // 256):
        z = _dot(xn, w_ref[:, _C_Q + 256 * j:_C_Q + 256 * (j + 1)])
        for k in range(2):
            r = (rope(z[:, LANES * k:LANES * (k + 1)]) * scale).astype(BF16)
            q_ref[4 * j + 2 * k] = r[:, :HEAD_DIM]
            q_ref[4 * j + 2 * k + 1] = r[:, HEAD_DIM:]

    z = _dot(xn, w_ref[:, _C_CMP:_C_CMP + 256])
    kc_ref[...] = rope(z[:, :LANES])
    vc_ref[...] = z[:, LANES:]

    ones = jnp.ones((tm, HEAD_DIM), BF16)
    zeros = jnp.zeros((tm, HEAD_DIM), BF16)
    pos = (pl.program_id(0) % tiles_per_seq) * tm + lax.broadcasted_iota(jnp.int32, (tm, LANES), 0)
    onehot = jnp.where(lane == pos // SEL_BLOCK, 1.0, 0.0).astype(BF16)

    z = _dot(xn, w_ref[:, _C_SLC:_C_SLC + 256])
    k = rope(z[:, :LANES]).astype(BF16)
    v = z[:, LANES:].astype(BF16)
    for g in range(N_KV_HEADS):
        gs = slice(g * HEAD_DIM, (g + 1) * HEAD_DIM)
        ks_ref[g, :, 0:LANES] = onehot
        ks_ref[g, :, LANES:LANES + HEAD_DIM] = k[:, gs]
        ks_ref[g, :, LANES + HEAD_DIM:] = zeros
        vs_ref[g, :, 0:HEAD_DIM] = v[:, gs]
        vs_ref[g, :, HEAD_DIM:] = ones

    z = _dot(xn, w_ref[:, _C_WIN:_C_WIN + 256])
    k = rope(z[:, :LANES]).astype(BF16)
    v = z[:, LANES:].astype(BF16)
    for g in range(N_KV_HEADS):
        gs = slice(g * HEAD_DIM, (g + 1) * HEAD_DIM)
        kw_ref[g] = k[:, gs]
        vw_ref[g, :, 0:HEAD_DIM] = v[:, gs]
        vw_ref[g, :, HEAD_DIM:] = ones

    for j in range(S5_WIDTH // 256):
        u_ref[:, 256 * j:256 * (j + 1)] = _dot(xn, w_ref[:, _C_U + 256 * j:_C_U + 256 * (j + 1)])
    gt_ref[...] = jax.nn.sigmoid(_dot(xn, w_ref[:, _C_G:_C_END]))


def _proj(h1, g, w_in_r, cos, sin_signed, batch, seq):
    t = batch * seq
    tm = min(TM_DENSE, seq)
    n = seq // tm
    bl = lambda w: pl.BlockSpec((None, tm, w), lambda i: (i // n, i % n, 0))
    bgl = lambda w: pl.BlockSpec((None, N_KV_HEADS, tm, w), lambda i: (i // n, 0, i % n, 0))
    tab = pl.BlockSpec((tm, LANES), lambda i: (i % n, 0))
    sds = jax.ShapeDtypeStruct
    return pl.pallas_call(
        functools.partial(_proj_body, tiles_per_seq=n),
        grid=(t // tm,),
        in_specs=[pl.BlockSpec((tm, D_MODEL), lambda i: (i, 0)), _resident(), _resident(), tab, tab],
        out_specs=[
            pl.BlockSpec((None, N_HEADS, tm, HEAD_DIM), lambda i: (i // n, 0, i % n, 0)),
            bl(KV_WIDTH), bl(KV_WIDTH),
            bgl(2 * LANES), bgl(LANES), bgl(HEAD_DIM), bgl(LANES),
            bl(N_KV_HEADS * LANES), bl(S5_WIDTH),
        ],
        out_shape=[
            sds((batch, N_HEADS, seq, HEAD_DIM), BF16),
            sds((batch, seq, KV_WIDTH), F32), sds((batch, seq, KV_WIDTH), F32),
            sds((batch, N_KV_HEADS, seq, 2 * LANES), BF16), sds((batch, N_KV_HEADS, seq, LANES), BF16),
            sds((batch, N_KV_HEADS, seq, HEAD_DIM), BF16), sds((batch, N_KV_HEADS, seq, LANES), BF16),
            sds((batch, seq, N_KV_HEADS * LANES), F32), sds((batch, seq, S5_WIDTH), F32),
        ],
        compiler_params=_params(1),
        name="proj",
    )(h1.reshape(t, D_MODEL), g, w_in_r, cos, sin_signed)


def _compress_body(kc_ref, vc_ref, pek_ref, pev_ref, wkt_ref, wkb_ref, wvt_ref, wvb_ref, wk2_ref, wv2_ref,
                   ko_ref, vo_ref):
    nc = kc_ref.shape[0]

    def hidden(x, pe_ref, wt_ref, wb_ref):
        top = _dot((x + pe_ref[0:1, :]).astype(BF16), wt_ref[...])
        bot = _dot((x + pe_ref[1:2, :]).astype(BF16), wb_ref[...])
        pre = top + pltpu.roll(bot, nc - 1, 0)
        return _gelu_tanh(pre).astype(BF16)

    hk = hidden(kc_ref[...], pek_ref, wkt_ref, wkb_ref)
    hv = hidden(vc_ref[...], pev_ref, wvt_ref, wvb_ref)
    ones = jnp.ones((nc, HEAD_DIM), BF16)
    for g in range(N_KV_HEADS):
        gs = slice(g * CMP_HIDDEN, (g + 1) * CMP_HIDDEN)
        ko_ref[g] = _dot(hk[:, gs], wk2_ref[...]).astype(BF16)
        vo_ref[g, :, 0:HEAD_DIM] = _dot(hv[:, gs], wv2_ref[...]).astype(BF16)
        vo_ref[g, :, HEAD_DIM:] = ones


def _compress(kc, vc, pe_k2, pe_v2, wkt, wkb, wvt, wvb, wk2, wv2):
    batch, seq, _ = kc.shape
    nc = seq // CMP_STRIDE
    width = CMP_STRIDE * KV_WIDTH
    x_spec = pl.BlockSpec((None, nc, width), lambda b: (b, 0, 0))
    sds = jax.ShapeDtypeStruct
    return pl.pallas_call(
        _compress_body,
        grid=(batch,),
        in_specs=[x_spec, x_spec] + [_resident()] * 8,
        out_specs=[pl.BlockSpec((None, N_KV_HEADS, nc, HEAD_DIM), lambda b: (b, 0, 0, 0)),
                   pl.BlockSpec((None, N_KV_HEADS, nc, LANES), lambda b: (b, 0, 0, 0))],
        out_shape=[sds((batch, N_KV_HEADS, nc, HEAD_DIM), BF16), sds((batch, N_KV_HEADS, nc, LANES), BF16)],
        compiler_params=_params(1),
        name="compress",
    )(kc.reshape(batch, nc, width), vc.reshape(batch, nc, width), pe_k2, pe_v2, wkt, wkb, wvt, wvb, wk2, wv2)


def _stack_heads(col):
    return jnp.concatenate([col] * HEADS_PER_GROUP, axis=0)


def _cmpwin_body(q_ref, kc_ref, vc_ref, ov_ref, kw_ref, vw_ref, gt_ref, op_ref, bias_ref):
    tq = q_ref.shape[1]
    rows = HEADS_PER_GROUP * tq
    nc = kc_ref.shape[0]
    i = pl.program_id(2)
    q2 = q_ref[...].reshape(rows, HEAD_DIM)
    t_col = i * tq + lax.broadcasted_iota(jnp.int32, (tq, 1), 0)
    t_rows = _stack_heads(t_col)

    s = _dot_nt(q2, kc_ref[...])
    c_end = lax.broadcasted_iota(jnp.int32, (rows, nc), 1) * CMP_STRIDE + (CMP_LEN - 1)
    mask = c_end <= t_rows
    s = jnp.where(mask, s, NEG)
    m = jnp.max(s, axis=-1, keepdims=True)
    e = jnp.where(mask, jnp.exp(s - m), 0.0)
    p = e / jnp.maximum(jnp.sum(e, axis=-1, keepdims=True), 1e-30)
    o_cmp = _dot(p.astype(BF16), vc_ref[:, 0:HEAD_DIM])

    psum = p[0:tq]
    for h in range(1, HEADS_PER_GROUP):
        psum = psum + p[h * tq:(h + 1) * tq]
    p_hi = psum.astype(BF16)
    p_lo = (psum - p_hi.astype(F32)).astype(BF16)
    imp = _dot(p_hi, ov_ref[...]) + _dot(p_lo, ov_ref[...])

    imp_t = imp.T
    blk = lax.broadcasted_iota(jnp.int32, (MAX_SEL_BLOCKS, tq), 0)
    t_lane = i * tq + lax.broadcasted_iota(jnp.int32, (MAX_SEL_BLOCKS, tq), 1)
    behind = t_lane // SEL_BLOCK - blk
    score = jnp.where(blk == 0, BIG, jnp.where(behind == 0, BIG, jnp.where(behind == 1, BIG, imp_t)))
    score = jnp.where(behind >= 0, score, -BIG)
    chosen = jnp.zeros((MAX_SEL_BLOCKS, tq), F32)
    for _ in range(SEL_TOPK):
        best = jnp.max(score, axis=0, keepdims=True)
        first = jnp.min(jnp.where(score == best, blk, MAX_SEL_BLOCKS), axis=0, keepdims=True)
        hit = blk == first
        chosen = jnp.where(hit, 1.0, chosen)
        score = jnp.where(hit, -jnp.inf, score)
    bias_t = jnp.where(chosen > 0.0, jnp.where(behind >= 0, 0.0, NEG), NEG)
    bias_ref[...] = bias_t.T.astype(BF16)

    span = WINDOW + tq
    start = pl.multiple_of(jnp.maximum(i * tq - WINDOW, 0), tq)
    s = _dot_nt(q2, kw_ref[pl.ds(start, span), :])
    kpos = start + lax.broadcasted_iota(jnp.int32, (rows, span), 1)
    dist = lax.bitcast_convert_type(t_rows - kpos, jnp.uint32)
    s = jnp.where(dist < WINDOW, s, NEG)
    m = jnp.max(s, axis=-1, keepdims=True)
    acc = _dot(jnp.exp(s - m).astype(BF16), vw_ref[pl.ds(start, span), :])
    o_win = acc[:, 0:HEAD_DIM] / acc[:, HEAD_DIM:HEAD_DIM + 1]

    gates = gt_ref[...]
    for h in range(HEADS_PER_GROUP):
        hs = slice(h * tq, (h + 1) * tq)
        op_ref[:, h * HEAD_DIM:(h + 1) * HEAD_DIM] = (gates[:, 3 * h:3 * h + 1] * o_cmp[hs]
                                                      + gates[:, 3 * h + 2:3 * h + 3] * o_win[hs])


def _cmpwin(q, kcmp, vcmp, overlap, kwin, vwin, gates):
    batch, _, seq, _ = q.shape
    nc = kcmp.shape[2]
    tq = TQ
    per_bg = lambda n, w: pl.BlockSpec((None, None, n, w), lambda b, g, i: (b, g, 0, 0))
    sds = jax.ShapeDtypeStruct
    return pl.pallas_call(
        _cmpwin_body,
        grid=(batch, N_KV_HEADS, seq // tq),
        in_specs=[
            pl.BlockSpec((None, HEADS_PER_GROUP, tq, HEAD_DIM), lambda b, g, i: (b, g, i, 0)),
            per_bg(nc, HEAD_DIM), per_bg(nc, LANES), _resident(),
            per_bg(seq, HEAD_DIM), per_bg(seq, LANES),
            pl.BlockSpec((None, tq, LANES), lambda b, g, i: (b, i, g)),
        ],
        out_specs=[pl.BlockSpec((None, tq, HEADS_PER_GROUP * HEAD_DIM), lambda b, g, i: (b, i, g)),
                   pl.BlockSpec((None, None, tq, LANES), lambda b, g, i: (b, g, i, 0))],
        out_shape=[sds((batch, seq, NSA_WIDTH), F32), sds((batch, N_KV_HEADS, seq, LANES), BF16)],
        compiler_params=_params(3),
        name="cmpwin",
    )(q, kcmp, vcmp, overlap, kwin, vwin, gates)


def _select_body(q_ref, bias_ref, ks_ref, vs_ref, gt_ref, op_ref, o_ref, qa_ref, m_ref, acc_ref):
    tq = q_ref.shape[1]
    rows = HEADS_PER_GROUP * tq
    tk = TK_SEL
    i = pl.program_id(2)

    bias = bias_ref[...]
    for h in range(HEADS_PER_GROUP):
        qa_ref[h * tq:(h + 1) * tq, 0:LANES] = bias
    qa_ref[:, LANES:LANES + HEAD_DIM] = q_ref[...].reshape(rows, HEAD_DIM)
    qa_ref[:, LANES + HEAD_DIM:] = jnp.zeros((rows, LANES - HEAD_DIM), BF16)
    m_ref[...] = jnp.full(m_ref.shape, NEG, F32)
    acc_ref[...] = jnp.zeros(acc_ref.shape, F32)

    def step(c, causal):
        off = pl.multiple_of(c * tk, tk)
        s = _dot_nt(qa_ref[...], ks_ref[pl.ds(off, tk), :])
        if causal:
            t_rows = _stack_heads(i * tq + lax.broadcasted_iota(jnp.int32, (tq, 1), 0))
            kpos = off + lax.broadcasted_iota(jnp.int32, (rows, tk), 1)
            s = jnp.where(kpos <= t_rows, s, NEG)
        m_old = m_ref[...]
        m_new = jnp.maximum(m_old, jnp.max(s, axis=-1, keepdims=True))
        p = jnp.exp(s - m_new).astype(BF16)
        acc_ref[...] = jnp.exp(m_old - m_new) * acc_ref[...] + _dot(p, vs_ref[pl.ds(off, tk), :])
        m_ref[...] = m_new

    n_full = (i * tq) // tk
    lax.fori_loop(0, n_full, lambda c, carry: (step(c, False), carry)[1], 0)
    step(n_full, True)

    acc = acc_ref[...]
    o_slc = acc[:, 0:HEAD_DIM] / acc[:, HEAD_DIM:HEAD_DIM + 1]
    gates = gt_ref[...]
    part = op_ref[...]
    for h in range(HEADS_PER_GROUP):
        hd = slice(h * HEAD_DIM, (h + 1) * HEAD_DIM)
        o_ref[:, hd] = (part[:, hd] + gates[:, 3 * h + 1:3 * h + 2] * o_slc[h * tq:(h + 1) * tq]).astype(BF16)


def _select(q, bias, ksel, vsel, gates, o_part):
    batch, _, seq, _ = q.shape
    tq = TQ
    rows = HEADS_PER_GROUP * tq
    per_bg = lambda n, w: pl.BlockSpec((None, None, n, w), lambda b, g, i: (b, g, 0, 0))
    tile_g = lambda w: pl.BlockSpec((None, tq, w), lambda b, g, i: (b, i, g))
    return pl.pallas_call(
        _select_body,
        grid=(batch, N_KV_HEADS, seq // tq),
        in_specs=[
            pl.BlockSpec((None, HEADS_PER_GROUP, tq, HEAD_DIM), lambda b, g, i: (b, g, i, 0)),
            pl.BlockSpec((None, None, tq, LANES), lambda b, g, i: (b, g, i, 0)),
            per_bg(seq, 2 * LANES), per_bg(seq, LANES),
            tile_g(LANES), tile_g(HEADS_PER_GROUP * HEAD_DIM),
        ],
        out_specs=tile_g(HEADS_PER_GROUP * HEAD_DIM),
        out_shape=jax.ShapeDtypeStruct((batch, seq, NSA_WIDTH), BF16),
        scratch_shapes=[pltpu.VMEM((rows, 2 * LANES), BF16), pltpu.VMEM((rows, 1), F32),
                        pltpu.VMEM((rows, LANES), F32)],
        compiler_params=_params(3),
        name="select",
    )(q, bias, ksel, vsel, gates, o_part)


def _s5_body(u_ref, bt_ref, pw_ref, a1_ref, ltri_ref, ct_ref, d_ref, wg_ref, bg_ref, o_ref, c_ref, h_ref):
    @pl.when(pl.program_id(1) == 0)
    def _():
        c_ref[...] = jnp.zeros(c_ref.shape, F32)

    n = S5_COLS
    u = u_ref[...]
    bu = _dot(u.astype(BF16), bt_ref[...])
    p_re, p_im, q_re, q_im = pw_ref[0], pw_ref[1], pw_ref[2], pw_ref[3]
    a_re, a_im = a1_ref[0:1, :], a1_ref[1:2, :]
    ltri = ltri_ref[...]

    def cumsum(x):
        hi = x.astype(BF16)
        lo = (x - hi.astype(F32)).astype(BF16)
        return _dot(ltri, hi) + _dot(ltri, lo)

    c_re, c_im = c_ref[0:1, :], c_ref[1:2, :]
    for j in range(u.shape[0] // S5_CHUNK):
        r = slice(j * S5_CHUNK, (j + 1) * S5_CHUNK)
        b_re, b_im = bu[r, 0:n], bu[r, n:2 * n]
        s_re = cumsum(q_re * b_re - q_im * b_im) + c_re
        s_im = cumsum(q_re * b_im + q_im * b_re) + c_im
        h_re = p_re * s_re - p_im * s_im
        h_im = p_re * s_im + p_im * s_re
        l_re, l_im = h_re[S5_CHUNK - 1:S5_CHUNK], h_im[S5_CHUNK - 1:S5_CHUNK]
        c_re = a_re * l_re - a_im * l_im
        c_im = a_re * l_im + a_im * l_re
        h_ref[r, 0:n] = h_re.astype(BF16)
        h_ref[r, n:2 * n] = h_im.astype(BF16)
    c_ref[0:1, :] = c_re
    c_ref[1:2, :] = c_im

    y = _gelu_tanh(_dot(h_ref[...], ct_ref[...]) + d_ref[...] * u)
    o_ref[...] = (y * jax.nn.sigmoid(_dot(y.astype(BF16), wg_ref[...]) + bg_ref[...])).astype(BF16)


def _s5(u, bt, powers, a1, ltri, ct, d, w_glu, b_glu):
    batch, seq, _ = u.shape
    ts = min(S5_TILE, seq)
    tile = pl.BlockSpec((None, ts, S5_WIDTH), lambda b, j: (b, j, 0))
    return pl.pallas_call(
        _s5_body,
        grid=(batch, seq // ts),
        in_specs=[tile] + [_resident()] * 8,
        out_specs=tile,
        out_shape=jax.ShapeDtypeStruct((batch, seq, S5_WIDTH), BF16),
        scratch_shapes=[pltpu.VMEM((2, S5_COLS), F32), pltpu.VMEM((ts, 2 * S5_COLS), BF16)],
        compiler_params=_params(2),
        name="s5",
    )(u, bt, powers, a1, ltri, ct, d, w_glu, b_glu)


def _ple_body(*refs, final_norm):
    if final_norm:
        h_ref, gp_ref, wg_ref, p_ref, wp_ref, gf_ref, o_ref = refs
    else:
        h_ref, gp_ref, wg_ref, p_ref, wp_ref, o_ref = refs
    h = h_ref[...]
    gate = jax.nn.sigmoid(_dot(_rmsnorm(h, gp_ref[...]).astype(BF16), wg_ref[...]))
    h = h + gate * _dot(p_ref[...].astype(BF16), wp_ref[...])
    o_ref[...] = _rmsnorm(h, gf_ref[...]) if final_norm else h


def _ple(h, g_ple, w_gate, p, w_ple, g_final=None):
    t = h.shape[0]
    tm = min(TM_DENSE, t)
    row = lambda w: pl.BlockSpec((tm, w), lambda i: (i, 0))
    args = [h, g_ple, w_gate, p, w_ple]
    specs = [row(D_MODEL), _resident(), _resident(), row(PLE_DIM), _resident()]
    if g_final is not None:
        args.append(g_final)
        specs.append(_resident())
    return pl.pallas_call(
        functools.partial(_ple_body, final_norm=g_final is not None),
        grid=(t // tm,),
        in_specs=specs,
        out_specs=row(D_MODEL),
        out_shape=jax.ShapeDtypeStruct((t, D_MODEL), F32),
        compiler_params=_params(1),
        name="ple",
    )(*args)


def _rearrange_w_in(w_in):
    g0 = NSA_WIDTH + 6 * KV_WIDTH
    per_group = 3 * HEADS_PER_GROUP
    pad = jnp.zeros((D_MODEL, LANES - per_group), w_in.dtype)
    gate_cols = []
    for g in range(N_KV_HEADS):
        gate_cols += [w_in[:, g0 + g * per_group:g0 + (g + 1) * per_group], pad]
    u0 = g0 + 3 * N_HEADS
    return jnp.concatenate([w_in[:, :g0], w_in[:, u0:]] + gate_cols, axis=1).astype(BF16)


def _rope_tables(seq):
    half = HEAD_DIM // 2
    inv = ROPE_THETA ** (-jnp.arange(half, dtype=F32) / half)
    ang = jnp.arange(seq, dtype=F32)[:, None] * inv[None, :]
    cos = jnp.tile(jnp.cos(ang), (1, LANES // half))
    sin = jnp.tile(jnp.concatenate([-jnp.sin(ang), jnp.sin(ang)], axis=1), (1, LANES // HEAD_DIM))
    return cos, sin


def _compress_weights(pe, w1, w2):
    w1 = w1.reshape(2, CMP_STRIDE, HEAD_DIM, CMP_HIDDEN)
    eye = jnp.eye(N_KV_HEADS, dtype=w1.dtype)
    big = jnp.einsum("sjdh,ge->sjgdeh", w1, eye).reshape(2, CMP_STRIDE * KV_WIDTH, N_KV_HEADS * CMP_HIDDEN)
    pe2 = jnp.broadcast_to(pe.reshape(2, CMP_STRIDE, 1, HEAD_DIM), (2, CMP_STRIDE, N_KV_HEADS, HEAD_DIM))
    return pe2.reshape(2, CMP_STRIDE * KV_WIDTH), big[0].astype(BF16), big[1].astype(BF16), w2.astype(BF16)


def _overlap_matrix(n_cmp_rows):
    c_start = np.arange(n_cmp_rows)[:, None] * CMP_STRIDE
    s_start = np.arange(MAX_SEL_BLOCKS)[None, :] * SEL_BLOCK
    ov = (c_start < s_start + SEL_BLOCK) & (c_start + CMP_LEN > s_start)
    return jnp.asarray(ov, dtype=BF16)


def _s5_tables(a_re, a_im, log_dt, b_re, b_im, c_re, c_im):
    dt = jnp.exp(log_dt)[:, None]
    lr, li = a_re * dt, a_im * dt
    mag = jnp.exp(lr)
    ab_re, ab_im = mag * jnp.cos(li), mag * jnp.sin(li)
    den = a_re * a_re + a_im * a_im
    nr, ni = ab_re - 1.0, ab_im
    bc_re = (nr * a_re + ni * a_im) / den
    bc_im = (ni * a_re - nr * a_im) / den
    bt_re = b_re * bc_re[..., None] - b_im * bc_im[..., None]
    bt_im = b_re * bc_im[..., None] + b_im * bc_re[..., None]
    eye = jnp.eye(S5_GROUPS, dtype=F32)
    blockdiag_in = lambda w: jnp.einsum("gni,ge->gien", w, eye).reshape(S5_WIDTH, S5_COLS)
    bt = jnp.concatenate([blockdiag_in(bt_re), blockdiag_in(bt_im)], axis=1).astype(BF16)
    blockdiag_out = lambda w: jnp.einsum("gon,ge->gneo", w, eye).reshape(S5_COLS, S5_WIDTH)
    ct = jnp.concatenate([blockdiag_out(c_re), -blockdiag_out(c_im)], axis=0).astype(BF16)
    k = jnp.arange(S5_CHUNK, dtype=F32)[:, None]
    lr, li = lr.reshape(1, S5_COLS), li.reshape(1, S5_COLS)
    powers = jnp.stack([jnp.exp(k * lr) * jnp.cos(k * li), jnp.exp(k * lr) * jnp.sin(k * li),
                        jnp.exp(-k * lr) * jnp.cos(k * li), -jnp.exp(-k * lr) * jnp.sin(k * li)])
    a1 = jnp.concatenate([ab_re.reshape(1, S5_COLS), ab_im.reshape(1, S5_COLS)], axis=0)
    ltri = jnp.asarray(np.tril(np.ones((S5_CHUNK, S5_CHUNK))), dtype=BF16)
    return bt, powers, a1, ltri, ct


def kernel(x, p, norm_ffn1, ffn1_w1, ffn1_w3, ffn1_w2, norm_mix, w_in, cmp_pe_k, cmp_pe_v, cmp_wk1, cmp_wk2,
           cmp_wv1, cmp_wv2, s5_a_re, s5_a_im, s5_log_dt, s5_b_re, s5_b_im, s5_c_re, s5_c_im, s5_d, s5_w_glu,
           s5_b_glu, w_out, norm_ffn2, ffn2_w1, ffn2_w3, ffn2_w2, norm_ple, w_ple_gate, w_ple, norm_final):
    batch, seq, _ = x.shape
    depth = p.shape[0]
    t = batch * seq
    assert seq % TK_SEL == 0 and seq % S5_TILE == 0 and seq >= WINDOW + TQ
    assert seq // SEL_BLOCK <= MAX_SEL_BLOCKS and seq // SEL_BLOCK >= SEL_TOPK
    bf = lambda w: w.astype(BF16)
    row = lambda v: v.reshape(1, -1)

    cos, sin_signed = _rope_tables(seq)
    overlap = _overlap_matrix(seq // CMP_STRIDE)
    h = x.reshape(t, D_MODEL)
    for i in range(depth):
        h = _ffn(h, row(norm_ffn1[i]), bf(ffn1_w1[i]), bf(ffn1_w3[i]), bf(ffn1_w2[i]))

        q, kc, vc, ksel, vsel, kwin, vwin, gates, u = _proj(
            h, row(norm_mix[i]), _rearrange_w_in(w_in[i]), cos, sin_signed, batch, seq)
        pe_k2, wkt, wkb, wk2 = _compress_weights(cmp_pe_k[i], cmp_wk1[i], cmp_wk2[i])
        pe_v2, wvt, wvb, wv2 = _compress_weights(cmp_pe_v[i], cmp_wv1[i], cmp_wv2[i])
        kcmp, vcmp = _compress(kc, vc, pe_k2, pe_v2, wkt, wkb, wvt, wvb, wk2, wv2)
        o_part, bias = _cmpwin(q, kcmp, vcmp, overlap, kwin, vwin, gates)
        o_nsa = _select(q, bias, ksel, vsel, gates, o_part)

        bt, powers, a1, ltri, ct = _s5_tables(s5_a_re[i], s5_a_im[i], s5_log_dt[i], s5_b_re[i], s5_b_im[i],
                                              s5_c_re[i], s5_c_im[i])
        o_s5 = _s5(u, bt, powers, a1, ltri, ct, row(s5_d[i]), bf(s5_w_glu[i]), row(s5_b_glu[i]))

        h = _ffn(h, row(norm_ffn2[i]), bf(ffn2_w1[i]), bf(ffn2_w3[i]), bf(ffn2_w2[i]),
                 mix=(o_nsa.reshape(t, NSA_WIDTH), o_s5.reshape(t, S5_WIDTH),
                      bf(w_out[i][:NSA_WIDTH]), bf(w_out[i][NSA_WIDTH:])))
        h = _ple(h, row(norm_ple[i]), bf(w_ple_gate[i]), p[i].reshape(t, PLE_DIM), bf(w_ple[i]),
                 row(norm_final) if i == depth - 1 else None)
    return h.reshape(batch, seq, D_MODEL)
```

```python
import functools

import jax
import jax.numpy as jnp
import numpy as np
from jax import lax
from jax.experimental import pallas as pl
from jax.experimental.pallas import tpu as pltpu

D_MODEL = 1024
PLE_DIM = 256
D_FF = 2816
N_HEADS = 8
N_KV_HEADS = 2
HEAD_DIM = 64
HEADS_PER_GROUP = N_HEADS // N_KV_HEADS
NSA_WIDTH = N_HEADS * HEAD_DIM
KV_WIDTH = N_KV_HEADS * HEAD_DIM
S5_WIDTH = D_MODEL - NSA_WIDTH
S5_GROUP = 16
S5_GROUPS = S5_WIDTH // S5_GROUP
S5_STATE = 64
S5_COLS = S5_GROUPS * S5_STATE
CMP_LEN = 32
CMP_STRIDE = 16
CMP_HIDDEN = 256
SEL_BLOCK = 64
SEL_TOPK = 16
WINDOW = 512
ROPE_THETA = 10000.0
RMS_EPS = 1e-6
NEG = -1e30
BIG = 1e9

LANES = 128
MAX_SEL_BLOCKS = LANES
VMEM_LIMIT = 52 * 1024 * 1024

TM_DENSE = 512
FF_CHUNK = 256
TQ = 128
TQ_SEL = 512
TK_SEL = 512
S5_TILE = 256
S5_CHUNK = 64

F32 = jnp.float32
BF16 = jnp.bfloat16


def _dot(a, b):
    return jnp.dot(a, b, preferred_element_type=F32)


def _dot_nt(a, b):
    return lax.dot_general(a, b, (((1,), (1,)), ((), ())), preferred_element_type=F32)


def _rmsnorm(x, g):
    ms = jnp.mean(x * x, axis=-1, keepdims=True)
    return x * lax.rsqrt(ms + RMS_EPS) * g


def _gelu_tanh(x):
    return 0.5 * x * (1.0 + jnp.tanh(np.sqrt(2.0 / np.pi).astype(np.float32) * (x + 0.044715 * (x * x * x))))


def _params(n_grid):
    return pltpu.CompilerParams(dimension_semantics=("arbitrary",) * n_grid, vmem_limit_bytes=VMEM_LIMIT)


def _resident():
    return pl.BlockSpec(memory_space=pltpu.VMEM)


def _ffn_body(*refs, with_mix):
    if with_mix:
        h_ref, on_ref, os_ref, won_ref, wos_ref, g_ref, w1_ref, w3_ref, w2_ref, o_ref = refs
        h = h_ref[...] + _dot(on_ref[...], won_ref[...]) + _dot(os_ref[...], wos_ref[...])
    else:
        h_ref, g_ref, w1_ref, w3_ref, w2_ref, o_ref = refs
        h = h_ref[...]
    xn = _rmsnorm(h, g_ref[...]).astype(BF16)
    acc = jnp.zeros(h.shape, F32)
    for j in range(D_FF // FF_CHUNK):
        sl = slice(j * FF_CHUNK, (j + 1) * FF_CHUNK)
        a = _dot(xn, w1_ref[:, sl])
        b = _dot(xn, w3_ref[:, sl])
        act = (a * jax.nn.sigmoid(a) * b).astype(BF16)
        acc = acc + _dot(act, w2_ref[sl, :])
    o_ref[...] = h + 0.5 * acc


def _ffn(h, g, w1, w3, w2, mix=None):
    t = h.shape[0]
    tm = min(TM_DENSE, t)
    row = lambda w: pl.BlockSpec((tm, w), lambda i: (i, 0))
    args, specs = [h], [row(D_MODEL)]
    if mix is not None:
        o_nsa, o_s5, wo_n, wo_s = mix
        args += [o_nsa, o_s5, wo_n, wo_s]
        specs += [row(NSA_WIDTH), row(S5_WIDTH), _resident(), _resident()]
    args += [g, w1, w3, w2]
    specs += [_resident()] * 4
    return pl.pallas_call(
        functools.partial(_ffn_body, with_mix=mix is not None),
        grid=(t // tm,),
        in_specs=specs,
        out_specs=row(D_MODEL),
        out_shape=jax.ShapeDtypeStruct((t, D_MODEL), F32),
        compiler_params=_params(1),
        name="ffn_mix" if mix is not None else "ffn",
    )(*args)


_C_Q = 0
_C_CMP = NSA_WIDTH
_C_SLC = _C_CMP + 2 * KV_WIDTH
_C_WIN = _C_SLC + 2 * KV_WIDTH
_C_U = _C_WIN + 2 * KV_WIDTH
_C_G = _C_U + S5_WIDTH
_C_END = _C_G + N_KV_HEADS * LANES


def _rope_pair(x, cos, sin_signed, first_half):
    fwd = pltpu.roll(x, HEAD_DIM // 2, 1)
    bwd = pltpu.roll(x, LANES - HEAD_DIM // 2, 1)
    return x * cos + jnp.where(first_half, bwd, fwd) * sin_signed


def _proj_body(h_ref, g_ref, w_ref, cos_ref, sin_ref,
               q_ref, qt_ref, kc_ref, vc_ref, ks_ref, vst_ref, kw_ref, vw_ref, gt_ref, u_ref, *, tiles_per_seq):
    tm = h_ref.shape[0]
    xn = _rmsnorm(h_ref[...], g_ref[...]).astype(BF16)
    cos = cos_ref[...]
    sin = sin_ref[...]
    lane = lax.broadcasted_iota(jnp.int32, (tm, LANES), 1)
    first_half = (lane % HEAD_DIM) < (HEAD_DIM // 2)
    rope = lambda x: _rope_pair(x, cos, sin, first_half)

    scale = HEAD_DIM ** -0.5
    for j in range(NSA_WIDTH // 256):
        z = _dot(xn, w_ref[:, _C_Q + 256 * j:_C_Q + 256 * (j + 1)])
        for k in range(2):
            r32 = rope(z[:, LANES * k:LANES * (k + 1)]) * scale
            r = r32.astype(BF16)
            rt = r32.T.astype(BF16)
            for e in range(2):
                head = 4 * j + 2 * k + e
                q_ref[head] = r[:, e * HEAD_DIM:(e + 1) * HEAD_DIM]
                qt_ref[head] = rt[e * HEAD_DIM:(e + 1) * HEAD_DIM, :]

    z = _dot(xn, w_ref[:, _C_CMP:_C_CMP + 256])
    kc_ref[...] = rope(z[:, :LANES])
    vc_ref[...] = z[:, LANES:]

    ones = jnp.ones((tm, HEAD_DIM), BF16)
    ones_t = jnp.ones((HEAD_DIM, tm), BF16)
    zeros = jnp.zeros((tm, HEAD_DIM), BF16)
    pos = (pl.program_id(0) % tiles_per_seq) * tm + lax.broadcasted_iota(jnp.int32, (tm, LANES), 0)
    onehot = jnp.where(lane == pos // SEL_BLOCK, 1.0, 0.0).astype(BF16)

    z = _dot(xn, w_ref[:, _C_SLC:_C_SLC + 256])
    k = rope(z[:, :LANES]).astype(BF16)
    vt = z[:, LANES:].T.astype(BF16)
    for g in range(N_KV_HEADS):
        gs = slice(g * HEAD_DIM, (g + 1) * HEAD_DIM)
        ks_ref[g, :, 0:LANES] = onehot
        ks_ref[g, :, LANES:LANES + HEAD_DIM] = k[:, gs]
        ks_ref[g, :, LANES + HEAD_DIM:] = zeros
        vst_ref[g, 0:HEAD_DIM, :] = vt[gs, :]
        vst_ref[g, HEAD_DIM:, :] = ones_t

    z = _dot(xn, w_ref[:, _C_WIN:_C_WIN + 256])
    k = rope(z[:, :LANES]).astype(BF16)
    v = z[:, LANES:].astype(BF16)
    for g in range(N_KV_HEADS):
        gs = slice(g * HEAD_DIM, (g + 1) * HEAD_DIM)
        kw_ref[g] = k[:, gs]
        vw_ref[g, :, 0:HEAD_DIM] = v[:, gs]
        vw_ref[g, :, HEAD_DIM:] = ones

    for j in range(S5_WIDTH // 256):
        u_ref[:, 256 * j:256 * (j + 1)] = _dot(xn, w_ref[:, _C_U + 256 * j:_C_U + 256 * (j + 1)])
    gt_ref[...] = jax.nn.sigmoid(_dot(xn, w_ref[:, _C_G:_C_END]))


def _proj(h1, g, w_in_r, cos, sin_signed, batch, seq):
    t = batch * seq
    tm = TK_SEL
    n = seq // tm
    bl = lambda w: pl.BlockSpec((None, tm, w), lambda i: (i // n, i % n, 0))
    bgl = lambda w: pl.BlockSpec((None, N_KV_HEADS, tm, w), lambda i: (i // n, 0, i % n, 0))
    tab = pl.BlockSpec((tm, LANES), lambda i: (i % n, 0))
    sds = jax.ShapeDtypeStruct
    return pl.pallas_call(
        functools.partial(_proj_body, tiles_per_seq=n),
        grid=(t // tm,),
        in_specs=[pl.BlockSpec((tm, D_MODEL), lambda i: (i, 0)), _resident(), _resident(), tab, tab],
        out_specs=[
            pl.BlockSpec((None, N_HEADS, tm, HEAD_DIM), lambda i: (i // n, 0, i % n, 0)),
            pl.BlockSpec((None, N_HEADS, HEAD_DIM, tm), lambda i: (i // n, 0, 0, i % n)),
            bl(KV_WIDTH), bl(KV_WIDTH),
            bgl(2 * LANES),
            pl.BlockSpec((None, N_KV_HEADS, None, LANES, tm), lambda i: (i // n, 0, i % n, 0, 0)),
            bgl(HEAD_DIM), bgl(LANES),
            bl(N_KV_HEADS * LANES), bl(S5_WIDTH),
        ],
        out_shape=[
            sds((batch, N_HEADS, seq, HEAD_DIM), BF16), sds((batch, N_HEADS, HEAD_DIM, seq), BF16),
            sds((batch, seq, KV_WIDTH), F32), sds((batch, seq, KV_WIDTH), F32),
            sds((batch, N_KV_HEADS, seq, 2 * LANES), BF16), sds((batch, N_KV_HEADS, n, LANES, tm), BF16),
            sds((batch, N_KV_HEADS, seq, HEAD_DIM), BF16), sds((batch, N_KV_HEADS, seq, LANES), BF16),
            sds((batch, seq, N_KV_HEADS * LANES), F32), sds((batch, seq, S5_WIDTH), F32),
        ],
        compiler_params=_params(1),
        name="proj",
    )(h1.reshape(t, D_MODEL), g, w_in_r, cos, sin_signed)


def _compress_body(kc_ref, vc_ref, pek_ref, pev_ref, wkt_ref, wkb_ref, wvt_ref, wvb_ref, wk2_ref, wv2_ref,
                   ko_ref, vo_ref):
    nc = kc_ref.shape[0]

    def hidden(x, pe_ref, wt_ref, wb_ref):
        top = _dot((x + pe_ref[0:1, :]).astype(BF16), wt_ref[...])
        bot = _dot((x + pe_ref[1:2, :]).astype(BF16), wb_ref[...])
        pre = top + pltpu.roll(bot, nc - 1, 0)
        return _gelu_tanh(pre).astype(BF16)

    hk = hidden(kc_ref[...], pek_ref, wkt_ref, wkb_ref)
    hv = hidden(vc_ref[...], pev_ref, wvt_ref, wvb_ref)
    ones = jnp.ones((nc, HEAD_DIM), BF16)
    for g in range(N_KV_HEADS):
        gs = slice(g * CMP_HIDDEN, (g + 1) * CMP_HIDDEN)
        ko_ref[g] = _dot(hk[:, gs], wk2_ref[...]).astype(BF16)
        vo_ref[g, :, 0:HEAD_DIM] = _dot(hv[:, gs], wv2_ref[...]).astype(BF16)
        vo_ref[g, :, HEAD_DIM:] = ones


def _compress(kc, vc, pe_k2, pe_v2, wkt, wkb, wvt, wvb, wk2, wv2):
    batch, seq, _ = kc.shape
    nc = seq // CMP_STRIDE
    width = CMP_STRIDE * KV_WIDTH
    x_spec = pl.BlockSpec((None, nc, width), lambda b: (b, 0, 0))
    sds = jax.ShapeDtypeStruct
    return pl.pallas_call(
        _compress_body,
        grid=(batch,),
        in_specs=[x_spec, x_spec] + [_resident()] * 8,
        out_specs=[pl.BlockSpec((None, N_KV_HEADS, nc, HEAD_DIM), lambda b: (b, 0, 0, 0)),
                   pl.BlockSpec((None, N_KV_HEADS, nc, LANES), lambda b: (b, 0, 0, 0))],
        out_shape=[sds((batch, N_KV_HEADS, nc, HEAD_DIM), BF16), sds((batch, N_KV_HEADS, nc, LANES), BF16)],
        compiler_params=_params(1),
        name="compress",
    )(kc.reshape(batch, nc, width), vc.reshape(batch, nc, width), pe_k2, pe_v2, wkt, wkb, wvt, wvb, wk2, wv2)


def _stack_heads(col):
    return jnp.concatenate([col] * HEADS_PER_GROUP, axis=0)


def _cmpwin_body(q_ref, kc_ref, vc_ref, ov_ref, kw_ref, vw_ref, gt_ref, op_ref, bias_ref):
    tq = q_ref.shape[1]
    rows = HEADS_PER_GROUP * tq
    nc = kc_ref.shape[0]
    i = pl.program_id(2)
    q2 = q_ref[...].reshape(rows, HEAD_DIM)
    t_col = i * tq + lax.broadcasted_iota(jnp.int32, (tq, 1), 0)
    t_rows = _stack_heads(t_col)

    s = _dot_nt(q2, kc_ref[...])
    c_end = lax.broadcasted_iota(jnp.int32, (rows, nc), 1) * CMP_STRIDE + (CMP_LEN - 1)
    mask = c_end <= t_rows
    s = jnp.where(mask, s, NEG)
    m = jnp.max(s, axis=-1, keepdims=True)
    e = jnp.where(mask, jnp.exp(s - m), 0.0)
    p = e / jnp.maximum(jnp.sum(e, axis=-1, keepdims=True), 1e-30)
    o_cmp = _dot(p.astype(BF16), vc_ref[:, 0:HEAD_DIM])

    psum = p[0:tq]
    for h in range(1, HEADS_PER_GROUP):
        psum = psum + p[h * tq:(h + 1) * tq]
    p_hi = psum.astype(BF16)
    p_lo = (psum - p_hi.astype(F32)).astype(BF16)
    imp = _dot(p_hi, ov_ref[...]) + _dot(p_lo, ov_ref[...])

    imp_t = imp.T
    blk = lax.broadcasted_iota(jnp.int32, (MAX_SEL_BLOCKS, tq), 0)
    t_lane = i * tq + lax.broadcasted_iota(jnp.int32, (MAX_SEL_BLOCKS, tq), 1)
    behind = t_lane // SEL_BLOCK - blk
    score = jnp.where(blk == 0, BIG, jnp.where(behind == 0, BIG, jnp.where(behind == 1, BIG, imp_t)))
    score = jnp.where(behind >= 0, score, -BIG)
    chosen = jnp.zeros((MAX_SEL_BLOCKS, tq), F32)
    for _ in range(SEL_TOPK):
        best = jnp.max(score, axis=0, keepdims=True)
        first = jnp.min(jnp.where(score == best, blk, MAX_SEL_BLOCKS), axis=0, keepdims=True)
        hit = blk == first
        chosen = jnp.where(hit, 1.0, chosen)
        score = jnp.where(hit, -jnp.inf, score)
    bias_ref[...] = jnp.where(chosen > 0.0, jnp.where(behind >= 0, 0.0, NEG), NEG).astype(BF16)

    span = WINDOW + tq
    start = pl.multiple_of(jnp.maximum(i * tq - WINDOW, 0), tq)
    s = _dot_nt(q2, kw_ref[pl.ds(start, span), :])
    kpos = start + lax.broadcasted_iota(jnp.int32, (rows, span), 1)
    dist = lax.bitcast_convert_type(t_rows - kpos, jnp.uint32)
    s = jnp.where(dist < WINDOW, s, NEG)
    m = jnp.max(s, axis=-1, keepdims=True)
    acc = _dot(jnp.exp(s - m).astype(BF16), vw_ref[pl.ds(start, span), :])
    o_win = acc[:, 0:HEAD_DIM] / acc[:, HEAD_DIM:HEAD_DIM + 1]

    gates = gt_ref[...]
    for h in range(HEADS_PER_GROUP):
        hs = slice(h * tq, (h + 1) * tq)
        op_ref[:, h * HEAD_DIM:(h + 1) * HEAD_DIM] = (gates[:, 3 * h:3 * h + 1] * o_cmp[hs]
                                                      + gates[:, 3 * h + 2:3 * h + 3] * o_win[hs])


def _cmpwin(q, kcmp, vcmp, overlap, kwin, vwin, gates):
    batch, _, seq, _ = q.shape
    nc = kcmp.shape[2]
    tq = TQ
    per_bg = lambda n, w: pl.BlockSpec((None, None, n, w), lambda b, g, i: (b, g, 0, 0))
    sds = jax.ShapeDtypeStruct
    return pl.pallas_call(
        _cmpwin_body,
        grid=(batch, N_KV_HEADS, seq // tq),
        in_specs=[
            pl.BlockSpec((None, HEADS_PER_GROUP, tq, HEAD_DIM), lambda b, g, i: (b, g, i, 0)),
            per_bg(nc, HEAD_DIM), per_bg(nc, LANES), _resident(),
            per_bg(seq, HEAD_DIM), per_bg(seq, LANES),
            pl.BlockSpec((None, tq, LANES), lambda b, g, i: (b, i, g)),
        ],
        out_specs=[pl.BlockSpec((None, tq, HEADS_PER_GROUP * HEAD_DIM), lambda b, g, i: (b, i, g)),
                   pl.BlockSpec((None, None, MAX_SEL_BLOCKS, tq), lambda b, g, i: (b, g, 0, i))],
        out_shape=[sds((batch, seq, NSA_WIDTH), F32), sds((batch, N_KV_HEADS, MAX_SEL_BLOCKS, seq), BF16)],
        compiler_params=_params(3),
        name="cmpwin",
    )(q, kcmp, vcmp, overlap, kwin, vwin, gates)


def _select_body(qt_ref, bias_ref, ks_ref, vst_ref, gt_ref, op_ref, o_ref, qa_ref, s_ref, mx_ref, m_ref, acc_ref):
    tq = qt_ref.shape[2]
    cols = HEADS_PER_GROUP * tq
    tk = TK_SEL
    i = pl.program_id(2)

    bias = bias_ref[...]
    for h in range(HEADS_PER_GROUP):
        qa_ref[0:LANES, h * tq:(h + 1) * tq] = bias
        qa_ref[LANES:LANES + HEAD_DIM, h * tq:(h + 1) * tq] = qt_ref[h]
    qa_ref[LANES + HEAD_DIM:, :] = jnp.zeros((LANES - HEAD_DIM, cols), BF16)
    m_ref[...] = jnp.full(m_ref.shape, NEG, F32)
    acc_ref[...] = jnp.zeros(acc_ref.shape, F32)

    def scores(c, causal):
        off = pl.multiple_of(c * tk, tk)
        s = _dot(ks_ref[pl.ds(off, tk), :], qa_ref[...])
        if causal:
            kpos = off + lax.broadcasted_iota(jnp.int32, (tk, cols), 0)
            t = i * tq + lax.broadcasted_iota(jnp.int32, (tk, cols), 1) % tq
            s = jnp.where(kpos <= t, s, NEG)
        s_ref[c % 2] = s
        mx_ref[c % 2] = jnp.max(s, axis=0, keepdims=True)

    def accumulate(c):
        m_old = m_ref[...]
        m_new = jnp.maximum(m_old, mx_ref[c % 2])
        p = jnp.exp(s_ref[c % 2] - m_new).astype(BF16)
        acc_ref[...] = jnp.exp(m_old - m_new) * acc_ref[...] + _dot(vst_ref[c], p)
        m_ref[...] = m_new

    n_full = (i * tq) // tk

    @pl.when(n_full > 0)
    def _():
        scores(0, False)

    def pair(c, carry):
        accumulate(c)
        scores(c + 1, False)
        return carry

    lax.fori_loop(0, n_full - 1, pair, 0)

    @pl.when(n_full > 0)
    def _():
        scores(n_full, True)
        accumulate(n_full - 1)

    @pl.when(n_full == 0)
    def _():
        scores(n_full, True)

    accumulate(n_full)

    acc = acc_ref[...]
    o_t = acc[0:HEAD_DIM, :] / acc[HEAD_DIM:HEAD_DIM + 1, :]
    o_slc = jnp.concatenate([o_t[:, h * tq:(h + 1) * tq] for h in range(HEADS_PER_GROUP)], axis=0).T
    gates = gt_ref[...]
    part = op_ref[...]
    for h in range(HEADS_PER_GROUP):
        hd = slice(h * HEAD_DIM, (h + 1) * HEAD_DIM)
        o_ref[:, hd] = (part[:, hd] + gates[:, 3 * h + 1:3 * h + 2] * o_slc[:, hd]).astype(BF16)


def _select(qt, bias_t, ksel, vsel_t, gates, o_part):
    batch, _, _, seq = qt.shape
    tq = TQ_SEL
    cols = HEADS_PER_GROUP * tq
    n_chunks = seq // TK_SEL
    tile_g = lambda w: pl.BlockSpec((None, tq, w), lambda b, g, i: (b, i, g))
    return pl.pallas_call(
        _select_body,
        grid=(batch, N_KV_HEADS, seq // tq),
        in_specs=[
            pl.BlockSpec((None, HEADS_PER_GROUP, HEAD_DIM, tq), lambda b, g, i: (b, g, 0, i)),
            pl.BlockSpec((None, None, MAX_SEL_BLOCKS, tq), lambda b, g, i: (b, g, 0, i)),
            pl.BlockSpec((None, None, seq, 2 * LANES), lambda b, g, i: (b, g, 0, 0)),
            pl.BlockSpec((None, None, n_chunks, LANES, TK_SEL), lambda b, g, i: (b, g, 0, 0, 0)),
            tile_g(LANES), tile_g(HEADS_PER_GROUP * HEAD_DIM),
        ],
        out_specs=tile_g(HEADS_PER_GROUP * HEAD_DIM),
        out_shape=jax.ShapeDtypeStruct((batch, seq, NSA_WIDTH), BF16),
        scratch_shapes=[pltpu.VMEM((2 * LANES, cols), BF16), pltpu.VMEM((2, TK_SEL, cols), F32),
                        pltpu.VMEM((2, 1, cols), F32), pltpu.VMEM((1, cols), F32), pltpu.VMEM((LANES, cols), F32)],
        compiler_params=_params(3),
        name="select",
    )(qt, bias_t, ksel, vsel_t, gates, o_part)


def _s5_body(u_ref, bt_ref, pw_ref, a1_ref, ltri_ref, ct_ref, d_ref, wg_ref, bg_ref, o_ref, c_ref, h_ref):
    @pl.when(pl.program_id(1) == 0)
    def _():
        c_ref[...] = jnp.zeros(c_ref.shape, F32)

    n = S5_COLS
    u = u_ref[...]
    bu = _dot(u.astype(BF16), bt_ref[...])
    p_re, p_im, q_re, q_im = pw_ref[0], pw_ref[1], pw_ref[2], pw_ref[3]
    a_re, a_im = a1_ref[0:1, :], a1_ref[1:2, :]
    ltri = ltri_ref[...]

    def cumsum(x):
        hi = x.astype(BF16)
        lo = (x - hi.astype(F32)).astype(BF16)
        return _dot(ltri, hi) + _dot(ltri, lo)

    c_re, c_im = c_ref[0:1, :], c_ref[1:2, :]
    for j in range(u.shape[0] // S5_CHUNK):
        r = slice(j * S5_CHUNK, (j + 1) * S5_CHUNK)
        b_re, b_im = bu[r, 0:n], bu[r, n:2 * n]
        s_re = cumsum(q_re * b_re - q_im * b_im) + c_re
        s_im = cumsum(q_re * b_im + q_im * b_re) + c_im
        h_re = p_re * s_re - p_im * s_im
        h_im = p_re * s_im + p_im * s_re
        l_re, l_im = h_re[S5_CHUNK - 1:S5_CHUNK], h_im[S5_CHUNK - 1:S5_CHUNK]
        c_re = a_re * l_re - a_im * l_im
        c_im = a_re * l_im + a_im * l_re
        h_ref[r, 0:n] = h_re.astype(BF16)
        h_ref[r, n:2 * n] = h_im.astype(BF16)
    c_ref[0:1, :] = c_re
    c_ref[1:2, :] = c_im

    y = _gelu_tanh(_dot(h_ref[...], ct_ref[...]) + d_ref[...] * u)
    o_ref[...] = (y * jax.nn.sigmoid(_dot(y.astype(BF16), wg_ref[...]) + bg_ref[...])).astype(BF16)


def _s5(u, bt, powers, a1, ltri, ct, d, w_glu, b_glu):
    batch, seq, _ = u.shape
    ts = min(S5_TILE, seq)
    tile = pl.BlockSpec((None, ts, S5_WIDTH), lambda b, j: (b, j, 0))
    return pl.pallas_call(
        _s5_body,
        grid=(batch, seq // ts),
        in_specs=[tile] + [_resident()] * 8,
        out_specs=tile,
        out_shape=jax.ShapeDtypeStruct((batch, seq, S5_WIDTH), BF16),
        scratch_shapes=[pltpu.VMEM((2, S5_COLS), F32), pltpu.VMEM((ts, 2 * S5_COLS), BF16)],
        compiler_params=_params(2),
        name="s5",
    )(u, bt, powers, a1, ltri, ct, d, w_glu, b_glu)


def _ple_body(*refs, final_norm):
    if final_norm:
        h_ref, gp_ref, wg_ref, p_ref, wp_ref, gf_ref, o_ref = refs
    else:
        h_ref, gp_ref, wg_ref, p_ref, wp_ref, o_ref = refs
    h = h_ref[...]
    gate = jax.nn.sigmoid(_dot(_rmsnorm(h, gp_ref[...]).astype(BF16), wg_ref[...]))
    h = h + gate * _dot(p_ref[...].astype(BF16), wp_ref[...])
    o_ref[...] = _rmsnorm(h, gf_ref[...]) if final_norm else h


def _ple(h, g_ple, w_gate, p, w_ple, g_final=None):
    t = h.shape[0]
    tm = min(TM_DENSE, t)
    row = lambda w: pl.BlockSpec((tm, w), lambda i: (i, 0))
    args = [h, g_ple, w_gate, p, w_ple]
    specs = [row(D_MODEL), _resident(), _resident(), row(PLE_DIM), _resident()]
    if g_final is not None:
        args.append(g_final)
        specs.append(_resident())
    return pl.pallas_call(
        functools.partial(_ple_body, final_norm=g_final is not None),
        grid=(t // tm,),
        in_specs=specs,
        out_specs=row(D_MODEL),
        out_shape=jax.ShapeDtypeStruct((t, D_MODEL), F32),
        compiler_params=_params(1),
        name="ple",
    )(*args)


def _rearrange_w_in(w_in):
    g0 = NSA_WIDTH + 6 * KV_WIDTH
    per_group = 3 * HEADS_PER_GROUP
    pad = jnp.zeros((D_MODEL, LANES - per_group), w_in.dtype)
    gate_cols = []
    for g in range(N_KV_HEADS):
        gate_cols += [w_in[:, g0 + g * per_group:g0 + (g + 1) * per_group], pad]
    u0 = g0 + 3 * N_HEADS
    return jnp.concatenate([w_in[:, :g0], w_in[:, u0:]] + gate_cols, axis=1).astype(BF16)


def _rope_tables(seq):
    half = HEAD_DIM // 2
    inv = ROPE_THETA ** (-jnp.arange(half, dtype=F32) / half)
    ang = jnp.arange(seq, dtype=F32)[:, None] * inv[None, :]
    cos = jnp.tile(jnp.cos(ang), (1, LANES // half))
    sin = jnp.tile(jnp.concatenate([-jnp.sin(ang), jnp.sin(ang)], axis=1), (1, LANES // HEAD_DIM))
    return cos, sin


def _compress_weights(pe, w1, w2):
    w1 = w1.reshape(2, CMP_STRIDE, HEAD_DIM, CMP_HIDDEN)
    eye = jnp.eye(N_KV_HEADS, dtype=w1.dtype)
    big = jnp.einsum("sjdh,ge->sjgdeh", w1, eye).reshape(2, CMP_STRIDE * KV_WIDTH, N_KV_HEADS * CMP_HIDDEN)
    pe2 = jnp.broadcast_to(pe.reshape(2, CMP_STRIDE, 1, HEAD_DIM), (2, CMP_STRIDE, N_KV_HEADS, HEAD_DIM))
    return pe2.reshape(2, CMP_STRIDE * KV_WIDTH), big[0].astype(BF16), big[1].astype(BF16), w2.astype(BF16)


def _overlap_matrix(n_cmp_rows):
    c_start = np.arange(n_cmp_rows)[:, None] * CMP_STRIDE
    s_start = np.arange(MAX_SEL_BLOCKS)[None, :] * SEL_BLOCK
    ov = (c_start < s_start + SEL_BLOCK) & (c_start + CMP_LEN > s_start)
    return jnp.asarray(ov, dtype=BF16)


def _s5_tables(a_re, a_im, log_dt, b_re, b_im, c_re, c_im):
    dt = jnp.exp(log_dt)[:, None]
    lr, li = a_re * dt, a_im * dt
    mag = jnp.exp(lr)
    ab_re, ab_im = mag * jnp.cos(li), mag * jnp.sin(li)
    den = a_re * a_re + a_im * a_im
    nr, ni = ab_re - 1.0, ab_im
    bc_re = (nr * a_re + ni * a_im) / den
    bc_im = (ni * a_re - nr * a_im) / den
    bt_re = b_re * bc_re[..., None] - b_im * bc_im[..., None]
    bt_im = b_re * bc_im[..., None] + b_im * bc_re[..., None]
    eye = jnp.eye(S5_GROUPS, dtype=F32)
    blockdiag_in = lambda w: jnp.einsum("gni,ge->gien", w, eye).reshape(S5_WIDTH, S5_COLS)
    bt = jnp.concatenate([blockdiag_in(bt_re), blockdiag_in(bt_im)], axis=1).astype(BF16)
    blockdiag_out = lambda w: jnp.einsum("gon,ge->gneo", w, eye).reshape(S5_COLS, S5_WIDTH)
    ct = jnp.concatenate([blockdiag_out(c_re), -blockdiag_out(c_im)], axis=0).astype(BF16)
    k = jnp.arange(S5_CHUNK, dtype=F32)[:, None]
    lr, li = lr.reshape(1, S5_COLS), li.reshape(1, S5_COLS)
    powers = jnp.stack([jnp.exp(k * lr) * jnp.cos(k * li), jnp.exp(k * lr) * jnp.sin(k * li),
                        jnp.exp(-k * lr) * jnp.cos(k * li), -jnp.exp(-k * lr) * jnp.sin(k * li)])
    a1 = jnp.concatenate([ab_re.reshape(1, S5_COLS), ab_im.reshape(1, S5_COLS)], axis=0)
    ltri = jnp.asarray(np.tril(np.ones((S5_CHUNK, S5_CHUNK))), dtype=BF16)
    return bt, powers, a1, ltri, ct


def kernel(x, p, norm_ffn1, ffn1_w1, ffn1_w3, ffn1_w2, norm_mix, w_in, cmp_pe_k, cmp_pe_v, cmp_wk1, cmp_wk2,
           cmp_wv1, cmp_wv2, s5_a_re, s5_a_im, s5_log_dt, s5_b_re, s5_b_im, s5_c_re, s5_c_im, s5_d, s5_w_glu,
           s5_b_glu, w_out, norm_ffn2, ffn2_w1, ffn2_w3, ffn2_w2, norm_ple, w_ple_gate, w_ple, norm_final):
    batch, seq, _ = x.shape
    depth = p.shape[0]
    t = batch * seq
    assert seq % TK_SEL == 0 and TK_SEL % TQ_SEL == 0 and seq % TQ_SEL == 0 and seq % S5_TILE == 0 and seq >= WINDOW + TQ
    assert seq // SEL_BLOCK <= MAX_SEL_BLOCKS and seq // SEL_BLOCK >= SEL_TOPK
    bf = lambda w: w.astype(BF16)
    row = lambda v: v.reshape(1, -1)

    cos, sin_signed = _rope_tables(seq)
    overlap = _overlap_matrix(seq // CMP_STRIDE)
    h = x.reshape(t, D_MODEL)
    for i in range(depth):
        h = _ffn(h, row(norm_ffn1[i]), bf(ffn1_w1[i]), bf(ffn1_w3[i]), bf(ffn1_w2[i]))

        q, qt, kc, vc, ksel, vsel_t, kwin, vwin, gates, u = _proj(
            h, row(norm_mix[i]), _rearrange_w_in(w_in[i]), cos, sin_signed, batch, seq)
        pe_k2, wkt, wkb, wk2 = _compress_weights(cmp_pe_k[i], cmp_wk1[i], cmp_wk2[i])
        pe_v2, wvt, wvb, wv2 = _compress_weights(cmp_pe_v[i], cmp_wv1[i], cmp_wv2[i])
        kcmp, vcmp = _compress(kc, vc, pe_k2, pe_v2, wkt, wkb, wvt, wvb, wk2, wv2)
        o_part, bias_t = _cmpwin(q, kcmp, vcmp, overlap, kwin, vwin, gates)
        o_nsa = _select(qt, bias_t, ksel, vsel_t, gates, o_part)

        bt, powers, a1, ltri, ct = _s5_tables(s5_a_re[i], s5_a_im[i], s5_log_dt[i], s5_b_re[i], s5_b_im[i],
                                              s5_c_re[i], s5_c_im[i])
        o_s5 = _s5(u, bt, powers, a1, ltri, ct, row(s5_d[i]), bf(s5_w_glu[i]), row(s5_b_glu[i]))

        h = _ffn(h, row(norm_ffn2[i]), bf(ffn2_w1[i]), bf(ffn2_w3[i]), bf(ffn2_w2[i]),
                 mix=(o_nsa.reshape(t, NSA_WIDTH), o_s5.reshape(t, S5_WIDTH),
                      bf(w_out[i][:NSA_WIDTH]), bf(w_out[i][NSA_WIDTH:])))
        h = _ple(h, row(norm_ple[i]), bf(w_ple_gate[i]), p[i].reshape(t, PLE_DIM), bf(w_ple[i]),
                 row(norm_final) if i == depth - 1 else None)
    return h.reshape(batch, seq, D_MODEL)
```

```python
import functools

import jax
import jax.numpy as jnp
import numpy as np
from jax import lax
from jax.experimental import pallas as pl
from jax.experimental.pallas import tpu as pltpu

D_MODEL = 1024
PLE_DIM = 256
D_FF = 2816
N_HEADS = 8
N_KV_HEADS = 2
HEAD_DIM = 64
HEADS_PER_GROUP = N_HEADS // N_KV_HEADS
NSA_WIDTH = N_HEADS * HEAD_DIM
KV_WIDTH = N_KV_HEADS * HEAD_DIM
S5_WIDTH = D_MODEL - NSA_WIDTH
S5_GROUP = 16
S5_GROUPS = S5_WIDTH // S5_GROUP
S5_STATE = 64
S5_COLS = S5_GROUPS * S5_STATE
CMP_LEN = 32
CMP_STRIDE = 16
CMP_HIDDEN = 256
SEL_BLOCK = 64
SEL_TOPK = 16
WINDOW = 512
ROPE_THETA = 10000.0
RMS_EPS = 1e-6
NEG = -1e30
BIG = 1e9

LANES = 128
V_ROWS = HEAD_DIM + 16
GATE_ROWS = 16
MAX_SEL_BLOCKS = LANES
VMEM_LIMIT = 52 * 1024 * 1024

TM_DENSE = 512
FF_CHUNK = 256
TQ = 128
TQ_SEL = 512
TK_SEL = 512
S5_TILE = 256
S5_CHUNK = 64
S5_SPLIT = 2

F32 = jnp.float32
BF16 = jnp.bfloat16


def _dot(a, b):
    return jnp.dot(a, b, preferred_element_type=F32)


def _rmsnorm(x, g):
    ms = jnp.mean(x * x, axis=-1, keepdims=True)
    return x * lax.rsqrt(ms + RMS_EPS) * g


def _gelu_tanh(x):
    return 0.5 * x * (1.0 + jnp.tanh(np.sqrt(2.0 / np.pi).astype(np.float32) * (x + 0.044715 * (x * x * x))))


def _params(n_grid):
    return pltpu.CompilerParams(dimension_semantics=("arbitrary",) * n_grid, vmem_limit_bytes=VMEM_LIMIT)


def _resident():
    return pl.BlockSpec(memory_space=pltpu.VMEM)


def _ffn_body(*refs, with_mix):
    if with_mix:
        h_ref, on_ref, os_ref, won_ref, wos_ref, g_ref, w1_ref, w3_ref, w2_ref, o_ref = refs
        h = h_ref[...] + _dot(on_ref[...], won_ref[...]) + _dot(os_ref[...], wos_ref[...])
    else:
        h_ref, g_ref, w1_ref, w3_ref, w2_ref, o_ref = refs
        h = h_ref[...]
    xn = _rmsnorm(h, g_ref[...]).astype(BF16)
    acc = jnp.zeros(h.shape, F32)
    for j in range(D_FF // FF_CHUNK):
        sl = slice(j * FF_CHUNK, (j + 1) * FF_CHUNK)
        a = _dot(xn, w1_ref[:, sl])
        b = _dot(xn, w3_ref[:, sl])
        act = (a * jax.nn.sigmoid(a) * b).astype(BF16)
        acc = acc + _dot(act, w2_ref[sl, :])
    o_ref[...] = h + 0.5 * acc


def _ffn(h, g, w1, w3, w2, mix=None):
    t = h.shape[0]
    tm = min(TM_DENSE, t)
    row = lambda w: pl.BlockSpec((tm, w), lambda i: (i, 0))
    args, specs = [h], [row(D_MODEL)]
    if mix is not None:
        o_nsa, o_s5, wo_n, wo_s = mix
        args += [o_nsa, o_s5, wo_n, wo_s]
        specs += [row(NSA_WIDTH), row(S5_WIDTH), _resident(), _resident()]
    args += [g, w1, w3, w2]
    specs += [_resident()] * 4
    return pl.pallas_call(
        functools.partial(_ffn_body, with_mix=mix is not None),
        grid=(t // tm,),
        in_specs=specs,
        out_specs=row(D_MODEL),
        out_shape=jax.ShapeDtypeStruct((t, D_MODEL), F32),
        compiler_params=_params(1),
        name="ffn_mix" if mix is not None else "ffn",
    )(*args)


_C_Q = 0
_C_CMP = NSA_WIDTH
_C_SLC = _C_CMP + 2 * KV_WIDTH
_C_WIN = _C_SLC + 2 * KV_WIDTH
_C_U = _C_WIN + 2 * KV_WIDTH
_C_G = _C_U + S5_WIDTH
_C_END = _C_G + N_KV_HEADS * LANES


def _rope_pair(x, cos, sin_signed, first_half):
    fwd = pltpu.roll(x, HEAD_DIM // 2, 1)
    bwd = pltpu.roll(x, LANES - HEAD_DIM // 2, 1)
    return x * cos + jnp.where(first_half, bwd, fwd) * sin_signed


def _proj_body(h_ref, g_ref, w_ref, cos_ref, sin_ref,
               qt_ref, kc_ref, vc_ref, ks_ref, vst_ref, kw_ref, vwt_ref, gtt_ref, u_ref, *, tiles_per_seq):
    tm = h_ref.shape[0]
    xn = _rmsnorm(h_ref[...], g_ref[...]).astype(BF16)
    cos = cos_ref[...]
    sin = sin_ref[...]
    lane = lax.broadcasted_iota(jnp.int32, (tm, LANES), 1)
    first_half = (lane % HEAD_DIM) < (HEAD_DIM // 2)
    rope = lambda x: _rope_pair(x, cos, sin, first_half)

    scale = HEAD_DIM ** -0.5
    for j in range(NSA_WIDTH // 256):
        z = _dot(xn, w_ref[:, _C_Q + 256 * j:_C_Q + 256 * (j + 1)])
        for k in range(2):
            rt = (rope(z[:, LANES * k:LANES * (k + 1)]) * scale).T.astype(BF16)
            for e in range(2):
                qt_ref[4 * j + 2 * k + e] = rt[e * HEAD_DIM:(e + 1) * HEAD_DIM, :]

    z = _dot(xn, w_ref[:, _C_CMP:_C_CMP + 256])
    kc_ref[...] = rope(z[:, :LANES])
    vc_ref[...] = z[:, LANES:]

    zeros = jnp.zeros((tm, HEAD_DIM), BF16)
    pos = (pl.program_id(0) % tiles_per_seq) * tm + lax.broadcasted_iota(jnp.int32, (tm, LANES), 0)
    onehot = jnp.where(lane == pos // SEL_BLOCK, 1.0, 0.0).astype(BF16)

    z = _dot(xn, w_ref[:, _C_SLC:_C_SLC + 256])
    k = rope(z[:, :LANES]).astype(BF16)
    vt = z[:, LANES:].T.astype(BF16)
    for g in range(N_KV_HEADS):
        gs = slice(g * HEAD_DIM, (g + 1) * HEAD_DIM)
        ks_ref[g, :, 0:LANES] = onehot
        ks_ref[g, :, LANES:LANES + HEAD_DIM] = k[:, gs]
        ks_ref[g, :, LANES + HEAD_DIM:] = zeros
        vst_ref[g, 0:HEAD_DIM, :] = vt[gs, :]
        vst_ref[g, HEAD_DIM:, :] = jnp.ones((V_ROWS - HEAD_DIM, tm), BF16)

    z = _dot(xn, w_ref[:, _C_WIN:_C_WIN + 256])
    k = rope(z[:, :LANES]).astype(BF16)
    vt = z[:, LANES:].T.astype(BF16)
    for g in range(N_KV_HEADS):
        gs = slice(g * HEAD_DIM, (g + 1) * HEAD_DIM)
        kw_ref[g] = k[:, gs]
        for c in range(tm // TQ):
            vwt_ref[g, c, 0:HEAD_DIM, :] = vt[gs, c * TQ:(c + 1) * TQ]
            vwt_ref[g, c, HEAD_DIM:, :] = jnp.ones((V_ROWS - HEAD_DIM, TQ), BF16)

    for j in range(S5_WIDTH // 256):
        u_ref[:, 256 * j:256 * (j + 1)] = _dot(xn, w_ref[:, _C_U + 256 * j:_C_U + 256 * (j + 1)])
    gates_t = jax.nn.sigmoid(_dot(xn, w_ref[:, _C_G:_C_END])).T
    for g in range(N_KV_HEADS):
        gtt_ref[g] = gates_t[g * LANES:g * LANES + GATE_ROWS, :]


def _proj(h1, g, w_in_r, cos, sin_signed, batch, seq):
    t = batch * seq
    tm = TK_SEL
    n = seq // tm
    bl = lambda w: pl.BlockSpec((None, tm, w), lambda i: (i // n, i % n, 0))
    bgl = lambda w: pl.BlockSpec((None, N_KV_HEADS, tm, w), lambda i: (i // n, 0, i % n, 0))
    tab = pl.BlockSpec((tm, LANES), lambda i: (i % n, 0))
    sds = jax.ShapeDtypeStruct
    return pl.pallas_call(
        functools.partial(_proj_body, tiles_per_seq=n),
        grid=(t // tm,),
        in_specs=[pl.BlockSpec((tm, D_MODEL), lambda i: (i, 0)), _resident(), _resident(), tab, tab],
        out_specs=[
            pl.BlockSpec((None, N_HEADS, HEAD_DIM, tm), lambda i: (i // n, 0, 0, i % n)),
            bl(KV_WIDTH), bl(KV_WIDTH),
            bgl(2 * LANES),
            pl.BlockSpec((None, N_KV_HEADS, None, V_ROWS, tm), lambda i: (i // n, 0, i % n, 0, 0)),
            bgl(HEAD_DIM),
            pl.BlockSpec((None, N_KV_HEADS, tm // TQ, V_ROWS, TQ), lambda i: (i // n, 0, i % n, 0, 0)),
            pl.BlockSpec((None, N_KV_HEADS, GATE_ROWS, tm), lambda i: (i // n, 0, 0, i % n)),
            bl(S5_WIDTH),
        ],
        out_shape=[
            sds((batch, N_HEADS, HEAD_DIM, seq), BF16),
            sds((batch, seq, KV_WIDTH), F32), sds((batch, seq, KV_WIDTH), F32),
            sds((batch, N_KV_HEADS, seq, 2 * LANES), BF16), sds((batch, N_KV_HEADS, n, V_ROWS, tm), BF16),
            sds((batch, N_KV_HEADS, seq, HEAD_DIM), BF16), sds((batch, N_KV_HEADS, seq // TQ, V_ROWS, TQ), BF16),
            sds((batch, N_KV_HEADS, GATE_ROWS, seq), F32), sds((batch, seq, S5_WIDTH), F32),
        ],
        compiler_params=_params(1),
        name="proj",
    )(h1.reshape(t, D_MODEL), g, w_in_r, cos, sin_signed)


def _compress_body(kc_ref, vc_ref, pek_ref, pev_ref, wkt_ref, wkb_ref, wvt_ref, wvb_ref, wk2_ref, wv2_ref,
                   ko_ref, vot_ref, v_ref):
    nc = kc_ref.shape[0]

    def hidden(x, pe_ref, wt_ref, wb_ref):
        top = _dot((x + pe_ref[0:1, :]).astype(BF16), wt_ref[...])
        bot = _dot((x + pe_ref[1:2, :]).astype(BF16), wb_ref[...])
        pre = top + pltpu.roll(bot, nc - 1, 0)
        return _gelu_tanh(pre).astype(BF16)

    hk = hidden(kc_ref[...], pek_ref, wkt_ref, wkb_ref)
    hv = hidden(vc_ref[...], pev_ref, wvt_ref, wvb_ref)
    for g in range(N_KV_HEADS):
        gs = slice(g * CMP_HIDDEN, (g + 1) * CMP_HIDDEN)
        ko_ref[g] = _dot(hk[:, gs], wk2_ref[...]).astype(BF16)
        v_ref[...] = _dot(hv[:, gs], wv2_ref[...])
        vot_ref[g] = v_ref[...].T.astype(BF16)


def _compress(kc, vc, pe_k2, pe_v2, wkt, wkb, wvt, wvb, wk2, wv2):
    batch, seq, _ = kc.shape
    nc = seq // CMP_STRIDE
    width = CMP_STRIDE * KV_WIDTH
    x_spec = pl.BlockSpec((None, nc, width), lambda b: (b, 0, 0))
    sds = jax.ShapeDtypeStruct
    return pl.pallas_call(
        _compress_body,
        grid=(batch,),
        in_specs=[x_spec, x_spec] + [_resident()] * 8,
        out_specs=[pl.BlockSpec((None, N_KV_HEADS, nc, HEAD_DIM), lambda b: (b, 0, 0, 0)),
                   pl.BlockSpec((None, N_KV_HEADS, HEAD_DIM, nc), lambda b: (b, 0, 0, 0))],
        out_shape=[sds((batch, N_KV_HEADS, nc, HEAD_DIM), BF16), sds((batch, N_KV_HEADS, HEAD_DIM, nc), BF16)],
        scratch_shapes=[pltpu.VMEM((nc, HEAD_DIM), F32)],
        compiler_params=_params(1),
        name="compress",
    )(kc.reshape(batch, nc, width), vc.reshape(batch, nc, width), pe_k2, pe_v2, wkt, wkb, wvt, wvb, wk2, wv2)


def _cmpwin_body(qt_ref, kc_ref, vct_ref, ovt_ref, kw_ref, vwt_ref, gtt_ref, opt_ref, bias_ref, s_ref):
    tq = qt_ref.shape[2]
    cols = HEADS_PER_GROUP * tq
    nc = kc_ref.shape[0]
    i = pl.program_id(2)
    q_t = jnp.concatenate([qt_ref[h] for h in range(HEADS_PER_GROUP)], axis=1)
    t_lane = i * tq + lax.broadcasted_iota(jnp.int32, (1, cols), 1) % tq

    s = _dot(kc_ref[...], q_t)
    c_end = lax.broadcasted_iota(jnp.int32, (nc, cols), 0) * CMP_STRIDE + (CMP_LEN - 1)
    s = jnp.where(c_end <= t_lane, s, NEG)
    e = jnp.exp(s - jnp.max(s, axis=0, keepdims=True))
    den = jnp.maximum(jnp.sum(e, axis=0, keepdims=True), 1e-30)
    p = e * jnp.where(t_lane >= CMP_LEN - 1, 1.0 / den, 0.0)
    o_cmp = _dot(vct_ref[...], p.astype(BF16))

    psum = p[:, 0:tq]
    for h in range(1, HEADS_PER_GROUP):
        psum = psum + p[:, h * tq:(h + 1) * tq]
    p_hi = psum.astype(BF16)
    p_lo = (psum - p_hi.astype(F32)).astype(BF16)
    imp = _dot(ovt_ref[...], p_hi) + _dot(ovt_ref[...], p_lo)

    blk = lax.broadcasted_iota(jnp.int32, (MAX_SEL_BLOCKS, tq), 0)
    behind = t_lane[:, 0:tq] // SEL_BLOCK - blk
    score = jnp.where(blk == 0, BIG, jnp.where(behind == 0, BIG, jnp.where(behind == 1, BIG, imp)))
    score = jnp.where(behind >= 0, score, -BIG)
    chosen = jnp.zeros((MAX_SEL_BLOCKS, tq), F32)
    for _ in range(SEL_TOPK):
        best = jnp.max(score, axis=0, keepdims=True)
        first = jnp.min(jnp.where(score == best, blk, MAX_SEL_BLOCKS), axis=0, keepdims=True)
        hit = blk == first
        chosen = jnp.where(hit, 1.0, chosen)
        score = jnp.where(hit, -jnp.inf, score)
    bias_ref[...] = jnp.where(chosen > 0.0, jnp.where(behind >= 0, 0.0, NEG), NEG).astype(BF16)

    span = WINDOW + tq
    n_sub = span // tq
    start = pl.multiple_of(jnp.maximum(i * tq - WINDOW, 0), tq)
    s_ref[...] = _dot(kw_ref[pl.ds(start, span), :], q_t)

    def band(rows):
        kpos = start + rows.start + lax.broadcasted_iota(jnp.int32, (rows.stop - rows.start, cols), 0)
        dist = lax.bitcast_convert_type(t_lane - kpos, jnp.uint32)
        s_ref[rows, :] = jnp.where(dist < WINDOW, s_ref[rows, :], NEG)

    @pl.when(i * tq >= WINDOW)
    def _():
        band(slice(0, tq))
        band(slice(span - tq, span))

    @pl.when(i * tq < WINDOW)
    def _():
        band(slice(0, span))

    s = s_ref[...]
    e = jnp.exp(s - jnp.max(s, axis=0, keepdims=True)).astype(BF16)
    c0 = start // tq
    vw_t = jnp.concatenate([vwt_ref[c0 + j] for j in range(n_sub)], axis=1)
    acc = _dot(vw_t, e)
    o_win = acc[0:HEAD_DIM, :] / acc[HEAD_DIM:HEAD_DIM + 1, :]

    gates = gtt_ref[...]
    for h in range(HEADS_PER_GROUP):
        hs = slice(h * tq, (h + 1) * tq)
        opt_ref[h * HEAD_DIM:(h + 1) * HEAD_DIM, :] = (gates[3 * h:3 * h + 1, :] * o_cmp[:, hs]
                                                       + gates[3 * h + 2:3 * h + 3, :] * o_win[:, hs])


def _cmpwin(qt, kcmp, vcmp_t, overlap_t, kwin, vwin_t, gates_t):
    batch, _, _, seq = qt.shape
    nc = kcmp.shape[2]
    tq = TQ
    per_bg = lambda *shape: pl.BlockSpec((None, None) + shape, lambda b, g, i: (b, g) + (0,) * len(shape))
    sds = jax.ShapeDtypeStruct
    return pl.pallas_call(
        _cmpwin_body,
        grid=(batch, N_KV_HEADS, seq // tq),
        in_specs=[
            pl.BlockSpec((None, HEADS_PER_GROUP, HEAD_DIM, tq), lambda b, g, i: (b, g, 0, i)),
            per_bg(nc, HEAD_DIM), per_bg(HEAD_DIM, nc), _resident(),
            per_bg(seq, HEAD_DIM), per_bg(seq // tq, V_ROWS, tq),
            pl.BlockSpec((None, None, GATE_ROWS, tq), lambda b, g, i: (b, g, 0, i)),
        ],
        out_specs=[pl.BlockSpec((None, None, HEADS_PER_GROUP * HEAD_DIM, tq), lambda b, g, i: (b, g, 0, i)),
                   pl.BlockSpec((None, None, MAX_SEL_BLOCKS, tq), lambda b, g, i: (b, g, 0, i))],
        out_shape=[sds((batch, N_KV_HEADS, HEADS_PER_GROUP * HEAD_DIM, seq), F32),
                   sds((batch, N_KV_HEADS, MAX_SEL_BLOCKS, seq), BF16)],
        scratch_shapes=[pltpu.VMEM((WINDOW + tq, HEADS_PER_GROUP * tq), F32)],
        compiler_params=_params(3),
        name="cmpwin",
    )(qt, kcmp, vcmp_t, overlap_t, kwin, vwin_t, gates_t)


def _select_body(qt_ref, bias_ref, ks_ref, vst_ref, gtt_ref, opt_ref, o_ref, qa_ref, s_ref, mx_ref, m_ref, acc_ref):
    tq = qt_ref.shape[2]
    cols = HEADS_PER_GROUP * tq
    tk = TK_SEL
    i = pl.program_id(2)

    bias = bias_ref[...]
    for h in range(HEADS_PER_GROUP):
        qa_ref[0:LANES, h * tq:(h + 1) * tq] = bias
        qa_ref[LANES:LANES + HEAD_DIM, h * tq:(h + 1) * tq] = qt_ref[h]
    qa_ref[LANES + HEAD_DIM:, :] = jnp.zeros((LANES - HEAD_DIM, cols), BF16)
    m_ref[...] = jnp.full(m_ref.shape, NEG, F32)
    acc_ref[...] = jnp.zeros(acc_ref.shape, F32)

    def scores(c, causal):
        off = pl.multiple_of(c * tk, tk)
        s = _dot(ks_ref[pl.ds(off, tk), :], qa_ref[...])
        if causal:
            kpos = off + lax.broadcasted_iota(jnp.int32, (tk, cols), 0)
            t = i * tq + lax.broadcasted_iota(jnp.int32, (tk, cols), 1) % tq
            s = jnp.where(kpos <= t, s, NEG)
        s_ref[c % 2] = s
        mx_ref[c % 2] = jnp.max(s, axis=0, keepdims=True)

    def accumulate(c):
        m_old = m_ref[...]
        m_new = jnp.maximum(m_old, mx_ref[c % 2])
        p = jnp.exp(s_ref[c % 2] - m_new).astype(BF16)
        acc_ref[...] = jnp.exp(m_old - m_new) * acc_ref[...] + _dot(vst_ref[c], p)
        m_ref[...] = m_new

    n_full = (i * tq) // tk

    @pl.when(n_full > 0)
    def _():
        scores(0, False)

    def pair(c, carry):
        accumulate(c)
        scores(c + 1, False)
        return carry

    lax.fori_loop(0, n_full - 1, pair, 0)

    @pl.when(n_full > 0)
    def _():
        accumulate(n_full - 1)
        scores(n_full, True)

    @pl.when(n_full == 0)
    def _():
        scores(n_full, True)

    accumulate(n_full)

    acc = acc_ref[...]
    o_slc = acc[0:HEAD_DIM, :] / acc[HEAD_DIM:HEAD_DIM + 1, :]
    gates = gtt_ref[...]
    o_t = jnp.concatenate([gates[3 * h + 1:3 * h + 2, :] * o_slc[:, h * tq:(h + 1) * tq]
                           for h in range(HEADS_PER_GROUP)], axis=0)
    o_ref[...] = (opt_ref[...] + o_t).T.astype(BF16)


def _select(qt, bias_t, ksel, vsel_t, gates_t, o_part_t):
    batch, _, _, seq = qt.shape
    tq = TQ_SEL
    cols = HEADS_PER_GROUP * tq
    n_chunks = seq // TK_SEL
    tile_t = lambda r: pl.BlockSpec((None, None, r, tq), lambda b, g, i: (b, g, 0, i))
    return pl.pallas_call(
        _select_body,
        grid=(batch, N_KV_HEADS, seq // tq),
        in_specs=[
            pl.BlockSpec((None, HEADS_PER_GROUP, HEAD_DIM, tq), lambda b, g, i: (b, g, 0, i)),
            tile_t(MAX_SEL_BLOCKS),
            pl.BlockSpec((None, None, seq, 2 * LANES), lambda b, g, i: (b, g, 0, 0)),
            pl.BlockSpec((None, None, n_chunks, V_ROWS, TK_SEL), lambda b, g, i: (b, g, 0, 0, 0)),
            tile_t(GATE_ROWS), tile_t(HEADS_PER_GROUP * HEAD_DIM),
        ],
        out_specs=pl.BlockSpec((None, tq, HEADS_PER_GROUP * HEAD_DIM), lambda b, g, i: (b, i, g)),
        out_shape=jax.ShapeDtypeStruct((batch, seq, NSA_WIDTH), BF16),
        scratch_shapes=[pltpu.VMEM((2 * LANES, cols), BF16), pltpu.VMEM((2, TK_SEL, cols), F32),
                        pltpu.VMEM((2, 1, cols), F32), pltpu.VMEM((1, cols), F32), pltpu.VMEM((V_ROWS, cols), F32)],
        compiler_params=_params(3),
        name="select",
    )(qt, bias_t, ksel, vsel_t, gates_t, o_part_t)


def _s5_body(u_ref, bt_ref, pw_ref, a1_ref, ltri_ref, ct_ref, d_ref, wg_ref, bg_ref, o_ref, c_ref, x_ref, h_ref):
    @pl.when(pl.program_id(1) == 0)
    def _():
        c_ref[...] = jnp.zeros(c_ref.shape, F32)

    n = S5_COLS // S5_SPLIT
    u = u_ref[...]
    last = S5_CHUNK - 1
    ys = []
    for hf in range(S5_SPLIT):
        cs = slice(hf * n, (hf + 1) * n)
        p_re, p_im, q_re, q_im = pw_ref[0, :, cs], pw_ref[1, :, cs], pw_ref[2, :, cs], pw_ref[3, :, cs]
        a_re, a_im = a1_ref[0:1, cs], a1_ref[1:2, cs]
        width = S5_WIDTH // S5_SPLIT
        bu = _dot(u[:, hf * width:(hf + 1) * width].astype(BF16), bt_ref[hf])
        for j in range(u.shape[0] // S5_CHUNK):
            r = slice(j * S5_CHUNK, (j + 1) * S5_CHUNK)
            b_re, b_im = bu[r, 0:n], bu[r, n:2 * n]
            x_ref[r, 0:n] = (q_re * b_re - q_im * b_im).astype(BF16)
            x_ref[r, n:2 * n] = (q_re * b_im + q_im * b_re).astype(BF16)
        local = _dot(ltri_ref[...], x_ref[...])
        c_re, c_im = c_ref[0:1, cs], c_ref[1:2, cs]
        for j in range(u.shape[0] // S5_CHUNK):
            r = slice(j * S5_CHUNK, (j + 1) * S5_CHUNK)
            s_re, s_im = local[r, 0:n] + c_re, local[r, n:2 * n] + c_im
            h_ref[r, 0:n] = (p_re * s_re - p_im * s_im).astype(BF16)
            h_ref[r, n:2 * n] = (p_re * s_im + p_im * s_re).astype(BF16)
            e_re, e_im = s_re[last:last + 1], s_im[last:last + 1]
            l_re = p_re[last:last + 1] * e_re - p_im[last:last + 1] * e_im
            l_im = p_re[last:last + 1] * e_im + p_im[last:last + 1] * e_re
            c_re = a_re * l_re - a_im * l_im
            c_im = a_re * l_im + a_im * l_re
        c_ref[0:1, cs] = c_re
        c_ref[1:2, cs] = c_im
        ys.append(_dot(h_ref[...], ct_ref[hf]))
    y = _gelu_tanh(jnp.concatenate(ys, axis=1) + d_ref[...] * u)
    o_ref[...] = (y * jax.nn.sigmoid(_dot(y.astype(BF16), wg_ref[...]) + bg_ref[...])).astype(BF16)


def _s5(u, bt, powers, a1, ltri, ct, d, w_glu, b_glu):
    batch, seq, _ = u.shape
    ts = S5_TILE
    tile = pl.BlockSpec((None, ts, S5_WIDTH), lambda b, j: (b, j, 0))
    cols = 2 * S5_COLS // S5_SPLIT
    return pl.pallas_call(
        _s5_body,
        grid=(batch, seq // ts),
        in_specs=[tile] + [_resident()] * 8,
        out_specs=tile,
        out_shape=jax.ShapeDtypeStruct((batch, seq, S5_WIDTH), BF16),
        scratch_shapes=[pltpu.VMEM((2, S5_COLS), F32), pltpu.VMEM((ts, cols), BF16), pltpu.VMEM((ts, cols), BF16)],
        compiler_params=_params(2),
        name="s5",
    )(u, bt, powers, a1, ltri, ct, d, w_glu, b_glu)


def _ple_body(*refs, final_norm):
    if final_norm:
        h_ref, gp_ref, wg_ref, p_ref, wp_ref, gf_ref, o_ref = refs
    else:
        h_ref, gp_ref, wg_ref, p_ref, wp_ref, o_ref = refs
    h = h_ref[...]
    gate = jax.nn.sigmoid(_dot(_rmsnorm(h, gp_ref[...]).astype(BF16), wg_ref[...]))
    h = h + gate * _dot(p_ref[...].astype(BF16), wp_ref[...])
    o_ref[...] = _rmsnorm(h, gf_ref[...]) if final_norm else h


def _ple(h, g_ple, w_gate, p, w_ple, g_final=None):
    t = h.shape[0]
    tm = min(TM_DENSE, t)
    row = lambda w: pl.BlockSpec((tm, w), lambda i: (i, 0))
    args = [h, g_ple, w_gate, p, w_ple]
    specs = [row(D_MODEL), _resident(), _resident(), row(PLE_DIM), _resident()]
    if g_final is not None:
        args.append(g_final)
        specs.append(_resident())
    return pl.pallas_call(
        functools.partial(_ple_body, final_norm=g_final is not None),
        grid=(t // tm,),
        in_specs=specs,
        out_specs=row(D_MODEL),
        out_shape=jax.ShapeDtypeStruct((t, D_MODEL), F32),
        compiler_params=_params(1),
        name="ple",
    )(*args)


def _rearrange_w_in(w_in):
    g0 = NSA_WIDTH + 6 * KV_WIDTH
    per_group = 3 * HEADS_PER_GROUP
    pad = jnp.zeros((D_MODEL, LANES - per_group), w_in.dtype)
    gate_cols = []
    for g in range(N_KV_HEADS):
        gate_cols += [w_in[:, g0 + g * per_group:g0 + (g + 1) * per_group], pad]
    u0 = g0 + 3 * N_HEADS
    return jnp.concatenate([w_in[:, :g0], w_in[:, u0:]] + gate_cols, axis=1).astype(BF16)


def _rope_tables(seq):
    half = HEAD_DIM // 2
    inv = ROPE_THETA ** (-jnp.arange(half, dtype=F32) / half)
    ang = jnp.arange(seq, dtype=F32)[:, None] * inv[None, :]
    cos = jnp.tile(jnp.cos(ang), (1, LANES // half))
    sin = jnp.tile(jnp.concatenate([-jnp.sin(ang), jnp.sin(ang)], axis=1), (1, LANES // HEAD_DIM))
    return cos, sin


def _compress_weights(pe, w1, w2):
    w1 = w1.reshape(2, CMP_STRIDE, HEAD_DIM, CMP_HIDDEN)
    eye = jnp.eye(N_KV_HEADS, dtype=w1.dtype)
    big = jnp.einsum("sjdh,ge->sjgdeh", w1, eye).reshape(2, CMP_STRIDE * KV_WIDTH, N_KV_HEADS * CMP_HIDDEN)
    pe2 = jnp.broadcast_to(pe.reshape(2, CMP_STRIDE, 1, HEAD_DIM), (2, CMP_STRIDE, N_KV_HEADS, HEAD_DIM))
    return pe2.reshape(2, CMP_STRIDE * KV_WIDTH), big[0].astype(BF16), big[1].astype(BF16), w2.astype(BF16)


def _overlap_matrix(n_cmp_rows):
    c_start = np.arange(n_cmp_rows)[:, None] * CMP_STRIDE
    s_start = np.arange(MAX_SEL_BLOCKS)[None, :] * SEL_BLOCK
    ov = (c_start < s_start + SEL_BLOCK) & (c_start + CMP_LEN > s_start)
    return jnp.asarray(ov.T, dtype=BF16)


def _s5_tables(a_re, a_im, log_dt, b_re, b_im, c_re, c_im):
    dt = jnp.exp(log_dt)[:, None]
    lr, li = a_re * dt, a_im * dt
    mag = jnp.exp(lr)
    ab_re, ab_im = mag * jnp.cos(li), mag * jnp.sin(li)
    den = a_re * a_re + a_im * a_im
    nr, ni = ab_re - 1.0, ab_im
    bc_re = (nr * a_re + ni * a_im) / den
    bc_im = (ni * a_re - nr * a_im) / den
    bt_re = b_re * bc_re[..., None] - b_im * bc_im[..., None]
    bt_im = b_re * bc_im[..., None] + b_im * bc_re[..., None]
    gh = S5_GROUPS // S5_SPLIT
    eye = jnp.eye(gh, dtype=F32)
    split = lambda w: w.reshape((S5_SPLIT, gh) + w.shape[1:])
    blockdiag_in = lambda w: jnp.einsum("hgni,ge->hgien", split(w), eye).reshape(S5_SPLIT, gh * S5_GROUP, gh * S5_STATE)
    bt = jnp.concatenate([blockdiag_in(bt_re), blockdiag_in(bt_im)], axis=2).astype(BF16)
    blockdiag_out = lambda w: jnp.einsum("hgon,ge->hgneo", split(w), eye).reshape(S5_SPLIT, gh * S5_STATE, gh * S5_GROUP)
    ct = jnp.concatenate([blockdiag_out(c_re), -blockdiag_out(c_im)], axis=1).astype(BF16)
    k = jnp.arange(S5_CHUNK, dtype=F32)[:, None]
    lr, li = lr.reshape(1, S5_COLS), li.reshape(1, S5_COLS)
    powers = jnp.stack([jnp.exp(k * lr) * jnp.cos(k * li), jnp.exp(k * lr) * jnp.sin(k * li),
                        jnp.exp(-k * lr) * jnp.cos(k * li), -jnp.exp(-k * lr) * jnp.sin(k * li)])
    a1 = jnp.concatenate([ab_re.reshape(1, S5_COLS), ab_im.reshape(1, S5_COLS)], axis=0)
    ltri = jnp.asarray(np.kron(np.eye(S5_TILE // S5_CHUNK), np.tril(np.ones((S5_CHUNK, S5_CHUNK)))), dtype=BF16)
    return bt, powers, a1, ltri, ct


def kernel(x, p, norm_ffn1, ffn1_w1, ffn1_w3, ffn1_w2, norm_mix, w_in, cmp_pe_k, cmp_pe_v, cmp_wk1, cmp_wk2,
           cmp_wv1, cmp_wv2, s5_a_re, s5_a_im, s5_log_dt, s5_b_re, s5_b_im, s5_c_re, s5_c_im, s5_d, s5_w_glu,
           s5_b_glu, w_out, norm_ffn2, ffn2_w1, ffn2_w3, ffn2_w2, norm_ple, w_ple_gate, w_ple, norm_final):
    batch, seq, _ = x.shape
    depth = p.shape[0]
    t = batch * seq
    assert seq % TK_SEL == 0 and TK_SEL % TQ_SEL == 0 and seq % TQ_SEL == 0 and seq % S5_TILE == 0 and seq >= WINDOW + TQ
    assert seq // SEL_BLOCK <= MAX_SEL_BLOCKS and seq // SEL_BLOCK >= SEL_TOPK
    bf = lambda w: w.astype(BF16)
    row = lambda v: v.reshape(1, -1)

    cos, sin_signed = _rope_tables(seq)
    overlap_t = _overlap_matrix(seq // CMP_STRIDE)
    h = x.reshape(t, D_MODEL)
    for i in range(depth):
        h = _ffn(h, row(norm_ffn1[i]), bf(ffn1_w1[i]), bf(ffn1_w3[i]), bf(ffn1_w2[i]))

        qt, kc, vc, ksel, vsel_t, kwin, vwin_t, gates_t, u = _proj(
            h, row(norm_mix[i]), _rearrange_w_in(w_in[i]), cos, sin_signed, batch, seq)
        pe_k2, wkt, wkb, wk2 = _compress_weights(cmp_pe_k[i], cmp_wk1[i], cmp_wk2[i])
        pe_v2, wvt, wvb, wv2 = _compress_weights(cmp_pe_v[i], cmp_wv1[i], cmp_wv2[i])
        kcmp, vcmp_t = _compress(kc, vc, pe_k2, pe_v2, wkt, wkb, wvt, wvb, wk2, wv2)
        o_part_t, bias_t = _cmpwin(qt, kcmp, vcmp_t, overlap_t, kwin, vwin_t, gates_t)
        o_nsa = _select(qt, bias_t, ksel, vsel_t, gates_t, o_part_t)

        bt, powers, a1, ltri, ct = _s5_tables(s5_a_re[i], s5_a_im[i], s5_log_dt[i], s5_b_re[i], s5_b_im[i],
                                              s5_c_re[i], s5_c_im[i])
        o_s5 = _s5(u, bt, powers, a1, ltri, ct, row(s5_d[i]), bf(s5_w_glu[i]), row(s5_b_glu[i]))

        h = _ffn(h, row(norm_ffn2[i]), bf(ffn2_w1[i]), bf(ffn2_w3[i]), bf(ffn2_w2[i]),
                 mix=(o_nsa.reshape(t, NSA_WIDTH), o_s5.reshape(t, S5_WIDTH),
                      bf(w_out[i][:NSA_WIDTH]), bf(w_out[i][NSA_WIDTH:])))
        h = _ple(h, row(norm_ple[i]), bf(w_ple_gate[i]), p[i].reshape(t, PLE_DIM), bf(w_ple[i]),
                 row(norm_final) if i == depth - 1 else None)
    return h.reshape(batch, seq, D_MODEL)
```

```python
import functools

import jax
import jax.numpy as jnp
import numpy as np
from jax import lax
from jax.experimental import pallas as pl
from jax.experimental.pallas import tpu as pltpu

D_MODEL = 1024
PLE_DIM = 256
D_FF = 2816
N_HEADS = 8
N_KV_HEADS = 2
HEAD_DIM = 64
HEADS_PER_GROUP = N_HEADS // N_KV_HEADS
NSA_WIDTH = N_HEADS * HEAD_DIM
KV_WIDTH = N_KV_HEADS * HEAD_DIM
S5_WIDTH = D_MODEL - NSA_WIDTH
S5_GROUP = 16
S5_GROUPS = S5_WIDTH // S5_GROUP
S5_STATE = 64
S5_COLS = S5_GROUPS * S5_STATE
CMP_LEN = 32
CMP_STRIDE = 16
CMP_HIDDEN = 256
SEL_BLOCK = 64
SEL_TOPK = 16
WINDOW = 512
ROPE_THETA = 10000.0
RMS_EPS = 1e-6
NEG = -1e30
BIG = 1e9

LANES = 128
V_ROWS = HEAD_DIM + 16
GATE_ROWS = 16
MAX_SEL_BLOCKS = LANES
VMEM_LIMIT = 52 * 1024 * 1024

TM_DENSE = 512
FF_CHUNK = 256
TQ = 256
WIN_SUB = 128
CMP_STEP = 128
TQ_SEL = 512
TK_SEL = 512
S5_TILE = 256
S5_CHUNK = 64
S5_SPLIT = 2

F32 = jnp.float32
BF16 = jnp.bfloat16


def _dot(a, b):
    return jnp.dot(a, b, preferred_element_type=F32)


def _rmsnorm(x, g):
    ms = jnp.mean(x * x, axis=-1, keepdims=True)
    return x * lax.rsqrt(ms + RMS_EPS) * g


def _gelu_tanh(x):
    return 0.5 * x * (1.0 + jnp.tanh(np.sqrt(2.0 / np.pi).astype(np.float32) * (x + 0.044715 * (x * x * x))))


def _params(n_grid):
    return pltpu.CompilerParams(dimension_semantics=("arbitrary",) * n_grid, vmem_limit_bytes=VMEM_LIMIT)


def _resident():
    return pl.BlockSpec(memory_space=pltpu.VMEM)


def _ffn_body(*refs, with_mix):
    if with_mix:
        h_ref, on_ref, os_ref, won_ref, wos_ref, g_ref, w1_ref, w3_ref, w2_ref, o_ref = refs
        h = h_ref[...] + _dot(on_ref[...], won_ref[...]) + _dot(os_ref[...], wos_ref[...])
    else:
        h_ref, g_ref, w1_ref, w3_ref, w2_ref, o_ref = refs
        h = h_ref[...]
    xn = _rmsnorm(h, g_ref[...]).astype(BF16)
    acc = jnp.zeros(h.shape, F32)
    for j in range(D_FF // FF_CHUNK):
        sl = slice(j * FF_CHUNK, (j + 1) * FF_CHUNK)
        a = _dot(xn, w1_ref[:, sl])
        b = _dot(xn, w3_ref[:, sl])
        act = (a * jax.nn.sigmoid(a) * b).astype(BF16)
        acc = acc + _dot(act, w2_ref[sl, :])
    o_ref[...] = h + 0.5 * acc


def _ffn(h, g, w1, w3, w2, mix=None):
    t = h.shape[0]
    tm = min(TM_DENSE, t)
    row = lambda w: pl.BlockSpec((tm, w), lambda i: (i, 0))
    args, specs = [h], [row(D_MODEL)]
    if mix is not None:
        o_nsa, o_s5, wo_n, wo_s = mix
        args += [o_nsa, o_s5, wo_n, wo_s]
        specs += [row(NSA_WIDTH), row(S5_WIDTH), _resident(), _resident()]
    args += [g, w1, w3, w2]
    specs += [_resident()] * 4
    return pl.pallas_call(
        functools.partial(_ffn_body, with_mix=mix is not None),
        grid=(t // tm,),
        in_specs=specs,
        out_specs=row(D_MODEL),
        out_shape=jax.ShapeDtypeStruct((t, D_MODEL), F32),
        compiler_params=_params(1),
        name="ffn_mix" if mix is not None else "ffn",
    )(*args)


_C_Q = 0
_C_CMP = NSA_WIDTH
_C_SLC = _C_CMP + 2 * KV_WIDTH
_C_WIN = _C_SLC + 2 * KV_WIDTH
_C_U = _C_WIN + 2 * KV_WIDTH
_C_G = _C_U + S5_WIDTH
_C_END = _C_G + N_KV_HEADS * LANES


def _rope_pair(x, cos, sin_signed, first_half):
    fwd = pltpu.roll(x, HEAD_DIM // 2, 1)
    bwd = pltpu.roll(x, LANES - HEAD_DIM // 2, 1)
    return x * cos + jnp.where(first_half, bwd, fwd) * sin_signed


def _proj_body(h_ref, g_ref, w_ref, cos_ref, sin_ref,
               qt_ref, kc_ref, vc_ref, ks_ref, vst_ref, kw_ref, vwt_ref, gtt_ref, u_ref, *, tiles_per_seq):
    tm = h_ref.shape[0]
    xn = _rmsnorm(h_ref[...], g_ref[...]).astype(BF16)
    cos = cos_ref[...]
    sin = sin_ref[...]
    lane = lax.broadcasted_iota(jnp.int32, (tm, LANES), 1)
    first_half = (lane % HEAD_DIM) < (HEAD_DIM // 2)
    rope = lambda x: _rope_pair(x, cos, sin, first_half)

    scale = HEAD_DIM ** -0.5
    for j in range(NSA_WIDTH // 256):
        z = _dot(xn, w_ref[:, _C_Q + 256 * j:_C_Q + 256 * (j + 1)])
        for k in range(2):
            rt = (rope(z[:, LANES * k:LANES * (k + 1)]) * scale).T.astype(BF16)
            for e in range(2):
                qt_ref[4 * j + 2 * k + e] = rt[e * HEAD_DIM:(e + 1) * HEAD_DIM, :]

    z = _dot(xn, w_ref[:, _C_CMP:_C_CMP + 256])
    kc_ref[...] = rope(z[:, :LANES])
    vc_ref[...] = z[:, LANES:]

    zeros = jnp.zeros((tm, HEAD_DIM), BF16)
    pos = (pl.program_id(0) % tiles_per_seq) * tm + lax.broadcasted_iota(jnp.int32, (tm, LANES), 0)
    onehot = jnp.where(lane == pos // SEL_BLOCK, 1.0, 0.0).astype(BF16)

    z = _dot(xn, w_ref[:, _C_SLC:_C_SLC + 256])
    k = rope(z[:, :LANES]).astype(BF16)
    vt = z[:, LANES:].T.astype(BF16)
    for g in range(N_KV_HEADS):
        gs = slice(g * HEAD_DIM, (g + 1) * HEAD_DIM)
        ks_ref[g, :, 0:LANES] = onehot
        ks_ref[g, :, LANES:LANES + HEAD_DIM] = k[:, gs]
        ks_ref[g, :, LANES + HEAD_DIM:] = zeros
        vst_ref[g, 0:HEAD_DIM, :] = vt[gs, :]
        vst_ref[g, HEAD_DIM:, :] = jnp.ones((V_ROWS - HEAD_DIM, tm), BF16)

    z = _dot(xn, w_ref[:, _C_WIN:_C_WIN + 256])
    k = rope(z[:, :LANES]).astype(BF16)
    vt = z[:, LANES:].T.astype(BF16)
    for g in range(N_KV_HEADS):
        gs = slice(g * HEAD_DIM, (g + 1) * HEAD_DIM)
        kw_ref[g] = k[:, gs]
        for c in range(tm // WIN_SUB):
            vwt_ref[g, c, 0:HEAD_DIM, :] = vt[gs, c * WIN_SUB:(c + 1) * WIN_SUB]
            vwt_ref[g, c, HEAD_DIM:, :] = jnp.ones((V_ROWS - HEAD_DIM, WIN_SUB), BF16)

    for j in range(S5_WIDTH // 256):
        u_ref[:, 256 * j:256 * (j + 1)] = _dot(xn, w_ref[:, _C_U + 256 * j:_C_U + 256 * (j + 1)])
    gates_t = jax.nn.sigmoid(_dot(xn, w_ref[:, _C_G:_C_END])).T
    for g in range(N_KV_HEADS):
        gtt_ref[g] = gates_t[g * LANES:g * LANES + GATE_ROWS, :]


def _proj(h1, g, w_in_r, cos, sin_signed, batch, seq):
    t = batch * seq
    tm = TK_SEL
    n = seq // tm
    bl = lambda w: pl.BlockSpec((None, tm, w), lambda i: (i // n, i % n, 0))
    bgl = lambda w: pl.BlockSpec((None, N_KV_HEADS, tm, w), lambda i: (i // n, 0, i % n, 0))
    tab = pl.BlockSpec((tm, LANES), lambda i: (i % n, 0))
    sds = jax.ShapeDtypeStruct
    return pl.pallas_call(
        functools.partial(_proj_body, tiles_per_seq=n),
        grid=(t // tm,),
        in_specs=[pl.BlockSpec((tm, D_MODEL), lambda i: (i, 0)), _resident(), _resident(), tab, tab],
        out_specs=[
            pl.BlockSpec((None, N_HEADS, HEAD_DIM, tm), lambda i: (i // n, 0, 0, i % n)),
            bl(KV_WIDTH), bl(KV_WIDTH),
            bgl(2 * LANES),
            pl.BlockSpec((None, N_KV_HEADS, None, V_ROWS, tm), lambda i: (i // n, 0, i % n, 0, 0)),
            bgl(HEAD_DIM),
            pl.BlockSpec((None, N_KV_HEADS, tm // WIN_SUB, V_ROWS, WIN_SUB), lambda i: (i // n, 0, i % n, 0, 0)),
            pl.BlockSpec((None, N_KV_HEADS, GATE_ROWS, tm), lambda i: (i // n, 0, 0, i % n)),
            bl(S5_WIDTH),
        ],
        out_shape=[
            sds((batch, N_HEADS, HEAD_DIM, seq), BF16),
            sds((batch, seq, KV_WIDTH), F32), sds((batch, seq, KV_WIDTH), F32),
            sds((batch, N_KV_HEADS, seq, 2 * LANES), BF16), sds((batch, N_KV_HEADS, n, V_ROWS, tm), BF16),
            sds((batch, N_KV_HEADS, seq, HEAD_DIM), BF16), sds((batch, N_KV_HEADS, seq // WIN_SUB, V_ROWS, WIN_SUB), BF16),
            sds((batch, N_KV_HEADS, GATE_ROWS, seq), F32), sds((batch, seq, S5_WIDTH), F32),
        ],
        compiler_params=_params(1),
        name="proj",
    )(h1.reshape(t, D_MODEL), g, w_in_r, cos, sin_signed)


def _compress_body(kc_ref, vc_ref, pek_ref, pev_ref, wkt_ref, wkb_ref, wvt_ref, wvb_ref, wk2_ref, wv2_ref,
                   ko_ref, vot_ref, v_ref):
    nc = kc_ref.shape[0]

    def hidden(x, pe_ref, wt_ref, wb_ref):
        top = _dot((x + pe_ref[0:1, :]).astype(BF16), wt_ref[...])
        bot = _dot((x + pe_ref[1:2, :]).astype(BF16), wb_ref[...])
        pre = top + pltpu.roll(bot, nc - 1, 0)
        return _gelu_tanh(pre).astype(BF16)

    hk = hidden(kc_ref[...], pek_ref, wkt_ref, wkb_ref)
    hv = hidden(vc_ref[...], pev_ref, wvt_ref, wvb_ref)
    for g in range(N_KV_HEADS):
        gs = slice(g * CMP_HIDDEN, (g + 1) * CMP_HIDDEN)
        ko_ref[g] = _dot(hk[:, gs], wk2_ref[...]).astype(BF16)
        v_ref[...] = _dot(hv[:, gs], wv2_ref[...])
        vot_ref[g] = v_ref[...].T.astype(BF16)


def _compress(kc, vc, pe_k2, pe_v2, wkt, wkb, wvt, wvb, wk2, wv2):
    batch, seq, _ = kc.shape
    nc = seq // CMP_STRIDE
    width = CMP_STRIDE * KV_WIDTH
    x_spec = pl.BlockSpec((None, nc, width), lambda b: (b, 0, 0))
    sds = jax.ShapeDtypeStruct
    return pl.pallas_call(
        _compress_body,
        grid=(batch,),
        in_specs=[x_spec, x_spec] + [_resident()] * 8,
        out_specs=[pl.BlockSpec((None, N_KV_HEADS, nc, HEAD_DIM), lambda b: (b, 0, 0, 0)),
                   pl.BlockSpec((None, N_KV_HEADS, HEAD_DIM, nc), lambda b: (b, 0, 0, 0))],
        out_shape=[sds((batch, N_KV_HEADS, nc, HEAD_DIM), BF16), sds((batch, N_KV_HEADS, HEAD_DIM, nc), BF16)],
        scratch_shapes=[pltpu.VMEM((nc, HEAD_DIM), F32)],
        compiler_params=_params(1),
        name="compress",
    )(kc.reshape(batch, nc, width), vc.reshape(batch, nc, width), pe_k2, pe_v2, wkt, wkb, wvt, wvb, wk2, wv2)


def _cmpwin_body(qt_ref, kc_ref, vct_ref, ovt_ref, kw_ref, vwt_ref, gtt_ref, opt_ref, bias_ref,
                 s_ref, ocmp_ref, imp_ref):
    tq = qt_ref.shape[2]
    cols = HEADS_PER_GROUP * tq
    nc = kc_ref.shape[0]
    i = pl.program_id(2)
    q_t = jnp.concatenate([qt_ref[h] for h in range(HEADS_PER_GROUP)], axis=1)
    t_lane = i * tq + lax.broadcasted_iota(jnp.int32, (1, cols), 1) % tq

    def compressed(n):
        s = _dot(kc_ref[0:n, :], q_t)
        c_end = lax.broadcasted_iota(jnp.int32, (n, cols), 0) * CMP_STRIDE + (CMP_LEN - 1)
        s = jnp.where(c_end <= t_lane, s, NEG)
        e = jnp.exp(s - jnp.max(s, axis=0, keepdims=True))
        den = jnp.maximum(jnp.sum(e, axis=0, keepdims=True), 1e-30)
        p = e * jnp.where(t_lane >= CMP_LEN - 1, 1.0 / den, 0.0)
        ocmp_ref[...] = _dot(vct_ref[:, 0:n], p.astype(BF16))
        psum = p[:, 0:tq]
        for h in range(1, HEADS_PER_GROUP):
            psum = psum + p[:, h * tq:(h + 1) * tq]
        p_hi = psum.astype(BF16)
        p_lo = (psum - p_hi.astype(F32)).astype(BF16)
        imp_ref[...] = _dot(ovt_ref[:, 0:n], p_hi) + _dot(ovt_ref[:, 0:n], p_lo)

    finished = (i * tq + tq - CMP_LEN) // CMP_STRIDE + 1
    n_steps = nc // CMP_STEP
    for k in range(1, n_steps + 1):
        in_step = finished > (k - 1) * CMP_STEP
        if k < n_steps:
            in_step = jnp.logical_and(in_step, finished <= k * CMP_STEP)
        pl.when(in_step)(functools.partial(compressed, k * CMP_STEP))
    o_cmp = ocmp_ref[...]
    imp = imp_ref[...]

    blk = lax.broadcasted_iota(jnp.int32, (MAX_SEL_BLOCKS, tq), 0)
    behind = t_lane[:, 0:tq] // SEL_BLOCK - blk
    score = jnp.where(blk == 0, BIG, jnp.where(behind == 0, BIG, jnp.where(behind == 1, BIG, imp)))
    score = jnp.where(behind >= 0, score, -BIG)
    chosen = jnp.zeros((MAX_SEL_BLOCKS, tq), F32)
    for _ in range(SEL_TOPK):
        best = jnp.max(score, axis=0, keepdims=True)
        first = jnp.min(jnp.where(score == best, blk, MAX_SEL_BLOCKS), axis=0, keepdims=True)
        hit = blk == first
        chosen = jnp.where(hit, 1.0, chosen)
        score = jnp.where(hit, -jnp.inf, score)
    bias_ref[...] = jnp.where(chosen > 0.0, jnp.where(behind >= 0, 0.0, NEG), NEG).astype(BF16)

    span = WINDOW + tq
    start = pl.multiple_of(jnp.maximum(i * tq - WINDOW, 0), tq)
    s_ref[...] = _dot(kw_ref[pl.ds(start, span), :], q_t)

    def band(rows):
        kpos = start + rows.start + lax.broadcasted_iota(jnp.int32, (rows.stop - rows.start, cols), 0)
        dist = lax.bitcast_convert_type(t_lane - kpos, jnp.uint32)
        s_ref[rows, :] = jnp.where(dist < WINDOW, s_ref[rows, :], NEG)

    @pl.when(i * tq >= WINDOW)
    def _():
        band(slice(0, tq))
        band(slice(span - tq, span))

    @pl.when(i * tq < WINDOW)
    def _():
        band(slice(0, span))

    s = s_ref[...]
    e = jnp.exp(s - jnp.max(s, axis=0, keepdims=True)).astype(BF16)
    c0 = start // WIN_SUB
    vw_t = jnp.concatenate([vwt_ref[c0 + j] for j in range(span // WIN_SUB)], axis=1)
    acc = _dot(vw_t, e)
    o_win = acc[0:HEAD_DIM, :] / acc[HEAD_DIM:HEAD_DIM + 1, :]

    gates = gtt_ref[...]
    for h in range(HEADS_PER_GROUP):
        hs = slice(h * tq, (h + 1) * tq)
        opt_ref[h * HEAD_DIM:(h + 1) * HEAD_DIM, :] = (gates[3 * h:3 * h + 1, :] * o_cmp[:, hs]
                                                       + gates[3 * h + 2:3 * h + 3, :] * o_win[:, hs])


def _cmpwin(qt, kcmp, vcmp_t, overlap_t, kwin, vwin_t, gates_t):
    batch, _, _, seq = qt.shape
    nc = kcmp.shape[2]
    tq = TQ
    per_bg = lambda *shape: pl.BlockSpec((None, None) + shape, lambda b, g, i: (b, g) + (0,) * len(shape))
    sds = jax.ShapeDtypeStruct
    return pl.pallas_call(
        _cmpwin_body,
        grid=(batch, N_KV_HEADS, seq // tq),
        in_specs=[
            pl.BlockSpec((None, HEADS_PER_GROUP, HEAD_DIM, tq), lambda b, g, i: (b, g, 0, i)),
            per_bg(nc, HEAD_DIM), per_bg(HEAD_DIM, nc), _resident(),
            per_bg(seq, HEAD_DIM), per_bg(seq // WIN_SUB, V_ROWS, WIN_SUB),
            pl.BlockSpec((None, None, GATE_ROWS, tq), lambda b, g, i: (b, g, 0, i)),
        ],
        out_specs=[pl.BlockSpec((None, None, HEADS_PER_GROUP * HEAD_DIM, tq), lambda b, g, i: (b, g, 0, i)),
                   pl.BlockSpec((None, None, MAX_SEL_BLOCKS, tq), lambda b, g, i: (b, g, 0, i))],
        out_shape=[sds((batch, N_KV_HEADS, HEADS_PER_GROUP * HEAD_DIM, seq), F32),
                   sds((batch, N_KV_HEADS, MAX_SEL_BLOCKS, seq), BF16)],
        scratch_shapes=[pltpu.VMEM((WINDOW + tq, HEADS_PER_GROUP * tq), F32),
                        pltpu.VMEM((HEAD_DIM, HEADS_PER_GROUP * tq), F32), pltpu.VMEM((MAX_SEL_BLOCKS, tq), F32)],
        compiler_params=_params(3),
        name="cmpwin",
    )(qt, kcmp, vcmp_t, overlap_t, kwin, vwin_t, gates_t)


def _select_body(qt_ref, bias_ref, ks_ref, vst_ref, gtt_ref, opt_ref, o_ref, qa_ref, s_ref, mx_ref, m_ref, acc_ref):
    tq = qt_ref.shape[2]
    cols = HEADS_PER_GROUP * tq
    tk = TK_SEL
    i = pl.program_id(2)

    bias = bias_ref[...]
    for h in range(HEADS_PER_GROUP):
        qa_ref[0:LANES, h * tq:(h + 1) * tq] = bias
        qa_ref[LANES:LANES + HEAD_DIM, h * tq:(h + 1) * tq] = qt_ref[h]
    qa_ref[LANES + HEAD_DIM:, :] = jnp.zeros((LANES - HEAD_DIM, cols), BF16)
    m_ref[...] = jnp.full(m_ref.shape, NEG, F32)
    acc_ref[...] = jnp.zeros(acc_ref.shape, F32)

    def scores(c, causal):
        off = pl.multiple_of(c * tk, tk)
        s = _dot(ks_ref[pl.ds(off, tk), :], qa_ref[...])
        if causal:
            kpos = off + lax.broadcasted_iota(jnp.int32, (tk, cols), 0)
            t = i * tq + lax.broadcasted_iota(jnp.int32, (tk, cols), 1) % tq
            s = jnp.where(kpos <= t, s, NEG)
        s_ref[c % 2] = s
        mx_ref[c % 2] = jnp.max(s, axis=0, keepdims=True)

    def accumulate(c):
        m_old = m_ref[...]
        m_new = jnp.maximum(m_old, mx_ref[c % 2])
        p = jnp.exp(s_ref[c % 2] - m_new).astype(BF16)
        acc_ref[...] = jnp.exp(m_old - m_new) * acc_ref[...] + _dot(vst_ref[c], p)
        m_ref[...] = m_new

    n_full = (i * tq) // tk

    @pl.when(n_full > 0)
    def _():
        scores(0, False)

    def pair(c, carry):
        accumulate(c)
        scores(c + 1, False)
        return carry

    lax.fori_loop(0, n_full - 1, pair, 0)

    @pl.when(n_full > 0)
    def _():
        accumulate(n_full - 1)
        scores(n_full, True)

    @pl.when(n_full == 0)
    def _():
        scores(n_full, True)

    accumulate(n_full)

    acc = acc_ref[...]
    o_slc = acc[0:HEAD_DIM, :] / acc[HEAD_DIM:HEAD_DIM + 1, :]
    gates = gtt_ref[...]
    o_t = jnp.concatenate([gates[3 * h + 1:3 * h + 2, :] * o_slc[:, h * tq:(h + 1) * tq]
                           for h in range(HEADS_PER_GROUP)], axis=0)
    o_ref[...] = (opt_ref[...] + o_t).T.astype(BF16)


def _select(qt, bias_t, ksel, vsel_t, gates_t, o_part_t):
    batch, _, _, seq = qt.shape
    tq = TQ_SEL
    cols = HEADS_PER_GROUP * tq
    n_chunks = seq // TK_SEL
    tile_t = lambda r: pl.BlockSpec((None, None, r, tq), lambda b, g, i: (b, g, 0, i))
    return pl.pallas_call(
        _select_body,
        grid=(batch, N_KV_HEADS, seq // tq),
        in_specs=[
            pl.BlockSpec((None, HEADS_PER_GROUP, HEAD_DIM, tq), lambda b, g, i: (b, g, 0, i)),
            tile_t(MAX_SEL_BLOCKS),
            pl.BlockSpec((None, None, seq, 2 * LANES), lambda b, g, i: (b, g, 0, 0)),
            pl.BlockSpec((None, None, n_chunks, V_ROWS, TK_SEL), lambda b, g, i: (b, g, 0, 0, 0)),
            tile_t(GATE_ROWS), tile_t(HEADS_PER_GROUP * HEAD_DIM),
        ],
        out_specs=pl.BlockSpec((None, tq, HEADS_PER_GROUP * HEAD_DIM), lambda b, g, i: (b, i, g)),
        out_shape=jax.ShapeDtypeStruct((batch, seq, NSA_WIDTH), BF16),
        scratch_shapes=[pltpu.VMEM((2 * LANES, cols), BF16), pltpu.VMEM((2, TK_SEL, cols), F32),
                        pltpu.VMEM((2, 1, cols), F32), pltpu.VMEM((1, cols), F32), pltpu.VMEM((V_ROWS, cols), F32)],
        compiler_params=_params(3),
        name="select",
    )(qt, bias_t, ksel, vsel_t, gates_t, o_part_t)


def _s5_body(u_ref, bt_ref, pw_ref, a1_ref, ltri_ref, ct_ref, d_ref, wg_ref, bg_ref, o_ref, c_ref, x_ref, h_ref):
    @pl.when(pl.program_id(1) == 0)
    def _():
        c_ref[...] = jnp.zeros(c_ref.shape, F32)

    n = S5_COLS // S5_SPLIT
    u = u_ref[...]
    last = S5_CHUNK - 1
    ys = []
    for hf in range(S5_SPLIT):
        cs = slice(hf * n, (hf + 1) * n)
        p_re, p_im, q_re, q_im = pw_ref[0, :, cs], pw_ref[1, :, cs], pw_ref[2, :, cs], pw_ref[3, :, cs]
        a_re, a_im = a1_ref[0:1, cs], a1_ref[1:2, cs]
        width = S5_WIDTH // S5_SPLIT
        bu = _dot(u[:, hf * width:(hf + 1) * width].astype(BF16), bt_ref[hf])
        for j in range(u.shape[0] // S5_CHUNK):
            r = slice(j * S5_CHUNK, (j + 1) * S5_CHUNK)
            b_re, b_im = bu[r, 0:n], bu[r, n:2 * n]
            x_ref[r, 0:n] = (q_re * b_re - q_im * b_im).astype(BF16)
            x_ref[r, n:2 * n] = (q_re * b_im + q_im * b_re).astype(BF16)
        local = _dot(ltri_ref[...], x_ref[...])
        c_re, c_im = c_ref[0:1, cs], c_ref[1:2, cs]
        for j in range(u.shape[0] // S5_CHUNK):
            r = slice(j * S5_CHUNK, (j + 1) * S5_CHUNK)
            s_re, s_im = local[r, 0:n] + c_re, local[r, n:2 * n] + c_im
            h_ref[r, 0:n] = (p_re * s_re - p_im * s_im).astype(BF16)
            h_ref[r, n:2 * n] = (p_re * s_im + p_im * s_re).astype(BF16)
            e_re, e_im = s_re[last:last + 1], s_im[last:last + 1]
            l_re = p_re[last:last + 1] * e_re - p_im[last:last + 1] * e_im
            l_im = p_re[last:last + 1] * e_im + p_im[last:last + 1] * e_re
            c_re = a_re * l_re - a_im * l_im
            c_im = a_re * l_im + a_im * l_re
        c_ref[0:1, cs] = c_re
        c_ref[1:2, cs] = c_im
        ys.append(_dot(h_ref[...], ct_ref[hf]))
    y = _gelu_tanh(jnp.concatenate(ys, axis=1) + d_ref[...] * u)
    o_ref[...] = (y * jax.nn.sigmoid(_dot(y.astype(BF16), wg_ref[...]) + bg_ref[...])).astype(BF16)


def _s5(u, bt, powers, a1, ltri, ct, d, w_glu, b_glu):
    batch, seq, _ = u.shape
    ts = S5_TILE
    tile = pl.BlockSpec((None, ts, S5_WIDTH), lambda b, j: (b, j, 0))
    cols = 2 * S5_COLS // S5_SPLIT
    return pl.pallas_call(
        _s5_body,
        grid=(batch, seq // ts),
        in_specs=[tile] + [_resident()] * 8,
        out_specs=tile,
        out_shape=jax.ShapeDtypeStruct((batch, seq, S5_WIDTH), BF16),
        scratch_shapes=[pltpu.VMEM((2, S5_COLS), F32), pltpu.VMEM((ts, cols), BF16), pltpu.VMEM((ts, cols), BF16)],
        compiler_params=_params(2),
        name="s5",
    )(u, bt, powers, a1, ltri, ct, d, w_glu, b_glu)


def _ple_body(*refs, final_norm):
    if final_norm:
        h_ref, gp_ref, wg_ref, p_ref, wp_ref, gf_ref, o_ref = refs
    else:
        h_ref, gp_ref, wg_ref, p_ref, wp_ref, o_ref = refs
    h = h_ref[...]
    gate = jax.nn.sigmoid(_dot(_rmsnorm(h, gp_ref[...]).astype(BF16), wg_ref[...]))
    h = h + gate * _dot(p_ref[...].astype(BF16), wp_ref[...])
    o_ref[...] = _rmsnorm(h, gf_ref[...]) if final_norm else h


def _ple(h, g_ple, w_gate, p, w_ple, g_final=None):
    t = h.shape[0]
    tm = min(TM_DENSE, t)
    row = lambda w: pl.BlockSpec((tm, w), lambda i: (i, 0))
    args = [h, g_ple, w_gate, p, w_ple]
    specs = [row(D_MODEL), _resident(), _resident(), row(PLE_DIM), _resident()]
    if g_final is not None:
        args.append(g_final)
        specs.append(_resident())
    return pl.pallas_call(
        functools.partial(_ple_body, final_norm=g_final is not None),
        grid=(t // tm,),
        in_specs=specs,
        out_specs=row(D_MODEL),
        out_shape=jax.ShapeDtypeStruct((t, D_MODEL), F32),
        compiler_params=_params(1),
        name="ple",
    )(*args)


def _rearrange_w_in(w_in):
    g0 = NSA_WIDTH + 6 * KV_WIDTH
    per_group = 3 * HEADS_PER_GROUP
    pad = jnp.zeros((D_MODEL, LANES - per_group), w_in.dtype)
    gate_cols = []
    for g in range(N_KV_HEADS):
        gate_cols += [w_in[:, g0 + g * per_group:g0 + (g + 1) * per_group], pad]
    u0 = g0 + 3 * N_HEADS
    return jnp.concatenate([w_in[:, :g0], w_in[:, u0:]] + gate_cols, axis=1).astype(BF16)


def _rope_tables(seq):
    half = HEAD_DIM // 2
    inv = ROPE_THETA ** (-jnp.arange(half, dtype=F32) / half)
    ang = jnp.arange(seq, dtype=F32)[:, None] * inv[None, :]
    cos = jnp.tile(jnp.cos(ang), (1, LANES // half))
    sin = jnp.tile(jnp.concatenate([-jnp.sin(ang), jnp.sin(ang)], axis=1), (1, LANES // HEAD_DIM))
    return cos, sin


def _compress_weights(pe, w1, w2):
    w1 = w1.reshape(2, CMP_STRIDE, HEAD_DIM, CMP_HIDDEN)
    eye = jnp.eye(N_KV_HEADS, dtype=w1.dtype)
    big = jnp.einsum("sjdh,ge->sjgdeh", w1, eye).reshape(2, CMP_STRIDE * KV_WIDTH, N_KV_HEADS * CMP_HIDDEN)
    pe2 = jnp.broadcast_to(pe.reshape(2, CMP_STRIDE, 1, HEAD_DIM), (2, CMP_STRIDE, N_KV_HEADS, HEAD_DIM))
    return pe2.reshape(2, CMP_STRIDE * KV_WIDTH), big[0].astype(BF16), big[1].astype(BF16), w2.astype(BF16)


def _overlap_matrix(n_cmp_rows):
    c_start = np.arange(n_cmp_rows)[:, None] * CMP_STRIDE
    s_start = np.arange(MAX_SEL_BLOCKS)[None, :] * SEL_BLOCK
    ov = (c_start < s_start + SEL_BLOCK) & (c_start + CMP_LEN > s_start)
    return jnp.asarray(ov.T, dtype=BF16)


def _s5_tables(a_re, a_im, log_dt, b_re, b_im, c_re, c_im):
    dt = jnp.exp(log_dt)[:, None]
    lr, li = a_re * dt, a_im * dt
    mag = jnp.exp(lr)
    ab_re, ab_im = mag * jnp.cos(li), mag * jnp.sin(li)
    den = a_re * a_re + a_im * a_im
    nr, ni = ab_re - 1.0, ab_im
    bc_re = (nr * a_re + ni * a_im) / den
    bc_im = (ni * a_re - nr * a_im) / den
    bt_re = b_re * bc_re[..., None] - b_im * bc_im[..., None]
    bt_im = b_re * bc_im[..., None] + b_im * bc_re[..., None]
    gh = S5_GROUPS // S5_SPLIT
    eye = jnp.eye(gh, dtype=F32)
    split = lambda w: w.reshape((S5_SPLIT, gh) + w.shape[1:])
    blockdiag_in = lambda w: jnp.einsum("hgni,ge->hgien", split(w), eye).reshape(S5_SPLIT, gh * S5_GROUP, gh * S5_STATE)
    bt = jnp.concatenate([blockdiag_in(bt_re), blockdiag_in(bt_im)], axis=2).astype(BF16)
    blockdiag_out = lambda w: jnp.einsum("hgon,ge->hgneo", split(w), eye).reshape(S5_SPLIT, gh * S5_STATE, gh * S5_GROUP)
    ct = jnp.concatenate([blockdiag_out(c_re), -blockdiag_out(c_im)], axis=1).astype(BF16)
    k = jnp.arange(S5_CHUNK, dtype=F32)[:, None]
    lr, li = lr.reshape(1, S5_COLS), li.reshape(1, S5_COLS)
    powers = jnp.stack([jnp.exp(k * lr) * jnp.cos(k * li), jnp.exp(k * lr) * jnp.sin(k * li),
                        jnp.exp(-k * lr) * jnp.cos(k * li), -jnp.exp(-k * lr) * jnp.sin(k * li)])
    a1 = jnp.concatenate([ab_re.reshape(1, S5_COLS), ab_im.reshape(1, S5_COLS)], axis=0)
    ltri = jnp.asarray(np.kron(np.eye(S5_TILE // S5_CHUNK), np.tril(np.ones((S5_CHUNK, S5_CHUNK)))), dtype=BF16)
    return bt, powers, a1, ltri, ct


def kernel(x, p, norm_ffn1, ffn1_w1, ffn1_w3, ffn1_w2, norm_mix, w_in, cmp_pe_k, cmp_pe_v, cmp_wk1, cmp_wk2,
           cmp_wv1, cmp_wv2, s5_a_re, s5_a_im, s5_log_dt, s5_b_re, s5_b_im, s5_c_re, s5_c_im, s5_d, s5_w_glu,
           s5_b_glu, w_out, norm_ffn2, ffn2_w1, ffn2_w3, ffn2_w2, norm_ple, w_ple_gate, w_ple, norm_final):
    batch, seq, _ = x.shape
    depth = p.shape[0]
    t = batch * seq
    assert seq % TK_SEL == 0 and TK_SEL % TQ_SEL == 0 and seq % TQ_SEL == 0 and seq % S5_TILE == 0 and seq >= WINDOW + TQ and seq % (CMP_STRIDE * CMP_STEP) == 0
    assert seq // SEL_BLOCK <= MAX_SEL_BLOCKS and seq // SEL_BLOCK >= SEL_TOPK
    bf = lambda w: w.astype(BF16)
    row = lambda v: v.reshape(1, -1)

    cos, sin_signed = _rope_tables(seq)
    overlap_t = _overlap_matrix(seq // CMP_STRIDE)
    h = x.reshape(t, D_MODEL)
    for i in range(depth):
        h = _ffn(h, row(norm_ffn1[i]), bf(ffn1_w1[i]), bf(ffn1_w3[i]), bf(ffn1_w2[i]))

        qt, kc, vc, ksel, vsel_t, kwin, vwin_t, gates_t, u = _proj(
            h, row(norm_mix[i]), _rearrange_w_in(w_in[i]), cos, sin_signed, batch, seq)
        pe_k2, wkt, wkb, wk2 = _compress_weights(cmp_pe_k[i], cmp_wk1[i], cmp_wk2[i])
        pe_v2, wvt, wvb, wv2 = _compress_weights(cmp_pe_v[i], cmp_wv1[i], cmp_wv2[i])
        kcmp, vcmp_t = _compress(kc, vc, pe_k2, pe_v2, wkt, wkb, wvt, wvb, wk2, wv2)
        o_part_t, bias_t = _cmpwin(qt, kcmp, vcmp_t, overlap_t, kwin, vwin_t, gates_t)
        o_nsa = _select(qt, bias_t, ksel, vsel_t, gates_t, o_part_t)

        bt, powers, a1, ltri, ct = _s5_tables(s5_a_re[i], s5_a_im[i], s5_log_dt[i], s5_b_re[i], s5_b_im[i],
                                              s5_c_re[i], s5_c_im[i])
        o_s5 = _s5(u, bt, powers, a1, ltri, ct, row(s5_d[i]), bf(s5_w_glu[i]), row(s5_b_glu[i]))

        h = _ffn(h, row(norm_ffn2[i]), bf(ffn2_w1[i]), bf(ffn2_w3[i]), bf(ffn2_w2[i]),
                 mix=(o_nsa.reshape(t, NSA_WIDTH), o_s5.reshape(t, S5_WIDTH),
                      bf(w_out[i][:NSA_WIDTH]), bf(w_out[i][NSA_WIDTH:])))
        h = _ple(h, row(norm_ple[i]), bf(w_ple_gate[i]), p[i].reshape(t, PLE_DIM), bf(w_ple[i]),
                 row(norm_final) if i == depth - 1 else None)
    return h.reshape(batch, seq, D_MODEL)
```

```python
import functools

import jax
import jax.numpy as jnp
import numpy as np
from jax import lax
from jax.experimental import pallas as pl
from jax.experimental.pallas import tpu as pltpu

D_MODEL = 1024
PLE_DIM = 256
D_FF = 2816
N_HEADS = 8
N_KV_HEADS = 2
HEAD_DIM = 64
HEADS_PER_GROUP = N_HEADS // N_KV_HEADS
NSA_WIDTH = N_HEADS * HEAD_DIM
KV_WIDTH = N_KV_HEADS * HEAD_DIM
S5_WIDTH = D_MODEL - NSA_WIDTH
S5_GROUP = 16
S5_GROUPS = S5_WIDTH // S5_GROUP
S5_STATE = 64
S5_COLS = S5_GROUPS * S5_STATE
CMP_LEN = 32
CMP_STRIDE = 16
CMP_HIDDEN = 256
SEL_BLOCK = 64
SEL_TOPK = 16
WINDOW = 512
ROPE_THETA = 10000.0
RMS_EPS = 1e-6
NEG = -1e30
BIG = 1e9

LANES = 128
V_ROWS = HEAD_DIM + 16
GATE_ROWS = 16
MAX_SEL_BLOCKS = LANES
VMEM_LIMIT = 52 * 1024 * 1024

TM_DENSE = 512
FF_CHUNK = 256
TQ = 256
WIN_SUB = 128
CMP_STEP = 128
TQ_SEL = 512
TK_SEL = 512
S5_TILE = 256
S5_CHUNK = 64
S5_SPLIT = 2

F32 = jnp.float32
BF16 = jnp.bfloat16


def _dot(a, b):
    return jnp.dot(a, b, preferred_element_type=F32)


def _rmsnorm(x, g):
    ms = jnp.mean(x * x, axis=-1, keepdims=True)
    return x * lax.rsqrt(ms + RMS_EPS) * g


def _gelu_tanh(x):
    return 0.5 * x * (1.0 + jnp.tanh(np.sqrt(2.0 / np.pi).astype(np.float32) * (x + 0.044715 * (x * x * x))))


def _params(n_grid):
    return pltpu.CompilerParams(dimension_semantics=("arbitrary",) * n_grid, vmem_limit_bytes=VMEM_LIMIT)


def _resident():
    return pl.BlockSpec(memory_space=pltpu.VMEM)


def _ffn_body(*refs, with_mix):
    if with_mix:
        h_ref, on_ref, os_ref, won_ref, wos_ref, g_ref, w1_ref, w3_ref, w2_ref, o_ref = refs
        h = h_ref[...] + _dot(on_ref[...], won_ref[...]) + _dot(os_ref[...], wos_ref[...])
    else:
        h_ref, g_ref, w1_ref, w3_ref, w2_ref, o_ref = refs
        h = h_ref[...]
    xn = _rmsnorm(h, g_ref[...]).astype(BF16)
    acc = jnp.zeros(h.shape, F32)
    for j in range(D_FF // FF_CHUNK):
        sl = slice(j * FF_CHUNK, (j + 1) * FF_CHUNK)
        a = _dot(xn, w1_ref[:, sl])
        b = _dot(xn, w3_ref[:, sl])
        act = (a * jax.nn.sigmoid(a) * b).astype(BF16)
        acc = acc + _dot(act, w2_ref[sl, :])
    o_ref[...] = h + 0.5 * acc


def _ffn(h, g, w1, w3, w2, mix=None):
    t = h.shape[0]
    tm = min(TM_DENSE, t)
    row = lambda w: pl.BlockSpec((tm, w), lambda i: (i, 0))
    args, specs = [h], [row(D_MODEL)]
    if mix is not None:
        o_nsa, o_s5, wo_n, wo_s = mix
        args += [o_nsa, o_s5, wo_n, wo_s]
        specs += [row(NSA_WIDTH), row(S5_WIDTH), _resident(), _resident()]
    args += [g, w1, w3, w2]
    specs += [_resident()] * 4
    return pl.pallas_call(
        functools.partial(_ffn_body, with_mix=mix is not None),
        grid=(t // tm,),
        in_specs=specs,
        out_specs=row(D_MODEL),
        out_shape=jax.ShapeDtypeStruct((t, D_MODEL), F32),
        compiler_params=_params(1),
        name="ffn_mix" if mix is not None else "ffn",
    )(*args)


_C_Q = 0
_C_CMP = NSA_WIDTH
_C_SLC = _C_CMP + 2 * KV_WIDTH
_C_WIN = _C_SLC + 2 * KV_WIDTH
_C_U = _C_WIN + 2 * KV_WIDTH
_C_G = _C_U + S5_WIDTH
_C_END = _C_G + N_KV_HEADS * LANES


def _rope_pair(x, cos, sin_signed, first_half):
    fwd = pltpu.roll(x, HEAD_DIM // 2, 1)
    bwd = pltpu.roll(x, LANES - HEAD_DIM // 2, 1)
    return x * cos + jnp.where(first_half, bwd, fwd) * sin_signed


def _proj_body(h_ref, g_ref, w_ref, cos_ref, sin_ref,
               qt_ref, kc_ref, vc_ref, ks_ref, vst_ref, kw_ref, vwt_ref, gtt_ref, u_ref, *, tiles_per_seq):
    tm = h_ref.shape[0]
    xn = _rmsnorm(h_ref[...], g_ref[...]).astype(BF16)
    cos = cos_ref[...]
    sin = sin_ref[...]
    lane = lax.broadcasted_iota(jnp.int32, (tm, LANES), 1)
    first_half = (lane % HEAD_DIM) < (HEAD_DIM // 2)
    rope = lambda x: _rope_pair(x, cos, sin, first_half)

    scale = HEAD_DIM ** -0.5
    for j in range(NSA_WIDTH // 256):
        z = _dot(xn, w_ref[:, _C_Q + 256 * j:_C_Q + 256 * (j + 1)])
        for k in range(2):
            rt = (rope(z[:, LANES * k:LANES * (k + 1)]) * scale).T.astype(BF16)
            for e in range(2):
                qt_ref[4 * j + 2 * k + e] = rt[e * HEAD_DIM:(e + 1) * HEAD_DIM, :]

    z = _dot(xn, w_ref[:, _C_CMP:_C_CMP + 256])
    kc_ref[...] = rope(z[:, :LANES])
    vc_ref[...] = z[:, LANES:]

    zeros = jnp.zeros((tm, HEAD_DIM), BF16)
    pos = (pl.program_id(0) % tiles_per_seq) * tm + lax.broadcasted_iota(jnp.int32, (tm, LANES), 0)
    onehot = jnp.where(lane == pos // SEL_BLOCK, 1.0, 0.0).astype(BF16)

    z = _dot(xn, w_ref[:, _C_SLC:_C_SLC + 256])
    k = rope(z[:, :LANES]).astype(BF16)
    vt = z[:, LANES:].T.astype(BF16)
    for g in range(N_KV_HEADS):
        gs = slice(g * HEAD_DIM, (g + 1) * HEAD_DIM)
        ks_ref[g, :, 0:LANES] = onehot
        ks_ref[g, :, LANES:LANES + HEAD_DIM] = k[:, gs]
        ks_ref[g, :, LANES + HEAD_DIM:] = zeros
        vst_ref[g, 0:HEAD_DIM, :] = vt[gs, :]
        vst_ref[g, HEAD_DIM:, :] = jnp.ones((V_ROWS - HEAD_DIM, tm), BF16)

    z = _dot(xn, w_ref[:, _C_WIN:_C_WIN + 256])
    k = rope(z[:, :LANES]).astype(BF16)
    vt = z[:, LANES:].T.astype(BF16)
    for g in range(N_KV_HEADS):
        gs = slice(g * HEAD_DIM, (g + 1) * HEAD_DIM)
        kw_ref[g] = k[:, gs]
        for c in range(tm // WIN_SUB):
            vwt_ref[g, c, 0:HEAD_DIM, :] = vt[gs, c * WIN_SUB:(c + 1) * WIN_SUB]
            vwt_ref[g, c, HEAD_DIM:, :] = jnp.ones((V_ROWS - HEAD_DIM, WIN_SUB), BF16)

    for j in range(S5_WIDTH // 256):
        u_ref[:, 256 * j:256 * (j + 1)] = _dot(xn, w_ref[:, _C_U + 256 * j:_C_U + 256 * (j + 1)])
    gates_t = jax.nn.sigmoid(_dot(xn, w_ref[:, _C_G:_C_END])).T
    for g in range(N_KV_HEADS):
        gtt_ref[g] = gates_t[g * LANES:g * LANES + GATE_ROWS, :]


def _proj(h1, g, w_in_r, cos, sin_signed, batch, seq):
    t = batch * seq
    tm = TK_SEL
    n = seq // tm
    bl = lambda w: pl.BlockSpec((None, tm, w), lambda i: (i // n, i % n, 0))
    bgl = lambda w: pl.BlockSpec((None, N_KV_HEADS, tm, w), lambda i: (i // n, 0, i % n, 0))
    tab = pl.BlockSpec((tm, LANES), lambda i: (i % n, 0))
    sds = jax.ShapeDtypeStruct
    return pl.pallas_call(
        functools.partial(_proj_body, tiles_per_seq=n),
        grid=(t // tm,),
        in_specs=[pl.BlockSpec((tm, D_MODEL), lambda i: (i, 0)), _resident(), _resident(), tab, tab],
        out_specs=[
            pl.BlockSpec((None, N_HEADS, HEAD_DIM, tm), lambda i: (i // n, 0, 0, i % n)),
            bl(KV_WIDTH), bl(KV_WIDTH),
            bgl(2 * LANES),
            pl.BlockSpec((None, N_KV_HEADS, None, V_ROWS, tm), lambda i: (i // n, 0, i % n, 0, 0)),
            bgl(HEAD_DIM),
            pl.BlockSpec((None, N_KV_HEADS, tm // WIN_SUB, V_ROWS, WIN_SUB), lambda i: (i // n, 0, i % n, 0, 0)),
            pl.BlockSpec((None, N_KV_HEADS, GATE_ROWS, tm), lambda i: (i // n, 0, 0, i % n)),
            bl(S5_WIDTH),
        ],
        out_shape=[
            sds((batch, N_HEADS, HEAD_DIM, seq), BF16),
            sds((batch, seq, KV_WIDTH), F32), sds((batch, seq, KV_WIDTH), F32),
            sds((batch, N_KV_HEADS, seq, 2 * LANES), BF16), sds((batch, N_KV_HEADS, n, V_ROWS, tm), BF16),
            sds((batch, N_KV_HEADS, seq, HEAD_DIM), BF16), sds((batch, N_KV_HEADS, seq // WIN_SUB, V_ROWS, WIN_SUB), BF16),
            sds((batch, N_KV_HEADS, GATE_ROWS, seq), F32), sds((batch, seq, S5_WIDTH), F32),
        ],
        compiler_params=_params(1),
        name="proj",
    )(h1.reshape(t, D_MODEL), g, w_in_r, cos, sin_signed)


def _compress_body(kc_ref, vc_ref, pek_ref, pev_ref, wkt_ref, wkb_ref, wvt_ref, wvb_ref, wk2_ref, wv2_ref,
                   ko_ref, vot_ref, v_ref):
    nc = kc_ref.shape[0]

    def hidden(x, pe_ref, wt_ref, wb_ref):
        top = _dot((x + pe_ref[0:1, :]).astype(BF16), wt_ref[...])
        bot = _dot((x + pe_ref[1:2, :]).astype(BF16), wb_ref[...])
        pre = top + pltpu.roll(bot, nc - 1, 0)
        return _gelu_tanh(pre).astype(BF16)

    hk = hidden(kc_ref[...], pek_ref, wkt_ref, wkb_ref)
    hv = hidden(vc_ref[...], pev_ref, wvt_ref, wvb_ref)
    for g in range(N_KV_HEADS):
        gs = slice(g * CMP_HIDDEN, (g + 1) * CMP_HIDDEN)
        ko_ref[g] = _dot(hk[:, gs], wk2_ref[...]).astype(BF16)
        v_ref[...] = _dot(hv[:, gs], wv2_ref[...])
        vot_ref[g] = v_ref[...].T.astype(BF16)


def _compress(kc, vc, pe_k2, pe_v2, wkt, wkb, wvt, wvb, wk2, wv2):
    batch, seq, _ = kc.shape
    nc = seq // CMP_STRIDE
    width = CMP_STRIDE * KV_WIDTH
    x_spec = pl.BlockSpec((None, nc, width), lambda b: (b, 0, 0))
    sds = jax.ShapeDtypeStruct
    return pl.pallas_call(
        _compress_body,
        grid=(batch,),
        in_specs=[x_spec, x_spec] + [_resident()] * 8,
        out_specs=[pl.BlockSpec((None, N_KV_HEADS, nc, HEAD_DIM), lambda b: (b, 0, 0, 0)),
                   pl.BlockSpec((None, N_KV_HEADS, HEAD_DIM, nc), lambda b: (b, 0, 0, 0))],
        out_shape=[sds((batch, N_KV_HEADS, nc, HEAD_DIM), BF16), sds((batch, N_KV_HEADS, HEAD_DIM, nc), BF16)],
        scratch_shapes=[pltpu.VMEM((nc, HEAD_DIM), F32)],
        compiler_params=_params(1),
        name="compress",
    )(kc.reshape(batch, nc, width), vc.reshape(batch, nc, width), pe_k2, pe_v2, wkt, wkb, wvt, wvb, wk2, wv2)


def _cmpwin_body(qt_ref, kc_ref, vct_ref, ovt_ref, kw_ref, vwt_ref, gtt_ref, opt_ref, bias_ref,
                 s_ref, ocmp_ref, imp_ref):
    tq = qt_ref.shape[2]
    cols = HEADS_PER_GROUP * tq
    nc = kc_ref.shape[0]
    i = pl.program_id(2)
    q_t = jnp.concatenate([qt_ref[h] for h in range(HEADS_PER_GROUP)], axis=1)
    t_lane = i * tq + lax.broadcasted_iota(jnp.int32, (1, cols), 1) % tq

    def compressed(n):
        s = _dot(kc_ref[0:n, :], q_t)
        c_end = lax.broadcasted_iota(jnp.int32, (n, cols), 0) * CMP_STRIDE + (CMP_LEN - 1)
        s = jnp.where(c_end <= t_lane, s, NEG)
        e = jnp.exp(s - jnp.max(s, axis=0, keepdims=True))
        den = jnp.maximum(jnp.sum(e, axis=0, keepdims=True), 1e-30)
        p = e * jnp.where(t_lane >= CMP_LEN - 1, 1.0 / den, 0.0)
        ocmp_ref[...] = _dot(vct_ref[:, 0:n], p.astype(BF16))
        psum = p[:, 0:tq]
        for h in range(1, HEADS_PER_GROUP):
            psum = psum + p[:, h * tq:(h + 1) * tq]
        p_hi = psum.astype(BF16)
        p_lo = (psum - p_hi.astype(F32)).astype(BF16)
        imp_ref[...] = _dot(ovt_ref[:, 0:n], p_hi) + _dot(ovt_ref[:, 0:n], p_lo)

    finished = (i * tq + tq - CMP_LEN) // CMP_STRIDE + 1
    n_steps = nc // CMP_STEP
    for k in range(1, n_steps + 1):
        in_step = finished > (k - 1) * CMP_STEP
        if k < n_steps:
            in_step = jnp.logical_and(in_step, finished <= k * CMP_STEP)
        pl.when(in_step)(functools.partial(compressed, k * CMP_STEP))
    o_cmp = ocmp_ref[...]
    imp = imp_ref[...]

    blk = lax.broadcasted_iota(jnp.int32, (MAX_SEL_BLOCKS, tq), 0)
    behind = t_lane[:, 0:tq] // SEL_BLOCK - blk
    score = jnp.where(blk == 0, BIG, jnp.where(behind == 0, BIG, jnp.where(behind == 1, BIG, imp)))
    score = jnp.where(behind >= 0, score, -BIG)
    chosen = jnp.zeros((MAX_SEL_BLOCKS, tq), F32)
    for _ in range(SEL_TOPK):
        best = jnp.max(score, axis=0, keepdims=True)
        first = jnp.min(jnp.where(score == best, blk, MAX_SEL_BLOCKS), axis=0, keepdims=True)
        hit = blk == first
        chosen = jnp.where(hit, 1.0, chosen)
        score = jnp.where(hit, -jnp.inf, score)
    bias_ref[...] = jnp.where(chosen > 0.0, jnp.where(behind >= 0, 0.0, NEG), NEG).astype(BF16)

    span = WINDOW + tq
    start = pl.multiple_of(jnp.maximum(i * tq - WINDOW, 0), tq)
    s_ref[...] = _dot(kw_ref[pl.ds(start, span), :], q_t)

    def band(rows):
        kpos = start + rows.start + lax.broadcasted_iota(jnp.int32, (rows.stop - rows.start, cols), 0)
        dist = lax.bitcast_convert_type(t_lane - kpos, jnp.uint32)
        s_ref[rows, :] = jnp.where(dist < WINDOW, s_ref[rows, :], NEG)

    @pl.when(i * tq >= WINDOW)
    def _():
        band(slice(0, tq))
        band(slice(span - tq, span))

    @pl.when(i * tq < WINDOW)
    def _():
        band(slice(0, span))

    s = s_ref[...]
    e = jnp.exp(s - jnp.max(s, axis=0, keepdims=True)).astype(BF16)
    c0 = start // WIN_SUB
    vw_t = jnp.concatenate([vwt_ref[c0 + j] for j in range(span // WIN_SUB)], axis=1)
    acc = _dot(vw_t, e)
    o_win = acc[0:HEAD_DIM, :] / acc[HEAD_DIM:HEAD_DIM + 1, :]

    gates = gtt_ref[...]
    for h in range(HEADS_PER_GROUP):
        hs = slice(h * tq, (h + 1) * tq)
        opt_ref[h * HEAD_DIM:(h + 1) * HEAD_DIM, :] = (gates[3 * h:3 * h + 1, :] * o_cmp[:, hs]
                                                       + gates[3 * h + 2:3 * h + 3, :] * o_win[:, hs])


def _cmpwin(qt, kcmp, vcmp_t, overlap_t, kwin, vwin_t, gates_t):
    batch, _, _, seq = qt.shape
    nc = kcmp.shape[2]
    tq = TQ
    per_bg = lambda *shape: pl.BlockSpec((None, None) + shape, lambda b, g, i: (b, g) + (0,) * len(shape))
    sds = jax.ShapeDtypeStruct
    return pl.pallas_call(
        _cmpwin_body,
        grid=(batch, N_KV_HEADS, seq // tq),
        in_specs=[
            pl.BlockSpec((None, HEADS_PER_GROUP, HEAD_DIM, tq), lambda b, g, i: (b, g, 0, i)),
            per_bg(nc, HEAD_DIM), per_bg(HEAD_DIM, nc), _resident(),
            per_bg(seq, HEAD_DIM), per_bg(seq // WIN_SUB, V_ROWS, WIN_SUB),
            pl.BlockSpec((None, None, GATE_ROWS, tq), lambda b, g, i: (b, g, 0, i)),
        ],
        out_specs=[pl.BlockSpec((None, None, HEADS_PER_GROUP * HEAD_DIM, tq), lambda b, g, i: (b, g, 0, i)),
                   pl.BlockSpec((None, None, MAX_SEL_BLOCKS, tq), lambda b, g, i: (b, g, 0, i))],
        out_shape=[sds((batch, N_KV_HEADS, HEADS_PER_GROUP * HEAD_DIM, seq), F32),
                   sds((batch, N_KV_HEADS, MAX_SEL_BLOCKS, seq), BF16)],
        scratch_shapes=[pltpu.VMEM((WINDOW + tq, HEADS_PER_GROUP * tq), F32),
                        pltpu.VMEM((HEAD_DIM, HEADS_PER_GROUP * tq), F32), pltpu.VMEM((MAX_SEL_BLOCKS, tq), F32)],
        compiler_params=_params(3),
        name="cmpwin",
    )(qt, kcmp, vcmp_t, overlap_t, kwin, vwin_t, gates_t)


def _select_body(qt_ref, bias_ref, ks_ref, vst_ref, gtt_ref, opt_ref, o_ref, qa_ref, m_ref, acc_ref, *head_refs):
    tq = qt_ref.shape[2]
    cols = HEADS_PER_GROUP * tq
    tk = TK_SEL
    i = pl.program_id(2)

    bias = bias_ref[...]
    for h in range(HEADS_PER_GROUP):
        qa_ref[0:LANES, h * tq:(h + 1) * tq] = bias
        qa_ref[LANES:LANES + HEAD_DIM, h * tq:(h + 1) * tq] = qt_ref[h]
    qa_ref[LANES + HEAD_DIM:, :] = jnp.zeros((LANES - HEAD_DIM, cols), BF16)
    m_ref[...] = jnp.full(m_ref.shape, NEG, F32)
    acc_ref[...] = jnp.zeros(acc_ref.shape, F32)

    s_refs, mx_refs = head_refs[:HEADS_PER_GROUP], head_refs[HEADS_PER_GROUP:]

    def scores(c, causal, h):
        hs = slice(h * tq, (h + 1) * tq)
        off = pl.multiple_of(c * tk, tk)
        s = _dot(ks_ref[pl.ds(off, tk), :], qa_ref[:, hs])
        if causal:
            kpos = off + lax.broadcasted_iota(jnp.int32, s.shape, 0)
            t = i * tq + lax.broadcasted_iota(jnp.int32, s.shape, 1) % tq
            s = jnp.where(kpos <= t, s, NEG)
        s_refs[h][c % 2] = s
        mx_refs[h][c % 2] = jnp.max(s, axis=0, keepdims=True)

    def accumulate(c, h):
        hs = slice(h * tq, (h + 1) * tq)
        m_old = m_ref[:, hs]
        m_new = jnp.maximum(m_old, mx_refs[h][c % 2])
        p = jnp.exp(s_refs[h][c % 2] - m_new).astype(BF16)
        acc_ref[:, hs] = jnp.exp(m_old - m_new) * acc_ref[:, hs] + _dot(vst_ref[c], p)
        m_ref[:, hs] = m_new

    heads = range(HEADS_PER_GROUP)

    n_full = (i * tq) // tk

    def pair(c, causal):
        scores(c + 1, causal, HEADS_PER_GROUP - 1)
        for h in heads:
            accumulate(c, h)
            if h > 0:
                scores(c + 1, causal, h - 1)

    @pl.when(n_full > 0)
    def _():
        for h in heads:
            scores(0, False, h)

    def two_pairs(k, carry):
        pair(2 * k, False)
        pair(2 * k + 1, False)
        return carry

    n_plain = jnp.maximum(n_full - 1, 0)
    lax.fori_loop(0, n_plain // 2, two_pairs, 0)

    @pl.when(n_plain % 2 == 1)
    def _():
        pair(n_plain - 1, False)

    @pl.when(n_full > 0)
    def _():
        pair(n_full - 1, True)

    @pl.when(n_full == 0)
    def _():
        for h in heads:
            scores(n_full, True, h)

    for h in heads:
        accumulate(n_full, h)

    acc = acc_ref[...]
    o_slc = acc[0:HEAD_DIM, :] / acc[HEAD_DIM:HEAD_DIM + 1, :]
    gates = gtt_ref[...]
    o_t = jnp.concatenate([gates[3 * h + 1:3 * h + 2, :] * o_slc[:, h * tq:(h + 1) * tq]
                           for h in range(HEADS_PER_GROUP)], axis=0)
    o_ref[...] = (opt_ref[...] + o_t).T.astype(BF16)


def _select(qt, bias_t, ksel, vsel_t, gates_t, o_part_t):
    batch, _, _, seq = qt.shape
    tq = TQ_SEL
    cols = HEADS_PER_GROUP * tq
    n_chunks = seq // TK_SEL
    tile_t = lambda r: pl.BlockSpec((None, None, r, tq), lambda b, g, i: (b, g, 0, i))
    return pl.pallas_call(
        _select_body,
        grid=(batch, N_KV_HEADS, seq // tq),
        in_specs=[
            pl.BlockSpec((None, HEADS_PER_GROUP, HEAD_DIM, tq), lambda b, g, i: (b, g, 0, i)),
            tile_t(MAX_SEL_BLOCKS),
            pl.BlockSpec((None, None, seq, 2 * LANES), lambda b, g, i: (b, g, 0, 0)),
            pl.BlockSpec((None, None, n_chunks, V_ROWS, TK_SEL), lambda b, g, i: (b, g, 0, 0, 0)),
            tile_t(GATE_ROWS), tile_t(HEADS_PER_GROUP * HEAD_DIM),
        ],
        out_specs=pl.BlockSpec((None, tq, HEADS_PER_GROUP * HEAD_DIM), lambda b, g, i: (b, i, g)),
        out_shape=jax.ShapeDtypeStruct((batch, seq, NSA_WIDTH), BF16),
        scratch_shapes=([pltpu.VMEM((2 * LANES, cols), BF16), pltpu.VMEM((1, cols), F32), pltpu.VMEM((V_ROWS, cols), F32)]
                        + [pltpu.VMEM((2, TK_SEL, tq), F32)] * HEADS_PER_GROUP
                        + [pltpu.VMEM((2, 1, tq), F32)] * HEADS_PER_GROUP),
        compiler_params=_params(3),
        name="select",
    )(qt, bias_t, ksel, vsel_t, gates_t, o_part_t)


def _s5_body(u_ref, bt_ref, pw_ref, a1_ref, ltri_ref, ct_ref, d_ref, wg_ref, bg_ref, o_ref, c_ref, x_ref, h_ref):
    @pl.when(pl.program_id(1) == 0)
    def _():
        c_ref[...] = jnp.zeros(c_ref.shape, F32)

    n = S5_COLS // S5_SPLIT
    u = u_ref[...]
    last = S5_CHUNK - 1
    ys = []
    for hf in range(S5_SPLIT):
        cs = slice(hf * n, (hf + 1) * n)
        p_re, p_im, q_re, q_im = pw_ref[0, :, cs], pw_ref[1, :, cs], pw_ref[2, :, cs], pw_ref[3, :, cs]
        a_re, a_im = a1_ref[0:1, cs], a1_ref[1:2, cs]
        width = S5_WIDTH // S5_SPLIT
        bu = _dot(u[:, hf * width:(hf + 1) * width].astype(BF16), bt_ref[hf])
        for j in range(u.shape[0] // S5_CHUNK):
            r = slice(j * S5_CHUNK, (j + 1) * S5_CHUNK)
            b_re, b_im = bu[r, 0:n], bu[r, n:2 * n]
            x_ref[r, 0:n] = (q_re * b_re - q_im * b_im).astype(BF16)
            x_ref[r, n:2 * n] = (q_re * b_im + q_im * b_re).astype(BF16)
        local = _dot(ltri_ref[...], x_ref[...])
        c_re, c_im = c_ref[0:1, cs], c_ref[1:2, cs]
        for j in range(u.shape[0] // S5_CHUNK):
            r = slice(j * S5_CHUNK, (j + 1) * S5_CHUNK)
            s_re, s_im = local[r, 0:n] + c_re, local[r, n:2 * n] + c_im
            h_ref[r, 0:n] = (p_re * s_re - p_im * s_im).astype(BF16)
            h_ref[r, n:2 * n] = (p_re * s_im + p_im * s_re).astype(BF16)
            e_re, e_im = s_re[last:last + 1], s_im[last:last + 1]
            l_re = p_re[last:last + 1] * e_re - p_im[last:last + 1] * e_im
            l_im = p_re[last:last + 1] * e_im + p_im[last:last + 1] * e_re
            c_re = a_re * l_re - a_im * l_im
            c_im = a_re * l_im + a_im * l_re
        c_ref[0:1, cs] = c_re
        c_ref[1:2, cs] = c_im
        ys.append(_dot(h_ref[...], ct_ref[hf]))
    y = _gelu_tanh(jnp.concatenate(ys, axis=1) + d_ref[...] * u)
    o_ref[...] = (y * jax.nn.sigmoid(_dot(y.astype(BF16), wg_ref[...]) + bg_ref[...])).astype(BF16)


def _s5(u, bt, powers, a1, ltri, ct, d, w_glu, b_glu):
    batch, seq, _ = u.shape
    ts = S5_TILE
    tile = pl.BlockSpec((None, ts, S5_WIDTH), lambda b, j: (b, j, 0))
    cols = 2 * S5_COLS // S5_SPLIT
    return pl.pallas_call(
        _s5_body,
        grid=(batch, seq // ts),
        in_specs=[tile] + [_resident()] * 8,
        out_specs=tile,
        out_shape=jax.ShapeDtypeStruct((batch, seq, S5_WIDTH), BF16),
        scratch_shapes=[pltpu.VMEM((2, S5_COLS), F32), pltpu.VMEM((ts, cols), BF16), pltpu.VMEM((ts, cols), BF16)],
        compiler_params=_params(2),
        name="s5",
    )(u, bt, powers, a1, ltri, ct, d, w_glu, b_glu)


def _ple_body(*refs, final_norm):
    if final_norm:
        h_ref, gp_ref, wg_ref, p_ref, wp_ref, gf_ref, o_ref = refs
    else:
        h_ref, gp_ref, wg_ref, p_ref, wp_ref, o_ref = refs
    h = h_ref[...]
    gate = jax.nn.sigmoid(_dot(_rmsnorm(h, gp_ref[...]).astype(BF16), wg_ref[...]))
    h = h + gate * _dot(p_ref[...].astype(BF16), wp_ref[...])
    o_ref[...] = _rmsnorm(h, gf_ref[...]) if final_norm else h


def _ple(h, g_ple, w_gate, p, w_ple, g_final=None):
    t = h.shape[0]
    tm = min(TM_DENSE, t)
    row = lambda w: pl.BlockSpec((tm, w), lambda i: (i, 0))
    args = [h, g_ple, w_gate, p, w_ple]
    specs = [row(D_MODEL), _resident(), _resident(), row(PLE_DIM), _resident()]
    if g_final is not None:
        args.append(g_final)
        specs.append(_resident())
    return pl.pallas_call(
        functools.partial(_ple_body, final_norm=g_final is not None),
        grid=(t // tm,),
        in_specs=specs,
        out_specs=row(D_MODEL),
        out_shape=jax.ShapeDtypeStruct((t, D_MODEL), F32),
        compiler_params=_params(1),
        name="ple",
    )(*args)


def _rearrange_w_in(w_in):
    g0 = NSA_WIDTH + 6 * KV_WIDTH
    per_group = 3 * HEADS_PER_GROUP
    pad = jnp.zeros((D_MODEL, LANES - per_group), w_in.dtype)
    gate_cols = []
    for g in range(N_KV_HEADS):
        gate_cols += [w_in[:, g0 + g * per_group:g0 + (g + 1) * per_group], pad]
    u0 = g0 + 3 * N_HEADS
    return jnp.concatenate([w_in[:, :g0], w_in[:, u0:]] + gate_cols, axis=1).astype(BF16)


def _rope_tables(seq):
    half = HEAD_DIM // 2
    inv = ROPE_THETA ** (-jnp.arange(half, dtype=F32) / half)
    ang = jnp.arange(seq, dtype=F32)[:, None] * inv[None, :]
    cos = jnp.tile(jnp.cos(ang), (1, LANES // half))
    sin = jnp.tile(jnp.concatenate([-jnp.sin(ang), jnp.sin(ang)], axis=1), (1, LANES // HEAD_DIM))
    return cos, sin


def _compress_weights(pe, w1, w2):
    w1 = w1.reshape(2, CMP_STRIDE, HEAD_DIM, CMP_HIDDEN)
    eye = jnp.eye(N_KV_HEADS, dtype=w1.dtype)
    big = jnp.einsum("sjdh,ge->sjgdeh", w1, eye).reshape(2, CMP_STRIDE * KV_WIDTH, N_KV_HEADS * CMP_HIDDEN)
    pe2 = jnp.broadcast_to(pe.reshape(2, CMP_STRIDE, 1, HEAD_DIM), (2, CMP_STRIDE, N_KV_HEADS, HEAD_DIM))
    return pe2.reshape(2, CMP_STRIDE * KV_WIDTH), big[0].astype(BF16), big[1].astype(BF16), w2.astype(BF16)


def _overlap_matrix(n_cmp_rows):
    c_start = np.arange(n_cmp_rows)[:, None] * CMP_STRIDE
    s_start = np.arange(MAX_SEL_BLOCKS)[None, :] * SEL_BLOCK
    ov = (c_start < s_start + SEL_BLOCK) & (c_start + CMP_LEN > s_start)
    return jnp.asarray(ov.T, dtype=BF16)


def _s5_tables(a_re, a_im, log_dt, b_re, b_im, c_re, c_im):
    dt = jnp.exp(log_dt)[:, None]
    lr, li = a_re * dt, a_im * dt
    mag = jnp.exp(lr)
    ab_re, ab_im = mag * jnp.cos(li), mag * jnp.sin(li)
    den = a_re * a_re + a_im * a_im
    nr, ni = ab_re - 1.0, ab_im
    bc_re = (nr * a_re + ni * a_im) / den
    bc_im = (ni * a_re - nr * a_im) / den
    bt_re = b_re * bc_re[..., None] - b_im * bc_im[..., None]
    bt_im = b_re * bc_im[..., None] + b_im * bc_re[..., None]
    gh = S5_GROUPS // S5_SPLIT
    eye = jnp.eye(gh, dtype=F32)
    split = lambda w: w.reshape((S5_SPLIT, gh) + w.shape[1:])
    blockdiag_in = lambda w: jnp.einsum("hgni,ge->hgien", split(w), eye).reshape(S5_SPLIT, gh * S5_GROUP, gh * S5_STATE)
    bt = jnp.concatenate([blockdiag_in(bt_re), blockdiag_in(bt_im)], axis=2).astype(BF16)
    blockdiag_out = lambda w: jnp.einsum("hgon,ge->hgneo", split(w), eye).reshape(S5_SPLIT, gh * S5_STATE, gh * S5_GROUP)
    ct = jnp.concatenate([blockdiag_out(c_re), -blockdiag_out(c_im)], axis=1).astype(BF16)
    k = jnp.arange(S5_CHUNK, dtype=F32)[:, None]
    lr, li = lr.reshape(1, S5_COLS), li.reshape(1, S5_COLS)
    powers = jnp.stack([jnp.exp(k * lr) * jnp.cos(k * li), jnp.exp(k * lr) * jnp.sin(k * li),
                        jnp.exp(-k * lr) * jnp.cos(k * li), -jnp.exp(-k * lr) * jnp.sin(k * li)])
    a1 = jnp.concatenate([ab_re.reshape(1, S5_COLS), ab_im.reshape(1, S5_COLS)], axis=0)
    ltri = jnp.asarray(np.kron(np.eye(S5_TILE // S5_CHUNK), np.tril(np.ones((S5_CHUNK, S5_CHUNK)))), dtype=BF16)
    return bt, powers, a1, ltri, ct


def kernel(x, p, norm_ffn1, ffn1_w1, ffn1_w3, ffn1_w2, norm_mix, w_in, cmp_pe_k, cmp_pe_v, cmp_wk1, cmp_wk2,
           cmp_wv1, cmp_wv2, s5_a_re, s5_a_im, s5_log_dt, s5_b_re, s5_b_im, s5_c_re, s5_c_im, s5_d, s5_w_glu,
           s5_b_glu, w_out, norm_ffn2, ffn2_w1, ffn2_w3, ffn2_w2, norm_ple, w_ple_gate, w_ple, norm_final):
    batch, seq, _ = x.shape
    depth = p.shape[0]
    t = batch * seq
    assert seq % TK_SEL == 0 and TK_SEL % TQ_SEL == 0 and seq % TQ_SEL == 0 and seq % S5_TILE == 0 and seq >= WINDOW + TQ and seq % (CMP_STRIDE * CMP_STEP) == 0
    assert seq // SEL_BLOCK <= MAX_SEL_BLOCKS and seq // SEL_BLOCK >= SEL_TOPK
    bf = lambda w: w.astype(BF16)
    row = lambda v: v.reshape(1, -1)

    cos, sin_signed = _rope_tables(seq)
    overlap_t = _overlap_matrix(seq // CMP_STRIDE)
    h = x.reshape(t, D_MODEL)
    for i in range(depth):
        h = _ffn(h, row(norm_ffn1[i]), bf(ffn1_w1[i]), bf(ffn1_w3[i]), bf(ffn1_w2[i]))

        qt, kc, vc, ksel, vsel_t, kwin, vwin_t, gates_t, u = _proj(
            h, row(norm_mix[i]), _rearrange_w_in(w_in[i]), cos, sin_signed, batch, seq)
        pe_k2, wkt, wkb, wk2 = _compress_weights(cmp_pe_k[i], cmp_wk1[i], cmp_wk2[i])
        pe_v2, wvt, wvb, wv2 = _compress_weights(cmp_pe_v[i], cmp_wv1[i], cmp_wv2[i])
        kcmp, vcmp_t = _compress(kc, vc, pe_k2, pe_v2, wkt, wkb, wvt, wvb, wk2, wv2)
        o_part_t, bias_t = _cmpwin(qt, kcmp, vcmp_t, overlap_t, kwin, vwin_t, gates_t)
        o_nsa = _select(qt, bias_t, ksel, vsel_t, gates_t, o_part_t)

        bt, powers, a1, ltri, ct = _s5_tables(s5_a_re[i], s5_a_im[i], s5_log_dt[i], s5_b_re[i], s5_b_im[i],
                                              s5_c_re[i], s5_c_im[i])
        o_s5 = _s5(u, bt, powers, a1, ltri, ct, row(s5_d[i]), bf(s5_w_glu[i]), row(s5_b_glu[i]))

        h = _ffn(h, row(norm_ffn2[i]), bf(ffn2_w1[i]), bf(ffn2_w3[i]), bf(ffn2_w2[i]),
                 mix=(o_nsa.reshape(t, NSA_WIDTH), o_s5.reshape(t, S5_WIDTH),
                      bf(w_out[i][:NSA_WIDTH]), bf(w_out[i][NSA_WIDTH:])))
        h = _ple(h, row(norm_ple[i]), bf(w_ple_gate[i]), p[i].reshape(t, PLE_DIM), bf(w_ple[i]),
                 row(norm_final) if i == depth - 1 else None)
    return h.reshape(batch, seq, D_MODEL)
```

```python
import functools

import jax
import jax.numpy as jnp
import numpy as np
from jax import lax
from jax.experimental import pallas as pl
from jax.experimental.pallas import tpu as pltpu

D_MODEL = 1024
PLE_DIM = 256
D_FF = 2816
N_HEADS = 8
N_KV_HEADS = 2
HEAD_DIM = 64
HEADS_PER_GROUP = N_HEADS // N_KV_HEADS
NSA_WIDTH = N_HEADS * HEAD_DIM
KV_WIDTH = N_KV_HEADS * HEAD_DIM
S5_WIDTH = D_MODEL - NSA_WIDTH
S5_GROUP = 16
S5_GROUPS = S5_WIDTH // S5_GROUP
S5_STATE = 64
S5_COLS = S5_GROUPS * S5_STATE
CMP_LEN = 32
CMP_STRIDE = 16
CMP_HIDDEN = 256
SEL_BLOCK = 64
SEL_TOPK = 16
WINDOW = 512
ROPE_THETA = 10000.0
RMS_EPS = 1e-6
NEG = -1e30
BIG = 1e9

LANES = 128
V_ROWS = HEAD_DIM + 16
GATE_ROWS = 16
MAX_SEL_BLOCKS = LANES
VMEM_LIMIT = 52 * 1024 * 1024

TM_DENSE = 512
FF_CHUNK = 256
TQ = 256
WIN_SUB = 128
CMP_STEP = 128
TQ_SEL = 512
TK_SEL = 512
S5_TILE = 256
S5_CHUNK = 64
S5_SPLIT = 2

F32 = jnp.float32
BF16 = jnp.bfloat16


def _dot(a, b):
    return jnp.dot(a, b, preferred_element_type=F32)


def _rmsnorm(x, g):
    ms = jnp.mean(x * x, axis=-1, keepdims=True)
    return x * lax.rsqrt(ms + RMS_EPS) * g


def _gelu_tanh(x):
    return 0.5 * x * (1.0 + jnp.tanh(np.sqrt(2.0 / np.pi).astype(np.float32) * (x + 0.044715 * (x * x * x))))


def _params(n_grid):
    return pltpu.CompilerParams(dimension_semantics=("arbitrary",) * n_grid, vmem_limit_bytes=VMEM_LIMIT)


def _resident():
    return pl.BlockSpec(memory_space=pltpu.VMEM)


def _swiglu_half_step(h, g_ref, w1_ref, w3_ref, w2_ref):
    xn = _rmsnorm(h, g_ref[...]).astype(BF16)
    acc = jnp.zeros(h.shape, F32)
    for j in range(D_FF // FF_CHUNK):
        sl = slice(j * FF_CHUNK, (j + 1) * FF_CHUNK)
        a = _dot(xn, w1_ref[:, sl])
        b = _dot(xn, w3_ref[:, sl])
        act = (a * jax.nn.sigmoid(a) * b).astype(BF16)
        acc = acc + _dot(act, w2_ref[sl, :])
    return h + 0.5 * acc


_C_Q = 0
_C_CMP = NSA_WIDTH
_C_SLC = _C_CMP + 2 * KV_WIDTH
_C_WIN = _C_SLC + 2 * KV_WIDTH
_C_U = _C_WIN + 2 * KV_WIDTH
_C_G = _C_U + S5_WIDTH
_C_END = _C_G + N_KV_HEADS * LANES


def _rope_pair(x, cos, sin_signed, first_half):
    fwd = pltpu.roll(x, HEAD_DIM // 2, 1)
    bwd = pltpu.roll(x, LANES - HEAD_DIM // 2, 1)
    return x * cos + jnp.where(first_half, bwd, fwd) * sin_signed


def _ffn_proj_body(x_ref, g1_ref, w1_ref, w3_ref, w2_ref, g_ref, w_ref, cos_ref, sin_ref,
                   h_ref, qt_ref, kc_ref, vc_ref, ks_ref, vst_ref, kw_ref, vwt_ref, gtt_ref, u_ref, *, tiles_per_seq):
    tm = x_ref.shape[0]
    h = _swiglu_half_step(x_ref[...], g1_ref, w1_ref, w3_ref, w2_ref)
    h_ref[...] = h
    xn = _rmsnorm(h, g_ref[...]).astype(BF16)
    cos = cos_ref[...]
    sin = sin_ref[...]
    lane = lax.broadcasted_iota(jnp.int32, (tm, LANES), 1)
    first_half = (lane % HEAD_DIM) < (HEAD_DIM // 2)
    rope = lambda x: _rope_pair(x, cos, sin, first_half)

    scale = HEAD_DIM ** -0.5
    for j in range(NSA_WIDTH // 256):
        z = _dot(xn, w_ref[:, _C_Q + 256 * j:_C_Q + 256 * (j + 1)])
        for k in range(2):
            rt = (rope(z[:, LANES * k:LANES * (k + 1)]) * scale).T.astype(BF16)
            for e in range(2):
                qt_ref[4 * j + 2 * k + e] = rt[e * HEAD_DIM:(e + 1) * HEAD_DIM, :]

    z = _dot(xn, w_ref[:, _C_CMP:_C_CMP + 256])
    kc_ref[...] = rope(z[:, :LANES])
    vc_ref[...] = z[:, LANES:]

    zeros = jnp.zeros((tm, HEAD_DIM), BF16)
    pos = (pl.program_id(0) % tiles_per_seq) * tm + lax.broadcasted_iota(jnp.int32, (tm, LANES), 0)
    onehot = jnp.where(lane == pos // SEL_BLOCK, 1.0, 0.0).astype(BF16)

    z = _dot(xn, w_ref[:, _C_SLC:_C_SLC + 256])
    k = rope(z[:, :LANES]).astype(BF16)
    vt = z[:, LANES:].T.astype(BF16)
    for g in range(N_KV_HEADS):
        gs = slice(g * HEAD_DIM, (g + 1) * HEAD_DIM)
        ks_ref[g, :, 0:LANES] = onehot
        ks_ref[g, :, LANES:LANES + HEAD_DIM] = k[:, gs]
        ks_ref[g, :, LANES + HEAD_DIM:] = zeros
        vst_ref[g, 0:HEAD_DIM, :] = vt[gs, :]
        vst_ref[g, HEAD_DIM:, :] = jnp.ones((V_ROWS - HEAD_DIM, tm), BF16)

    z = _dot(xn, w_ref[:, _C_WIN:_C_WIN + 256])
    k = rope(z[:, :LANES]).astype(BF16)
    vt = z[:, LANES:].T.astype(BF16)
    for g in range(N_KV_HEADS):
        gs = slice(g * HEAD_DIM, (g + 1) * HEAD_DIM)
        kw_ref[g] = k[:, gs]
        for c in range(tm // WIN_SUB):
            vwt_ref[g, c, 0:HEAD_DIM, :] = vt[gs, c * WIN_SUB:(c + 1) * WIN_SUB]
            vwt_ref[g, c, HEAD_DIM:, :] = jnp.ones((V_ROWS - HEAD_DIM, WIN_SUB), BF16)

    for j in range(S5_WIDTH // 256):
        u_ref[:, 256 * j:256 * (j + 1)] = _dot(xn, w_ref[:, _C_U + 256 * j:_C_U + 256 * (j + 1)])
    gates_t = jax.nn.sigmoid(_dot(xn, w_ref[:, _C_G:_C_END])).T
    for g in range(N_KV_HEADS):
        gtt_ref[g] = gates_t[g * LANES:g * LANES + GATE_ROWS, :]


def _ffn_proj(x, g_ffn, w1, w3, w2, g, w_in_r, cos, sin_signed, batch, seq):
    t = batch * seq
    tm = TK_SEL
    n = seq // tm
    bl = lambda w: pl.BlockSpec((None, tm, w), lambda i: (i // n, i % n, 0))
    bgl = lambda w: pl.BlockSpec((None, N_KV_HEADS, tm, w), lambda i: (i // n, 0, i % n, 0))
    tab = pl.BlockSpec((tm, LANES), lambda i: (i % n, 0))
    row = pl.BlockSpec((tm, D_MODEL), lambda i: (i, 0))
    sds = jax.ShapeDtypeStruct
    return pl.pallas_call(
        functools.partial(_ffn_proj_body, tiles_per_seq=n),
        grid=(t // tm,),
        in_specs=[row] + [_resident()] * 6 + [tab, tab],
        out_specs=[
            row,
            pl.BlockSpec((None, N_HEADS, HEAD_DIM, tm), lambda i: (i // n, 0, 0, i % n)),
            bl(KV_WIDTH), bl(KV_WIDTH),
            bgl(2 * LANES),
            pl.BlockSpec((None, N_KV_HEADS, None, V_ROWS, tm), lambda i: (i // n, 0, i % n, 0, 0)),
            bgl(HEAD_DIM),
            pl.BlockSpec((None, N_KV_HEADS, tm // WIN_SUB, V_ROWS, WIN_SUB), lambda i: (i // n, 0, i % n, 0, 0)),
            pl.BlockSpec((None, N_KV_HEADS, GATE_ROWS, tm), lambda i: (i // n, 0, 0, i % n)),
            bl(S5_WIDTH),
        ],
        out_shape=[
            sds((t, D_MODEL), F32),
            sds((batch, N_HEADS, HEAD_DIM, seq), BF16),
            sds((batch, seq, KV_WIDTH), F32), sds((batch, seq, KV_WIDTH), F32),
            sds((batch, N_KV_HEADS, seq, 2 * LANES), BF16), sds((batch, N_KV_HEADS, n, V_ROWS, tm), BF16),
            sds((batch, N_KV_HEADS, seq, HEAD_DIM), BF16), sds((batch, N_KV_HEADS, seq // WIN_SUB, V_ROWS, WIN_SUB), BF16),
            sds((batch, N_KV_HEADS, GATE_ROWS, seq), F32), sds((batch, seq, S5_WIDTH), F32),
        ],
        compiler_params=_params(1),
        name="ffn_proj",
    )(x, g_ffn, w1, w3, w2, g, w_in_r, cos, sin_signed)


def _compress_body(kc_ref, vc_ref, pek_ref, pev_ref, wkt_ref, wkb_ref, wvt_ref, wvb_ref, wk2_ref, wv2_ref,
                   ko_ref, vot_ref, v_ref):
    nc = kc_ref.shape[0]

    def hidden(x, pe_ref, wt_ref, wb_ref):
        top = _dot((x + pe_ref[0:1, :]).astype(BF16), wt_ref[...])
        bot = _dot((x + pe_ref[1:2, :]).astype(BF16), wb_ref[...])
        pre = top + pltpu.roll(bot, nc - 1, 0)
        return _gelu_tanh(pre).astype(BF16)

    hk = hidden(kc_ref[...], pek_ref, wkt_ref, wkb_ref)
    hv = hidden(vc_ref[...], pev_ref, wvt_ref, wvb_ref)
    for g in range(N_KV_HEADS):
        gs = slice(g * CMP_HIDDEN, (g + 1) * CMP_HIDDEN)
        ko_ref[g] = _dot(hk[:, gs], wk2_ref[...]).astype(BF16)
        v_ref[...] = _dot(hv[:, gs], wv2_ref[...])
        vot_ref[g] = v_ref[...].T.astype(BF16)


def _compress(kc, vc, pe_k2, pe_v2, wkt, wkb, wvt, wvb, wk2, wv2):
    batch, seq, _ = kc.shape
    nc = seq // CMP_STRIDE
    width = CMP_STRIDE * KV_WIDTH
    x_spec = pl.BlockSpec((None, nc, width), lambda b: (b, 0, 0))
    sds = jax.ShapeDtypeStruct
    return pl.pallas_call(
        _compress_body,
        grid=(batch,),
        in_specs=[x_spec, x_spec] + [_resident()] * 8,
        out_specs=[pl.BlockSpec((None, N_KV_HEADS, nc, HEAD_DIM), lambda b: (b, 0, 0, 0)),
                   pl.BlockSpec((None, N_KV_HEADS, HEAD_DIM, nc), lambda b: (b, 0, 0, 0))],
        out_shape=[sds((batch, N_KV_HEADS, nc, HEAD_DIM), BF16), sds((batch, N_KV_HEADS, HEAD_DIM, nc), BF16)],
        scratch_shapes=[pltpu.VMEM((nc, HEAD_DIM), F32)],
        compiler_params=_params(1),
        name="compress",
    )(kc.reshape(batch, nc, width), vc.reshape(batch, nc, width), pe_k2, pe_v2, wkt, wkb, wvt, wvb, wk2, wv2)


def _cmpwin_body(qt_ref, kc_ref, vct_ref, ovt_ref, kw_ref, vwt_ref, gtt_ref, opt_ref, bias_ref,
                 s_ref, ocmp_ref, imp_ref):
    tq = qt_ref.shape[2]
    cols = HEADS_PER_GROUP * tq
    nc = kc_ref.shape[0]
    i = pl.program_id(2)
    q_t = jnp.concatenate([qt_ref[h] for h in range(HEADS_PER_GROUP)], axis=1)
    t_lane = i * tq + lax.broadcasted_iota(jnp.int32, (1, cols), 1) % tq

    def compressed(n):
        s = _dot(kc_ref[0:n, :], q_t)
        c_end = lax.broadcasted_iota(jnp.int32, (n, cols), 0) * CMP_STRIDE + (CMP_LEN - 1)
        s = jnp.where(c_end <= t_lane, s, NEG)
        e = jnp.exp(s - jnp.max(s, axis=0, keepdims=True))
        den = jnp.maximum(jnp.sum(e, axis=0, keepdims=True), 1e-30)
        p = e * jnp.where(t_lane >= CMP_LEN - 1, 1.0 / den, 0.0)
        ocmp_ref[...] = _dot(vct_ref[:, 0:n], p.astype(BF16))
        psum = p[:, 0:tq]
        for h in range(1, HEADS_PER_GROUP):
            psum = psum + p[:, h * tq:(h + 1) * tq]
        p_hi = psum.astype(BF16)
        p_lo = (psum - p_hi.astype(F32)).astype(BF16)
        imp_ref[...] = _dot(ovt_ref[:, 0:n], p_hi) + _dot(ovt_ref[:, 0:n], p_lo)

    finished = (i * tq + tq - CMP_LEN) // CMP_STRIDE + 1
    n_steps = nc // CMP_STEP
    for k in range(1, n_steps + 1):
        in_step = finished > (k - 1) * CMP_STEP
        if k < n_steps:
            in_step = jnp.logical_and(in_step, finished <= k * CMP_STEP)
        pl.when(in_step)(functools.partial(compressed, k * CMP_STEP))
    o_cmp = ocmp_ref[...]
    imp = imp_ref[...]

    blk = lax.broadcasted_iota(jnp.int32, (MAX_SEL_BLOCKS, tq), 0)
    behind = t_lane[:, 0:tq] // SEL_BLOCK - blk
    score = jnp.where(blk == 0, BIG, jnp.where(behind == 0, BIG, jnp.where(behind == 1, BIG, imp)))
    score = jnp.where(behind >= 0, score, -BIG)
    chosen = jnp.zeros((MAX_SEL_BLOCKS, tq), F32)
    for _ in range(SEL_TOPK):
        best = jnp.max(score, axis=0, keepdims=True)
        first = jnp.min(jnp.where(score == best, blk, MAX_SEL_BLOCKS), axis=0, keepdims=True)
        hit = blk == first
        chosen = jnp.where(hit, 1.0, chosen)
        score = jnp.where(hit, -jnp.inf, score)
    bias_ref[...] = jnp.where(chosen > 0.0, jnp.where(behind >= 0, 0.0, NEG), NEG).astype(BF16)

    span = WINDOW + tq
    start = pl.multiple_of(jnp.maximum(i * tq - WINDOW, 0), tq)
    s_ref[...] = _dot(kw_ref[pl.ds(start, span), :], q_t)

    def band(rows):
        kpos = start + rows.start + lax.broadcasted_iota(jnp.int32, (rows.stop - rows.start, cols), 0)
        dist = lax.bitcast_convert_type(t_lane - kpos, jnp.uint32)
        s_ref[rows, :] = jnp.where(dist < WINDOW, s_ref[rows, :], NEG)

    @pl.when(i * tq >= WINDOW)
    def _():
        band(slice(0, tq))
        band(slice(span - tq, span))

    @pl.when(i * tq < WINDOW)
    def _():
        band(slice(0, span))

    s = s_ref[...]
    e = jnp.exp(s - jnp.max(s, axis=0, keepdims=True)).astype(BF16)
    c0 = start // WIN_SUB
    vw_t = jnp.concatenate([vwt_ref[c0 + j] for j in range(span // WIN_SUB)], axis=1)
    acc = _dot(vw_t, e)
    o_win = acc[0:HEAD_DIM, :] / acc[HEAD_DIM:HEAD_DIM + 1, :]

    gates = gtt_ref[...]
    for h in range(HEADS_PER_GROUP):
        hs = slice(h * tq, (h + 1) * tq)
        opt_ref[h * HEAD_DIM:(h + 1) * HEAD_DIM, :] = (gates[3 * h:3 * h + 1, :] * o_cmp[:, hs]
                                                       + gates[3 * h + 2:3 * h + 3, :] * o_win[:, hs])


def _cmpwin(qt, kcmp, vcmp_t, overlap_t, kwin, vwin_t, gates_t):
    batch, _, _, seq = qt.shape
    nc = kcmp.shape[2]
    tq = TQ
    per_bg = lambda *shape: pl.BlockSpec((None, None) + shape, lambda b, g, i: (b, g) + (0,) * len(shape))
    sds = jax.ShapeDtypeStruct
    return pl.pallas_call(
        _cmpwin_body,
        grid=(batch, N_KV_HEADS, seq // tq),
        in_specs=[
            pl.BlockSpec((None, HEADS_PER_GROUP, HEAD_DIM, tq), lambda b, g, i: (b, g, 0, i)),
            per_bg(nc, HEAD_DIM), per_bg(HEAD_DIM, nc), _resident(),
            per_bg(seq, HEAD_DIM), per_bg(seq // WIN_SUB, V_ROWS, WIN_SUB),
            pl.BlockSpec((None, None, GATE_ROWS, tq), lambda b, g, i: (b, g, 0, i)),
        ],
        out_specs=[pl.BlockSpec((None, None, HEADS_PER_GROUP * HEAD_DIM, tq), lambda b, g, i: (b, g, 0, i)),
                   pl.BlockSpec((None, None, MAX_SEL_BLOCKS, tq), lambda b, g, i: (b, g, 0, i))],
        out_shape=[sds((batch, N_KV_HEADS, HEADS_PER_GROUP * HEAD_DIM, seq), F32),
                   sds((batch, N_KV_HEADS, MAX_SEL_BLOCKS, seq), BF16)],
        scratch_shapes=[pltpu.VMEM((WINDOW + tq, HEADS_PER_GROUP * tq), F32),
                        pltpu.VMEM((HEAD_DIM, HEADS_PER_GROUP * tq), F32), pltpu.VMEM((MAX_SEL_BLOCKS, tq), F32)],
        compiler_params=_params(3),
        name="cmpwin",
    )(qt, kcmp, vcmp_t, overlap_t, kwin, vwin_t, gates_t)


def _select_body(qt_ref, bias_ref, ks_ref, vst_ref, gtt_ref, opt_ref, o_ref, qa_ref, m_ref, acc_ref, *head_refs):
    tq = qt_ref.shape[2]
    cols = HEADS_PER_GROUP * tq
    tk = TK_SEL
    i = pl.program_id(2)

    bias = bias_ref[...]
    for h in range(HEADS_PER_GROUP):
        qa_ref[0:LANES, h * tq:(h + 1) * tq] = bias
        qa_ref[LANES:LANES + HEAD_DIM, h * tq:(h + 1) * tq] = qt_ref[h]
    qa_ref[LANES + HEAD_DIM:, :] = jnp.zeros((LANES - HEAD_DIM, cols), BF16)
    m_ref[...] = jnp.full(m_ref.shape, NEG, F32)
    acc_ref[...] = jnp.zeros(acc_ref.shape, F32)

    s_refs, mx_refs = head_refs[:HEADS_PER_GROUP], head_refs[HEADS_PER_GROUP:]

    def scores(c, causal, h):
        hs = slice(h * tq, (h + 1) * tq)
        off = pl.multiple_of(c * tk, tk)
        s = _dot(ks_ref[pl.ds(off, tk), :], qa_ref[:, hs])
        if causal:
            kpos = off + lax.broadcasted_iota(jnp.int32, s.shape, 0)
            t = i * tq + lax.broadcasted_iota(jnp.int32, s.shape, 1) % tq
            s = jnp.where(kpos <= t, s, NEG)
        s_refs[h][c % 2] = s
        mx_refs[h][c % 2] = jnp.max(s, axis=0, keepdims=True)

    def accumulate(c, h):
        hs = slice(h * tq, (h + 1) * tq)
        m_old = m_ref[:, hs]
        m_new = jnp.maximum(m_old, mx_refs[h][c % 2])
        p = jnp.exp(s_refs[h][c % 2] - m_new).astype(BF16)
        acc_ref[:, hs] = jnp.exp(m_old - m_new) * acc_ref[:, hs] + _dot(vst_ref[c], p)
        m_ref[:, hs] = m_new

    heads = range(HEADS_PER_GROUP)

    n_full = (i * tq) // tk

    def pair(c, causal):
        scores(c + 1, causal, HEADS_PER_GROUP - 1)
        for h in heads:
            accumulate(c, h)
            if h > 0:
                scores(c + 1, causal, h - 1)

    @pl.when(n_full > 0)
    def _():
        for h in heads:
            scores(0, False, h)

    def two_pairs(k, carry):
        pair(2 * k, False)
        pair(2 * k + 1, False)
        return carry

    n_plain = jnp.maximum(n_full - 1, 0)
    lax.fori_loop(0, n_plain // 2, two_pairs, 0)

    @pl.when(n_plain % 2 == 1)
    def _():
        pair(n_plain - 1, False)

    @pl.when(n_full > 0)
    def _():
        pair(n_full - 1, True)

    @pl.when(n_full == 0)
    def _():
        for h in heads:
            scores(n_full, True, h)

    for h in heads:
        accumulate(n_full, h)

    acc = acc_ref[...]
    o_slc = acc[0:HEAD_DIM, :] / acc[HEAD_DIM:HEAD_DIM + 1, :]
    gates = gtt_ref[...]
    o_t = jnp.concatenate([gates[3 * h + 1:3 * h + 2, :] * o_slc[:, h * tq:(h + 1) * tq]
                           for h in range(HEADS_PER_GROUP)], axis=0)
    o_ref[...] = (opt_ref[...] + o_t).T.astype(BF16)


def _select(qt, bias_t, ksel, vsel_t, gates_t, o_part_t):
    batch, _, _, seq = qt.shape
    tq = TQ_SEL
    cols = HEADS_PER_GROUP * tq
    n_chunks = seq // TK_SEL
    tile_t = lambda r: pl.BlockSpec((None, None, r, tq), lambda b, g, i: (b, g, 0, i))
    return pl.pallas_call(
        _select_body,
        grid=(batch, N_KV_HEADS, seq // tq),
        in_specs=[
            pl.BlockSpec((None, HEADS_PER_GROUP, HEAD_DIM, tq), lambda b, g, i: (b, g, 0, i)),
            tile_t(MAX_SEL_BLOCKS),
            pl.BlockSpec((None, None, seq, 2 * LANES), lambda b, g, i: (b, g, 0, 0)),
            pl.BlockSpec((None, None, n_chunks, V_ROWS, TK_SEL), lambda b, g, i: (b, g, 0, 0, 0)),
            tile_t(GATE_ROWS), tile_t(HEADS_PER_GROUP * HEAD_DIM),
        ],
        out_specs=pl.BlockSpec((None, tq, HEADS_PER_GROUP * HEAD_DIM), lambda b, g, i: (b, i, g)),
        out_shape=jax.ShapeDtypeStruct((batch, seq, NSA_WIDTH), BF16),
        scratch_shapes=([pltpu.VMEM((2 * LANES, cols), BF16), pltpu.VMEM((1, cols), F32), pltpu.VMEM((V_ROWS, cols), F32)]
                        + [pltpu.VMEM((2, TK_SEL, tq), F32)] * HEADS_PER_GROUP
                        + [pltpu.VMEM((2, 1, tq), F32)] * HEADS_PER_GROUP),
        compiler_params=_params(3),
        name="select",
    )(qt, bias_t, ksel, vsel_t, gates_t, o_part_t)


def _s5_body(u_ref, bt_ref, pw_ref, a1_ref, ltri_ref, ct_ref, d_ref, wg_ref, bg_ref, o_ref,
             c_ref, x_ref, h_ref, f_ref):
    @pl.when(pl.program_id(1) == 0)
    def _():
        c_ref[...] = jnp.zeros(c_ref.shape, F32)

    n = S5_COLS // S5_SPLIT
    width = S5_WIDTH // S5_SPLIT
    chunks = [slice(j * S5_CHUNK, (j + 1) * S5_CHUNK) for j in range(u_ref.shape[0] // S5_CHUNK)]
    last = S5_CHUNK - 1
    ys = []
    for hf in range(S5_SPLIT):
        f_ref[...] = _dot(u_ref[:, hf * width:(hf + 1) * width].astype(BF16), bt_ref[hf])
        for cb in range(n // LANES):
            re, im = slice(cb * LANES, (cb + 1) * LANES), slice(n + cb * LANES, n + (cb + 1) * LANES)
            tab = slice(hf * n + cb * LANES, hf * n + (cb + 1) * LANES)
            q_re, q_im = pw_ref[2, :, tab], pw_ref[3, :, tab]
            for r in chunks:
                b_re, b_im = f_ref[r, re], f_ref[r, im]
                x_ref[r, re] = (q_re * b_re - q_im * b_im).astype(BF16)
                x_ref[r, im] = (q_re * b_im + q_im * b_re).astype(BF16)
        f_ref[...] = _dot(ltri_ref[...], x_ref[...])
        for cb in range(n // LANES):
            re, im = slice(cb * LANES, (cb + 1) * LANES), slice(n + cb * LANES, n + (cb + 1) * LANES)
            tab = slice(hf * n + cb * LANES, hf * n + (cb + 1) * LANES)
            p_re, p_im = pw_ref[0, :, tab], pw_ref[1, :, tab]
            a_re, a_im = a1_ref[0:1, tab], a1_ref[1:2, tab]
            c_re, c_im = c_ref[0:1, tab], c_ref[1:2, tab]
            for r in chunks:
                s_re, s_im = f_ref[r, re] + c_re, f_ref[r, im] + c_im
                h_re, h_im = p_re * s_re - p_im * s_im, p_re * s_im + p_im * s_re
                h_ref[r, re] = h_re.astype(BF16)
                h_ref[r, im] = h_im.astype(BF16)
                l_re, l_im = h_re[last:last + 1], h_im[last:last + 1]
                c_re = a_re * l_re - a_im * l_im
                c_im = a_re * l_im + a_im * l_re
            c_ref[0:1, tab] = c_re
            c_ref[1:2, tab] = c_im
        ys.append(_dot(h_ref[...], ct_ref[hf]))
    y = _gelu_tanh(jnp.concatenate(ys, axis=1) + d_ref[...] * u_ref[...])
    o_ref[...] = (y * jax.nn.sigmoid(_dot(y.astype(BF16), wg_ref[...]) + bg_ref[...])).astype(BF16)


def _s5(u, bt, powers, a1, ltri, ct, d, w_glu, b_glu):
    batch, seq, _ = u.shape
    ts = S5_TILE
    tile = pl.BlockSpec((None, ts, S5_WIDTH), lambda b, j: (b, j, 0))
    cols = 2 * S5_COLS // S5_SPLIT
    return pl.pallas_call(
        _s5_body,
        grid=(batch, seq // ts),
        in_specs=[tile] + [_resident()] * 8,
        out_specs=tile,
        out_shape=jax.ShapeDtypeStruct((batch, seq, S5_WIDTH), BF16),
        scratch_shapes=[pltpu.VMEM((2, S5_COLS), F32), pltpu.VMEM((ts, cols), BF16), pltpu.VMEM((ts, cols), BF16),
                        pltpu.VMEM((ts, cols), F32)],
        compiler_params=_params(2),
        name="s5",
    )(u, bt, powers, a1, ltri, ct, d, w_glu, b_glu)


def _ffn_out_body(*refs, final_norm):
    (h_ref, on_ref, os_ref, won_ref, wos_ref, g2_ref, w1_ref, w3_ref, w2_ref,
     gp_ref, wg_ref, p_ref, wp_ref) = refs[:13]
    o_ref = refs[-1]
    h = h_ref[...] + _dot(on_ref[...], won_ref[...]) + _dot(os_ref[...], wos_ref[...])
    h = _swiglu_half_step(h, g2_ref, w1_ref, w3_ref, w2_ref)
    gate = jax.nn.sigmoid(_dot(_rmsnorm(h, gp_ref[...]).astype(BF16), wg_ref[...]))
    h = h + gate * _dot(p_ref[...].astype(BF16), wp_ref[...])
    o_ref[...] = _rmsnorm(h, refs[13][...]) if final_norm else h


def _ffn_out(h, o_nsa, o_s5, wo_n, wo_s, g_ffn, w1, w3, w2, g_ple, w_gate, p, w_ple, g_final=None):
    t = h.shape[0]
    tm = min(TM_DENSE, t)
    row = lambda w: pl.BlockSpec((tm, w), lambda i: (i, 0))
    args = [h, o_nsa, o_s5, wo_n, wo_s, g_ffn, w1, w3, w2, g_ple, w_gate, p, w_ple]
    specs = [row(D_MODEL), row(NSA_WIDTH), row(S5_WIDTH)] + [_resident()] * 8 + [row(PLE_DIM), _resident()]
    if g_final is not None:
        args.append(g_final)
        specs.append(_resident())
    return pl.pallas_call(
        functools.partial(_ffn_out_body, final_norm=g_final is not None),
        grid=(t // tm,),
        in_specs=specs,
        out_specs=row(D_MODEL),
        out_shape=jax.ShapeDtypeStruct((t, D_MODEL), F32),
        compiler_params=_params(1),
        name="ffn_out",
    )(*args)


def _rearrange_w_in(w_in):
    g0 = NSA_WIDTH + 6 * KV_WIDTH
    per_group = 3 * HEADS_PER_GROUP
    pad = jnp.zeros((D_MODEL, LANES - per_group), w_in.dtype)
    gate_cols = []
    for g in range(N_KV_HEADS):
        gate_cols += [w_in[:, g0 + g * per_group:g0 + (g + 1) * per_group], pad]
    u0 = g0 + 3 * N_HEADS
    return jnp.concatenate([w_in[:, :g0], w_in[:, u0:]] + gate_cols, axis=1).astype(BF16)


def _rope_tables(seq):
    half = HEAD_DIM // 2
    inv = ROPE_THETA ** (-jnp.arange(half, dtype=F32) / half)
    ang = jnp.arange(seq, dtype=F32)[:, None] * inv[None, :]
    cos = jnp.tile(jnp.cos(ang), (1, LANES // half))
    sin = jnp.tile(jnp.concatenate([-jnp.sin(ang), jnp.sin(ang)], axis=1), (1, LANES // HEAD_DIM))
    return cos, sin


def _compress_weights(pe, w1, w2):
    w1 = w1.reshape(2, CMP_STRIDE, HEAD_DIM, CMP_HIDDEN)
    eye = jnp.eye(N_KV_HEADS, dtype=w1.dtype)
    big = jnp.einsum("sjdh,ge->sjgdeh", w1, eye).reshape(2, CMP_STRIDE * KV_WIDTH, N_KV_HEADS * CMP_HIDDEN)
    pe2 = jnp.broadcast_to(pe.reshape(2, CMP_STRIDE, 1, HEAD_DIM), (2, CMP_STRIDE, N_KV_HEADS, HEAD_DIM))
    return pe2.reshape(2, CMP_STRIDE * KV_WIDTH), big[0].astype(BF16), big[1].astype(BF16), w2.astype(BF16)


def _overlap_matrix(n_cmp_rows):
    c_start = np.arange(n_cmp_rows)[:, None] * CMP_STRIDE
    s_start = np.arange(MAX_SEL_BLOCKS)[None, :] * SEL_BLOCK
    ov = (c_start < s_start + SEL_BLOCK) & (c_start + CMP_LEN > s_start)
    return jnp.asarray(ov.T, dtype=BF16)


def _s5_tables(a_re, a_im, log_dt, b_re, b_im, c_re, c_im):
    dt = jnp.exp(log_dt)[:, None]
    lr, li = a_re * dt, a_im * dt
    mag = jnp.exp(lr)
    ab_re, ab_im = mag * jnp.cos(li), mag * jnp.sin(li)
    den = a_re * a_re + a_im * a_im
    nr, ni = ab_re - 1.0, ab_im
    bc_re = (nr * a_re + ni * a_im) / den
    bc_im = (ni * a_re - nr * a_im) / den
    bt_re = b_re * bc_re[..., None] - b_im * bc_im[..., None]
    bt_im = b_re * bc_im[..., None] + b_im * bc_re[..., None]
    gh = S5_GROUPS // S5_SPLIT
    eye = jnp.eye(gh, dtype=F32)
    split = lambda w: w.reshape((S5_SPLIT, gh) + w.shape[1:])
    blockdiag_in = lambda w: jnp.einsum("hgni,ge->hgien", split(w), eye).reshape(S5_SPLIT, gh * S5_GROUP, gh * S5_STATE)
    bt = jnp.concatenate([blockdiag_in(bt_re), blockdiag_in(bt_im)], axis=2).astype(BF16)
    blockdiag_out = lambda w: jnp.einsum("hgon,ge->hgneo", split(w), eye).reshape(S5_SPLIT, gh * S5_STATE, gh * S5_GROUP)
    ct = jnp.concatenate([blockdiag_out(c_re), -blockdiag_out(c_im)], axis=1).astype(BF16)
    k = jnp.arange(S5_CHUNK, dtype=F32)[:, None]
    lr, li = lr.reshape(1, S5_COLS), li.reshape(1, S5_COLS)
    powers = jnp.stack([jnp.exp(k * lr) * jnp.cos(k * li), jnp.exp(k * lr) * jnp.sin(k * li),
                        jnp.exp(-k * lr) * jnp.cos(k * li), -jnp.exp(-k * lr) * jnp.sin(k * li)])
    a1 = jnp.concatenate([ab_re.reshape(1, S5_COLS), ab_im.reshape(1, S5_COLS)], axis=0)
    ltri = jnp.asarray(np.kron(np.eye(S5_TILE // S5_CHUNK), np.tril(np.ones((S5_CHUNK, S5_CHUNK)))), dtype=BF16)
    return bt, powers, a1, ltri, ct


def kernel(x, p, norm_ffn1, ffn1_w1, ffn1_w3, ffn1_w2, norm_mix, w_in, cmp_pe_k, cmp_pe_v, cmp_wk1, cmp_wk2,
           cmp_wv1, cmp_wv2, s5_a_re, s5_a_im, s5_log_dt, s5_b_re, s5_b_im, s5_c_re, s5_c_im, s5_d, s5_w_glu,
           s5_b_glu, w_out, norm_ffn2, ffn2_w1, ffn2_w3, ffn2_w2, norm_ple, w_ple_gate, w_ple, norm_final):
    batch, seq, _ = x.shape
    depth = p.shape[0]
    t = batch * seq
    assert seq % TK_SEL == 0 and TK_SEL % TQ_SEL == 0 and seq % TQ_SEL == 0 and seq % S5_TILE == 0 and seq >= WINDOW + TQ and seq % (CMP_STRIDE * CMP_STEP) == 0
    assert seq // SEL_BLOCK <= MAX_SEL_BLOCKS and seq // SEL_BLOCK >= SEL_TOPK
    bf = lambda w: w.astype(BF16)
    row = lambda v: v.reshape(1, -1)

    cos, sin_signed = _rope_tables(seq)
    overlap_t = _overlap_matrix(seq // CMP_STRIDE)
    h = x.reshape(t, D_MODEL)
    for i in range(depth):
        h, qt, kc, vc, ksel, vsel_t, kwin, vwin_t, gates_t, u = _ffn_proj(
            h, row(norm_ffn1[i]), bf(ffn1_w1[i]), bf(ffn1_w3[i]), bf(ffn1_w2[i]),
            row(norm_mix[i]), _rearrange_w_in(w_in[i]), cos, sin_signed, batch, seq)
        pe_k2, wkt, wkb, wk2 = _compress_weights(cmp_pe_k[i], cmp_wk1[i], cmp_wk2[i])
        pe_v2, wvt, wvb, wv2 = _compress_weights(cmp_pe_v[i], cmp_wv1[i], cmp_wv2[i])
        kcmp, vcmp_t = _compress(kc, vc, pe_k2, pe_v2, wkt, wkb, wvt, wvb, wk2, wv2)
        o_part_t, bias_t = _cmpwin(qt, kcmp, vcmp_t, overlap_t, kwin, vwin_t, gates_t)
        o_nsa = _select(qt, bias_t, ksel, vsel_t, gates_t, o_part_t)

        bt, powers, a1, ltri, ct = _s5_tables(s5_a_re[i], s5_a_im[i], s5_log_dt[i], s5_b_re[i], s5_b_im[i],
                                              s5_c_re[i], s5_c_im[i])
        o_s5 = _s5(u, bt, powers, a1, ltri, ct, row(s5_d[i]), bf(s5_w_glu[i]), row(s5_b_glu[i]))

        h = _ffn_out(h, o_nsa.reshape(t, NSA_WIDTH), o_s5.reshape(t, S5_WIDTH),
                     bf(w_out[i][:NSA_WIDTH]), bf(w_out[i][NSA_WIDTH:]),
                     row(norm_ffn2[i]), bf(ffn2_w1[i]), bf(ffn2_w3[i]), bf(ffn2_w2[i]),
                     row(norm_ple[i]), bf(w_ple_gate[i]), p[i].reshape(t, PLE_DIM), bf(w_ple[i]),
                     row(norm_final) if i == depth - 1 else None)
    return h.reshape(batch, seq, D_MODEL)
```

```python
import functools

import jax
import jax.numpy as jnp
import numpy as np
from jax import lax
from jax.experimental import pallas as pl
from jax.experimental.pallas import tpu as pltpu

D_MODEL = 1024
PLE_DIM = 256
D_FF = 2816
N_HEADS = 8
N_KV_HEADS = 2
HEAD_DIM = 64
HEADS_PER_GROUP = N_HEADS // N_KV_HEADS
NSA_WIDTH = N_HEADS * HEAD_DIM
KV_WIDTH = N_KV_HEADS * HEAD_DIM
S5_WIDTH = D_MODEL - NSA_WIDTH
S5_GROUP = 16
S5_GROUPS = S5_WIDTH // S5_GROUP
S5_STATE = 64
S5_COLS = S5_GROUPS * S5_STATE
CMP_LEN = 32
CMP_STRIDE = 16
CMP_HIDDEN = 256
SEL_BLOCK = 64
SEL_TOPK = 16
WINDOW = 512
ROPE_THETA = 10000.0
RMS_EPS = 1e-6
NEG = -1e30
BIG = 1e9

LANES = 128
V_ROWS = HEAD_DIM + 16
GATE_ROWS = 16
MAX_SEL_BLOCKS = LANES
VMEM_LIMIT = 52 * 1024 * 1024

TM_DENSE = 512
FF_CHUNK = 256
TQ = 256
WIN_SUB = 128
CMP_STEP = 128
TQ_SEL = 512
TK_SEL = 512
S5_TILE = 256
S5_CHUNK = 64
S5_SPLIT = 2

F32 = jnp.float32
BF16 = jnp.bfloat16


def _dot(a, b):
    return jnp.dot(a, b, preferred_element_type=F32)


def _rmsnorm(x, g):
    ms = jnp.mean(x * x, axis=-1, keepdims=True)
    return x * lax.rsqrt(ms + RMS_EPS) * g


def _gelu_tanh(x):
    return 0.5 * x * (1.0 + jnp.tanh(np.sqrt(2.0 / np.pi).astype(np.float32) * (x + 0.044715 * (x * x * x))))


def _params(n_grid):
    return pltpu.CompilerParams(dimension_semantics=("arbitrary",) * n_grid, vmem_limit_bytes=VMEM_LIMIT)


def _resident():
    return pl.BlockSpec(memory_space=pltpu.VMEM)


def _swiglu_half_step(h, g_ref, w1_ref, w3_ref, w2_ref):
    xn = _rmsnorm(h, g_ref[...]).astype(BF16)
    acc = jnp.zeros(h.shape, F32)
    for j in range(D_FF // FF_CHUNK):
        sl = slice(j * FF_CHUNK, (j + 1) * FF_CHUNK)
        a = _dot(xn, w1_ref[:, sl])
        b = _dot(xn, w3_ref[:, sl])
        act = (a * jax.nn.sigmoid(a) * b).astype(BF16)
        acc = acc + _dot(act, w2_ref[sl, :])
    return h + 0.5 * acc


_C_Q = 0
_C_CMP = NSA_WIDTH
_C_SLC = _C_CMP + 2 * KV_WIDTH
_C_WIN = _C_SLC + 2 * KV_WIDTH
_C_U = _C_WIN + 2 * KV_WIDTH
_C_G = _C_U + S5_WIDTH
_C_END = _C_G + N_KV_HEADS * LANES


def _rope_pair(x, cos, sin_signed, first_half):
    fwd = pltpu.roll(x, HEAD_DIM // 2, 1)
    bwd = pltpu.roll(x, LANES - HEAD_DIM // 2, 1)
    return x * cos + jnp.where(first_half, bwd, fwd) * sin_signed


def _ffn_proj_body(x_ref, g1_ref, w1_ref, w3_ref, w2_ref, g_ref, w_ref, cos_ref, sin_ref,
                   h_ref, qt_ref, kc_ref, vc_ref, ks_ref, vst_ref, kw_ref, vwt_ref, gtt_ref, u_ref, *, tiles_per_seq):
    tm = x_ref.shape[0]
    h = _swiglu_half_step(x_ref[...], g1_ref, w1_ref, w3_ref, w2_ref)
    h_ref[...] = h
    xn = _rmsnorm(h, g_ref[...]).astype(BF16)
    cos = cos_ref[...]
    sin = sin_ref[...]
    lane = lax.broadcasted_iota(jnp.int32, (tm, LANES), 1)
    first_half = (lane % HEAD_DIM) < (HEAD_DIM // 2)
    rope = lambda x: _rope_pair(x, cos, sin, first_half)

    scale = HEAD_DIM ** -0.5
    for j in range(NSA_WIDTH // 256):
        z = _dot(xn, w_ref[:, _C_Q + 256 * j:_C_Q + 256 * (j + 1)])
        for k in range(2):
            rt = (rope(z[:, LANES * k:LANES * (k + 1)]) * scale).T.astype(BF16)
            for e in range(2):
                qt_ref[4 * j + 2 * k + e] = rt[e * HEAD_DIM:(e + 1) * HEAD_DIM, :]

    z = _dot(xn, w_ref[:, _C_CMP:_C_CMP + 256])
    kc_ref[...] = rope(z[:, :LANES])
    vc_ref[...] = z[:, LANES:]

    zeros = jnp.zeros((tm, HEAD_DIM), BF16)
    pos = (pl.program_id(0) % tiles_per_seq) * tm + lax.broadcasted_iota(jnp.int32, (tm, LANES), 0)
    onehot = jnp.where(lane == pos // SEL_BLOCK, 1.0, 0.0).astype(BF16)

    z = _dot(xn, w_ref[:, _C_SLC:_C_SLC + 256])
    k = rope(z[:, :LANES]).astype(BF16)
    vt = z[:, LANES:].T.astype(BF16)
    for g in range(N_KV_HEADS):
        gs = slice(g * HEAD_DIM, (g + 1) * HEAD_DIM)
        ks_ref[g, :, 0:LANES] = onehot
        ks_ref[g, :, LANES:LANES + HEAD_DIM] = k[:, gs]
        ks_ref[g, :, LANES + HEAD_DIM:] = zeros
        vst_ref[g, 0:HEAD_DIM, :] = vt[gs, :]
        vst_ref[g, HEAD_DIM:, :] = jnp.ones((V_ROWS - HEAD_DIM, tm), BF16)

    z = _dot(xn, w_ref[:, _C_WIN:_C_WIN + 256])
    k = rope(z[:, :LANES]).astype(BF16)
    vt = z[:, LANES:].T.astype(BF16)
    for g in range(N_KV_HEADS):
        gs = slice(g * HEAD_DIM, (g + 1) * HEAD_DIM)
        kw_ref[g] = k[:, gs]
        for c in range(tm // WIN_SUB):
            vwt_ref[g, c, 0:HEAD_DIM, :] = vt[gs, c * WIN_SUB:(c + 1) * WIN_SUB]
            vwt_ref[g, c, HEAD_DIM:, :] = jnp.ones((V_ROWS - HEAD_DIM, WIN_SUB), BF16)

    for j in range(S5_WIDTH // 256):
        u_ref[:, 256 * j:256 * (j + 1)] = _dot(xn, w_ref[:, _C_U + 256 * j:_C_U + 256 * (j + 1)])
    gates_t = jax.nn.sigmoid(_dot(xn, w_ref[:, _C_G:_C_END])).T
    for g in range(N_KV_HEADS):
        gtt_ref[g] = gates_t[g * LANES:g * LANES + GATE_ROWS, :]


def _ffn_proj(x, g_ffn, w1, w3, w2, g, w_in_r, cos, sin_signed, batch, seq):
    t = batch * seq
    tm = TK_SEL
    n = seq // tm
    bl = lambda w: pl.BlockSpec((None, tm, w), lambda i: (i // n, i % n, 0))
    bgl = lambda w: pl.BlockSpec((None, N_KV_HEADS, tm, w), lambda i: (i // n, 0, i % n, 0))
    tab = pl.BlockSpec((tm, LANES), lambda i: (i % n, 0))
    row = pl.BlockSpec((tm, D_MODEL), lambda i: (i, 0))
    sds = jax.ShapeDtypeStruct
    return pl.pallas_call(
        functools.partial(_ffn_proj_body, tiles_per_seq=n),
        grid=(t // tm,),
        in_specs=[row] + [_resident()] * 6 + [tab, tab],
        out_specs=[
            row,
            pl.BlockSpec((None, N_HEADS, HEAD_DIM, tm), lambda i: (i // n, 0, 0, i % n)),
            bl(KV_WIDTH), bl(KV_WIDTH),
            bgl(2 * LANES),
            pl.BlockSpec((None, N_KV_HEADS, None, V_ROWS, tm), lambda i: (i // n, 0, i % n, 0, 0)),
            bgl(HEAD_DIM),
            pl.BlockSpec((None, N_KV_HEADS, tm // WIN_SUB, V_ROWS, WIN_SUB), lambda i: (i // n, 0, i % n, 0, 0)),
            pl.BlockSpec((None, N_KV_HEADS, GATE_ROWS, tm), lambda i: (i // n, 0, 0, i % n)),
            bl(S5_WIDTH),
        ],
        out_shape=[
            sds((t, D_MODEL), F32),
            sds((batch, N_HEADS, HEAD_DIM, seq), BF16),
            sds((batch, seq, KV_WIDTH), F32), sds((batch, seq, KV_WIDTH), F32),
            sds((batch, N_KV_HEADS, seq, 2 * LANES), BF16), sds((batch, N_KV_HEADS, n, V_ROWS, tm), BF16),
            sds((batch, N_KV_HEADS, seq, HEAD_DIM), BF16), sds((batch, N_KV_HEADS, seq // WIN_SUB, V_ROWS, WIN_SUB), BF16),
            sds((batch, N_KV_HEADS, GATE_ROWS, seq), F32), sds((batch, seq, S5_WIDTH), F32),
        ],
        compiler_params=_params(1),
        name="ffn_proj",
    )(x, g_ffn, w1, w3, w2, g, w_in_r, cos, sin_signed)


def _compress_body(kc_ref, vc_ref, pek_ref, pev_ref, wkt_ref, wkb_ref, wvt_ref, wvb_ref, wk2_ref, wv2_ref,
                   ko_ref, vot_ref, v_ref):
    nc = kc_ref.shape[0]

    def hidden(x, pe_ref, wt_ref, wb_ref):
        top = _dot((x + pe_ref[0:1, :]).astype(BF16), wt_ref[...])
        bot = _dot((x + pe_ref[1:2, :]).astype(BF16), wb_ref[...])
        pre = top + pltpu.roll(bot, nc - 1, 0)
        return _gelu_tanh(pre).astype(BF16)

    hk = hidden(kc_ref[...], pek_ref, wkt_ref, wkb_ref)
    hv = hidden(vc_ref[...], pev_ref, wvt_ref, wvb_ref)
    for g in range(N_KV_HEADS):
        gs = slice(g * CMP_HIDDEN, (g + 1) * CMP_HIDDEN)
        ko_ref[g] = _dot(hk[:, gs], wk2_ref[...]).astype(BF16)
        v_ref[...] = _dot(hv[:, gs], wv2_ref[...])
        vot_ref[g] = v_ref[...].T.astype(BF16)


def _compress(kc, vc, pe_k2, pe_v2, wkt, wkb, wvt, wvb, wk2, wv2):
    batch, seq, _ = kc.shape
    nc = seq // CMP_STRIDE
    width = CMP_STRIDE * KV_WIDTH
    x_spec = pl.BlockSpec((None, nc, width), lambda b: (b, 0, 0))
    sds = jax.ShapeDtypeStruct
    return pl.pallas_call(
        _compress_body,
        grid=(batch,),
        in_specs=[x_spec, x_spec] + [_resident()] * 8,
        out_specs=[pl.BlockSpec((None, N_KV_HEADS, nc, HEAD_DIM), lambda b: (b, 0, 0, 0)),
                   pl.BlockSpec((None, N_KV_HEADS, HEAD_DIM, nc), lambda b: (b, 0, 0, 0))],
        out_shape=[sds((batch, N_KV_HEADS, nc, HEAD_DIM), BF16), sds((batch, N_KV_HEADS, HEAD_DIM, nc), BF16)],
        scratch_shapes=[pltpu.VMEM((nc, HEAD_DIM), F32)],
        compiler_params=_params(1),
        name="compress",
    )(kc.reshape(batch, nc, width), vc.reshape(batch, nc, width), pe_k2, pe_v2, wkt, wkb, wvt, wvb, wk2, wv2)


def _cmpwin_body(qt_ref, kc_ref, vct_ref, ovt_ref, edge_ref, kw_ref, vwt_ref, gtt_ref, opt_ref, bias_ref,
                 s_ref, ocmp_ref, imp_ref):
    tq = qt_ref.shape[2]
    cols = HEADS_PER_GROUP * tq
    nc = kc_ref.shape[0]
    i = pl.program_id(2)
    q_t = jnp.concatenate([qt_ref[h] for h in range(HEADS_PER_GROUP)], axis=1)
    t_lane = i * tq + lax.broadcasted_iota(jnp.int32, (1, cols), 1) % tq

    def compressed(n):
        s = _dot(kc_ref[0:n, :], q_t)
        c_end = lax.broadcasted_iota(jnp.int32, (n, cols), 0) * CMP_STRIDE + (CMP_LEN - 1)
        s = jnp.where(c_end <= t_lane, s, NEG)
        e = jnp.exp(s - jnp.max(s, axis=0, keepdims=True))
        den = jnp.maximum(jnp.sum(e, axis=0, keepdims=True), 1e-30)
        p = e * jnp.where(t_lane >= CMP_LEN - 1, 1.0 / den, 0.0)
        ocmp_ref[...] = _dot(vct_ref[:, 0:n], p.astype(BF16))
        psum = p[:, 0:tq]
        for h in range(1, HEADS_PER_GROUP):
            psum = psum + p[:, h * tq:(h + 1) * tq]
        p_hi = psum.astype(BF16)
        p_lo = (psum - p_hi.astype(F32)).astype(BF16)
        imp_ref[...] = _dot(ovt_ref[:, 0:n], p_hi) + _dot(ovt_ref[:, 0:n], p_lo)

    finished = (i * tq + tq - CMP_LEN) // CMP_STRIDE + 1
    n_steps = nc // CMP_STEP
    for k in range(1, n_steps + 1):
        in_step = finished > (k - 1) * CMP_STEP
        if k < n_steps:
            in_step = jnp.logical_and(in_step, finished <= k * CMP_STEP)
        pl.when(in_step)(functools.partial(compressed, k * CMP_STEP))
    o_cmp = ocmp_ref[...]
    imp = imp_ref[...]

    blk = lax.broadcasted_iota(jnp.int32, (MAX_SEL_BLOCKS, tq), 0)
    behind = t_lane[:, 0:tq] // SEL_BLOCK - blk
    forced = jnp.where(blk == 0, 1.0, jnp.where(behind == 0, 1.0, jnp.where(behind == 1, 1.0, 0.0)))

    def top_k(score, chosen, rounds):
        for _ in range(rounds):
            best = jnp.max(score, axis=0, keepdims=True)
            first = jnp.min(jnp.where(score == best, blk, MAX_SEL_BLOCKS), axis=0, keepdims=True)
            hit = blk == first
            chosen = jnp.where(hit, 1.0, chosen)
            score = jnp.where(hit, -jnp.inf, score)
        bias_ref[...] = jnp.where(chosen > 0.0, jnp.where(behind >= 0, 0.0, NEG), NEG).astype(BF16)

    n_forced = 3

    @pl.when(i * tq >= (n_forced - 1) * SEL_BLOCK)
    def _():
        top_k(jnp.where(forced > 0.0, -jnp.inf, jnp.where(behind >= 0, imp, -BIG)), forced, SEL_TOPK - n_forced)

    @pl.when(i * tq < (n_forced - 1) * SEL_BLOCK)
    def _():
        score = jnp.where(forced > 0.0, BIG, imp)
        top_k(jnp.where(behind >= 0, score, -BIG), jnp.zeros((MAX_SEL_BLOCKS, tq), F32), SEL_TOPK)

    span = WINDOW + tq
    start = pl.multiple_of(jnp.maximum(i * tq - WINDOW, 0), tq)

    @pl.when(i * tq >= WINDOW)
    def _():
        s_ref[...] = edge_ref[...] + _dot(kw_ref[pl.ds(start, span), :], q_t)

    @pl.when(i * tq < WINDOW)
    def _():
        kpos = start + lax.broadcasted_iota(jnp.int32, (span, cols), 0)
        dist = lax.bitcast_convert_type(t_lane - kpos, jnp.uint32)
        s_ref[...] = jnp.where(dist < WINDOW, _dot(kw_ref[pl.ds(start, span), :], q_t), NEG)

    s = s_ref[...]
    e = jnp.exp(s - jnp.max(s, axis=0, keepdims=True)).astype(BF16)
    c0 = start // WIN_SUB
    vw_t = jnp.concatenate([vwt_ref[c0 + j] for j in range(span // WIN_SUB)], axis=1)
    acc = _dot(vw_t, e)
    o_win = acc[0:HEAD_DIM, :] / acc[HEAD_DIM:HEAD_DIM + 1, :]

    gates = gtt_ref[...]
    for h in range(HEADS_PER_GROUP):
        hs = slice(h * tq, (h + 1) * tq)
        opt_ref[h * HEAD_DIM:(h + 1) * HEAD_DIM, :] = (gates[3 * h:3 * h + 1, :] * o_cmp[:, hs]
                                                       + gates[3 * h + 2:3 * h + 3, :] * o_win[:, hs])


def _window_edge():
    r = np.arange(WINDOW + TQ)[:, None]
    c = np.arange(HEADS_PER_GROUP * TQ)[None, :] % TQ
    dist = WINDOW + c - r
    return jnp.asarray(np.where((dist >= 0) & (dist < WINDOW), 0.0, NEG), dtype=F32)


def _cmpwin(qt, kcmp, vcmp_t, overlap_t, kwin, vwin_t, gates_t):
    batch, _, _, seq = qt.shape
    nc = kcmp.shape[2]
    tq = TQ
    per_bg = lambda *shape: pl.BlockSpec((None, None) + shape, lambda b, g, i: (b, g) + (0,) * len(shape))
    sds = jax.ShapeDtypeStruct
    return pl.pallas_call(
        _cmpwin_body,
        grid=(batch, N_KV_HEADS, seq // tq),
        in_specs=[
            pl.BlockSpec((None, HEADS_PER_GROUP, HEAD_DIM, tq), lambda b, g, i: (b, g, 0, i)),
            per_bg(nc, HEAD_DIM), per_bg(HEAD_DIM, nc), _resident(), _resident(),
            per_bg(seq, HEAD_DIM), per_bg(seq // WIN_SUB, V_ROWS, WIN_SUB),
            pl.BlockSpec((None, None, GATE_ROWS, tq), lambda b, g, i: (b, g, 0, i)),
        ],
        out_specs=[pl.BlockSpec((None, None, HEADS_PER_GROUP * HEAD_DIM, tq), lambda b, g, i: (b, g, 0, i)),
                   pl.BlockSpec((None, None, MAX_SEL_BLOCKS, tq), lambda b, g, i: (b, g, 0, i))],
        out_shape=[sds((batch, N_KV_HEADS, HEADS_PER_GROUP * HEAD_DIM, seq), F32),
                   sds((batch, N_KV_HEADS, MAX_SEL_BLOCKS, seq), BF16)],
        scratch_shapes=[pltpu.VMEM((WINDOW + tq, HEADS_PER_GROUP * tq), F32),
                        pltpu.VMEM((HEAD_DIM, HEADS_PER_GROUP * tq), F32), pltpu.VMEM((MAX_SEL_BLOCKS, tq), F32)],
        compiler_params=_params(3),
        name="cmpwin",
    )(qt, kcmp, vcmp_t, overlap_t, _window_edge(), kwin, vwin_t, gates_t)


def _select_body(qt_ref, bias_ref, ks_ref, vst_ref, gtt_ref, opt_ref, o_ref, qa_ref, m_ref, acc_ref, *head_refs):
    tq = qt_ref.shape[2]
    cols = HEADS_PER_GROUP * tq
    tk = TK_SEL
    i = pl.program_id(2)

    bias = bias_ref[...]
    for h in range(HEADS_PER_GROUP):
        qa_ref[0:LANES, h * tq:(h + 1) * tq] = bias
        qa_ref[LANES:LANES + HEAD_DIM, h * tq:(h + 1) * tq] = qt_ref[h]
    qa_ref[LANES + HEAD_DIM:, :] = jnp.zeros((LANES - HEAD_DIM, cols), BF16)
    m_ref[...] = jnp.full(m_ref.shape, NEG, F32)
    acc_ref[...] = jnp.zeros(acc_ref.shape, F32)

    s_refs, mx_refs = head_refs[:HEADS_PER_GROUP], head_refs[HEADS_PER_GROUP:]

    def scores(c, causal, h):
        hs = slice(h * tq, (h + 1) * tq)
        off = pl.multiple_of(c * tk, tk)
        s = _dot(ks_ref[pl.ds(off, tk), :], qa_ref[:, hs])
        if causal:
            kpos = off + lax.broadcasted_iota(jnp.int32, s.shape, 0)
            t = i * tq + lax.broadcasted_iota(jnp.int32, s.shape, 1) % tq
            s = jnp.where(kpos <= t, s, NEG)
        s_refs[h][c % 2] = s
        mx_refs[h][c % 2] = jnp.max(s, axis=0, keepdims=True)

    def accumulate(c, h):
        hs = slice(h * tq, (h + 1) * tq)
        m_old = m_ref[:, hs]
        m_new = jnp.maximum(m_old, mx_refs[h][c % 2])
        p = jnp.exp(s_refs[h][c % 2] - m_new).astype(BF16)
        acc_ref[:, hs] = jnp.exp(m_old - m_new) * acc_ref[:, hs] + _dot(vst_ref[c], p)
        m_ref[:, hs] = m_new

    heads = range(HEADS_PER_GROUP)

    n_full = (i * tq) // tk

    def pair(c, causal):
        scores(c + 1, causal, HEADS_PER_GROUP - 1)
        for h in heads:
            accumulate(c, h)
            if h > 0:
                scores(c + 1, causal, h - 1)

    @pl.when(n_full > 0)
    def _():
        for h in heads:
            scores(0, False, h)

    def two_pairs(k, carry):
        pair(2 * k, False)
        pair(2 * k + 1, False)
        return carry

    n_plain = jnp.maximum(n_full - 1, 0)
    lax.fori_loop(0, n_plain // 2, two_pairs, 0)

    @pl.when(n_plain % 2 == 1)
    def _():
        pair(n_plain - 1, False)

    @pl.when(n_full > 0)
    def _():
        pair(n_full - 1, True)

    @pl.when(n_full == 0)
    def _():
        for h in heads:
            scores(n_full, True, h)

    for h in heads:
        accumulate(n_full, h)

    acc = acc_ref[...]
    o_slc = acc[0:HEAD_DIM, :] / acc[HEAD_DIM:HEAD_DIM + 1, :]
    gates = gtt_ref[...]
    o_t = jnp.concatenate([gates[3 * h + 1:3 * h + 2, :] * o_slc[:, h * tq:(h + 1) * tq]
                           for h in range(HEADS_PER_GROUP)], axis=0)
    o_ref[...] = (opt_ref[...] + o_t).T.astype(BF16)


def _select(qt, bias_t, ksel, vsel_t, gates_t, o_part_t):
    batch, _, _, seq = qt.shape
    tq = TQ_SEL
    cols = HEADS_PER_GROUP * tq
    n_chunks = seq // TK_SEL
    tile_t = lambda r: pl.BlockSpec((None, None, r, tq), lambda b, g, i: (b, g, 0, i))
    return pl.pallas_call(
        _select_body,
        grid=(batch, N_KV_HEADS, seq // tq),
        in_specs=[
            pl.BlockSpec((None, HEADS_PER_GROUP, HEAD_DIM, tq), lambda b, g, i: (b, g, 0, i)),
            tile_t(MAX_SEL_BLOCKS),
            pl.BlockSpec((None, None, seq, 2 * LANES), lambda b, g, i: (b, g, 0, 0)),
            pl.BlockSpec((None, None, n_chunks, V_ROWS, TK_SEL), lambda b, g, i: (b, g, 0, 0, 0)),
            tile_t(GATE_ROWS), tile_t(HEADS_PER_GROUP * HEAD_DIM),
        ],
        out_specs=pl.BlockSpec((None, tq, HEADS_PER_GROUP * HEAD_DIM), lambda b, g, i: (b, i, g)),
        out_shape=jax.ShapeDtypeStruct((batch, seq, NSA_WIDTH), BF16),
        scratch_shapes=([pltpu.VMEM((2 * LANES, cols), BF16), pltpu.VMEM((1, cols), F32), pltpu.VMEM((V_ROWS, cols), F32)]
                        + [pltpu.VMEM((2, TK_SEL, tq), F32)] * HEADS_PER_GROUP
                        + [pltpu.VMEM((2, 1, tq), F32)] * HEADS_PER_GROUP),
        compiler_params=_params(3),
        name="select",
    )(qt, bias_t, ksel, vsel_t, gates_t, o_part_t)


def _s5_body(u_ref, bt_ref, pw_ref, a1_ref, ltri_ref, ct_ref, d_ref, wg_ref, bg_ref, o_ref, c_ref, x_ref, h_ref):
    @pl.when(pl.program_id(1) == 0)
    def _():
        c_ref[...] = jnp.zeros(c_ref.shape, F32)

    n = S5_COLS // S5_SPLIT
    u = u_ref[...]
    last = S5_CHUNK - 1
    ys = []
    for hf in range(S5_SPLIT):
        cs = slice(hf * n, (hf + 1) * n)
        p_re, p_im, q_re, q_im = pw_ref[0, :, cs], pw_ref[1, :, cs], pw_ref[2, :, cs], pw_ref[3, :, cs]
        a_re, a_im = a1_ref[0:1, cs], a1_ref[1:2, cs]
        width = S5_WIDTH // S5_SPLIT
        bu = _dot(u[:, hf * width:(hf + 1) * width].astype(BF16), bt_ref[hf])
        for j in range(u.shape[0] // S5_CHUNK):
            r = slice(j * S5_CHUNK, (j + 1) * S5_CHUNK)
            b_re, b_im = bu[r, 0:n], bu[r, n:2 * n]
            x_ref[r, 0:n] = (q_re * b_re - q_im * b_im).astype(BF16)
            x_ref[r, n:2 * n] = (q_re * b_im + q_im * b_re).astype(BF16)
        local = _dot(ltri_ref[...], x_ref[...])
        c_re, c_im = c_ref[0:1, cs], c_ref[1:2, cs]
        for j in range(u.shape[0] // S5_CHUNK):
            r = slice(j * S5_CHUNK, (j + 1) * S5_CHUNK)
            s_re, s_im = local[r, 0:n] + c_re, local[r, n:2 * n] + c_im
            h_ref[r, 0:n] = (p_re * s_re - p_im * s_im).astype(BF16)
            h_ref[r, n:2 * n] = (p_re * s_im + p_im * s_re).astype(BF16)
            e_re, e_im = s_re[last:last + 1], s_im[last:last + 1]
            l_re = p_re[last:last + 1] * e_re - p_im[last:last + 1] * e_im
            l_im = p_re[last:last + 1] * e_im + p_im[last:last + 1] * e_re
            c_re = a_re * l_re - a_im * l_im
            c_im = a_re * l_im + a_im * l_re
        c_ref[0:1, cs] = c_re
        c_ref[1:2, cs] = c_im
        ys.append(_dot(h_ref[...], ct_ref[hf]))
    y = _gelu_tanh(jnp.concatenate(ys, axis=1) + d_ref[...] * u)
    o_ref[...] = (y * jax.nn.sigmoid(_dot(y.astype(BF16), wg_ref[...]) + bg_ref[...])).astype(BF16)


def _s5(u, bt, powers, a1, ltri, ct, d, w_glu, b_glu):
    batch, seq, _ = u.shape
    ts = S5_TILE
    tile = pl.BlockSpec((None, ts, S5_WIDTH), lambda b, j: (b, j, 0))
    cols = 2 * S5_COLS // S5_SPLIT
    return pl.pallas_call(
        _s5_body,
        grid=(batch, seq // ts),
        in_specs=[tile] + [_resident()] * 8,
        out_specs=tile,
        out_shape=jax.ShapeDtypeStruct((batch, seq, S5_WIDTH), BF16),
        scratch_shapes=[pltpu.VMEM((2, S5_COLS), F32), pltpu.VMEM((ts, cols), BF16), pltpu.VMEM((ts, cols), BF16)],
        compiler_params=_params(2),
        name="s5",
    )(u, bt, powers, a1, ltri, ct, d, w_glu, b_glu)


def _ffn_out_body(*refs, final_norm):
    (h_ref, on_ref, os_ref, won_ref, wos_ref, g2_ref, w1_ref, w3_ref, w2_ref,
     gp_ref, wg_ref, p_ref, wp_ref) = refs[:13]
    o_ref = refs[-1]
    h = h_ref[...] + _dot(on_ref[...], won_ref[...]) + _dot(os_ref[...], wos_ref[...])
    h = _swiglu_half_step(h, g2_ref, w1_ref, w3_ref, w2_ref)
    gate = jax.nn.sigmoid(_dot(_rmsnorm(h, gp_ref[...]).astype(BF16), wg_ref[...]))
    h = h + gate * _dot(p_ref[...].astype(BF16), wp_ref[...])
    o_ref[...] = _rmsnorm(h, refs[13][...]) if final_norm else h


def _ffn_out(h, o_nsa, o_s5, wo_n, wo_s, g_ffn, w1, w3, w2, g_ple, w_gate, p, w_ple, g_final=None):
    t = h.shape[0]
    tm = min(TM_DENSE, t)
    row = lambda w: pl.BlockSpec((tm, w), lambda i: (i, 0))
    args = [h, o_nsa, o_s5, wo_n, wo_s, g_ffn, w1, w3, w2, g_ple, w_gate, p, w_ple]
    specs = [row(D_MODEL), row(NSA_WIDTH), row(S5_WIDTH)] + [_resident()] * 8 + [row(PLE_DIM), _resident()]
    if g_final is not None:
        args.append(g_final)
        specs.append(_resident())
    return pl.pallas_call(
        functools.partial(_ffn_out_body, final_norm=g_final is not None),
        grid=(t // tm,),
        in_specs=specs,
        out_specs=row(D_MODEL),
        out_shape=jax.ShapeDtypeStruct((t, D_MODEL), F32),
        compiler_params=_params(1),
        name="ffn_out",
    )(*args)


def _rearrange_w_in(w_in):
    g0 = NSA_WIDTH + 6 * KV_WIDTH
    per_group = 3 * HEADS_PER_GROUP
    pad = jnp.zeros((D_MODEL, LANES - per_group), w_in.dtype)
    gate_cols = []
    for g in range(N_KV_HEADS):
        gate_cols += [w_in[:, g0 + g * per_group:g0 + (g + 1) * per_group], pad]
    u0 = g0 + 3 * N_HEADS
    return jnp.concatenate([w_in[:, :g0], w_in[:, u0:]] + gate_cols, axis=1).astype(BF16)


def _rope_tables(seq):
    half = HEAD_DIM // 2
    inv = ROPE_THETA ** (-jnp.arange(half, dtype=F32) / half)
    ang = jnp.arange(seq, dtype=F32)[:, None] * inv[None, :]
    cos = jnp.tile(jnp.cos(ang), (1, LANES // half))
    sin = jnp.tile(jnp.concatenate([-jnp.sin(ang), jnp.sin(ang)], axis=1), (1, LANES // HEAD_DIM))
    return cos, sin


def _compress_weights(pe, w1, w2):
    w1 = w1.reshape(2, CMP_STRIDE, HEAD_DIM, CMP_HIDDEN)
    eye = jnp.eye(N_KV_HEADS, dtype=w1.dtype)
    big = jnp.einsum("sjdh,ge->sjgdeh", w1, eye).reshape(2, CMP_STRIDE * KV_WIDTH, N_KV_HEADS * CMP_HIDDEN)
    pe2 = jnp.broadcast_to(pe.reshape(2, CMP_STRIDE, 1, HEAD_DIM), (2, CMP_STRIDE, N_KV_HEADS, HEAD_DIM))
    return pe2.reshape(2, CMP_STRIDE * KV_WIDTH), big[0].astype(BF16), big[1].astype(BF16), w2.astype(BF16)


def _overlap_matrix(n_cmp_rows):
    c_start = np.arange(n_cmp_rows)[:, None] * CMP_STRIDE
    s_start = np.arange(MAX_SEL_BLOCKS)[None, :] * SEL_BLOCK
    ov = (c_start < s_start + SEL_BLOCK) & (c_start + CMP_LEN > s_start)
    return jnp.asarray(ov.T, dtype=BF16)


def _s5_tables(a_re, a_im, log_dt, b_re, b_im, c_re, c_im):
    dt = jnp.exp(log_dt)[:, None]
    lr, li = a_re * dt, a_im * dt
    mag = jnp.exp(lr)
    ab_re, ab_im = mag * jnp.cos(li), mag * jnp.sin(li)
    den = a_re * a_re + a_im * a_im
    nr, ni = ab_re - 1.0, ab_im
    bc_re = (nr * a_re + ni * a_im) / den
    bc_im = (ni * a_re - nr * a_im) / den
    bt_re = b_re * bc_re[..., None] - b_im * bc_im[..., None]
    bt_im = b_re * bc_im[..., None] + b_im * bc_re[..., None]
    gh = S5_GROUPS // S5_SPLIT
    eye = jnp.eye(gh, dtype=F32)
    split = lambda w: w.reshape((S5_SPLIT, gh) + w.shape[1:])
    blockdiag_in = lambda w: jnp.einsum("hgni,ge->hgien", split(w), eye).reshape(S5_SPLIT, gh * S5_GROUP, gh * S5_STATE)
    bt = jnp.concatenate([blockdiag_in(bt_re), blockdiag_in(bt_im)], axis=2).astype(BF16)
    blockdiag_out = lambda w: jnp.einsum("hgon,ge->hgneo", split(w), eye).reshape(S5_SPLIT, gh * S5_STATE, gh * S5_GROUP)
    ct = jnp.concatenate([blockdiag_out(c_re), -blockdiag_out(c_im)], axis=1).astype(BF16)
    k = jnp.arange(S5_CHUNK, dtype=F32)[:, None]
    lr, li = lr.reshape(1, S5_COLS), li.reshape(1, S5_COLS)
    powers = jnp.stack([jnp.exp(k * lr) * jnp.cos(k * li), jnp.exp(k * lr) * jnp.sin(k * li),
                        jnp.exp(-k * lr) * jnp.cos(k * li), -jnp.exp(-k * lr) * jnp.sin(k * li)])
    a1 = jnp.concatenate([ab_re.reshape(1, S5_COLS), ab_im.reshape(1, S5_COLS)], axis=0)
    ltri = jnp.asarray(np.kron(np.eye(S5_TILE // S5_CHUNK), np.tril(np.ones((S5_CHUNK, S5_CHUNK)))), dtype=BF16)
    return bt, powers, a1, ltri, ct


def kernel(x, p, norm_ffn1, ffn1_w1, ffn1_w3, ffn1_w2, norm_mix, w_in, cmp_pe_k, cmp_pe_v, cmp_wk1, cmp_wk2,
           cmp_wv1, cmp_wv2, s5_a_re, s5_a_im, s5_log_dt, s5_b_re, s5_b_im, s5_c_re, s5_c_im, s5_d, s5_w_glu,
           s5_b_glu, w_out, norm_ffn2, ffn2_w1, ffn2_w3, ffn2_w2, norm_ple, w_ple_gate, w_ple, norm_final):
    batch, seq, _ = x.shape
    depth = p.shape[0]
    t = batch * seq
    assert seq % TK_SEL == 0 and TK_SEL % TQ_SEL == 0 and seq % TQ_SEL == 0 and seq % S5_TILE == 0 and seq >= WINDOW + TQ and seq % (CMP_STRIDE * CMP_STEP) == 0
    assert seq // SEL_BLOCK <= MAX_SEL_BLOCKS and seq // SEL_BLOCK >= SEL_TOPK
    bf = lambda w: w.astype(BF16)
    row = lambda v: v.reshape(1, -1)

    cos, sin_signed = _rope_tables(seq)
    overlap_t = _overlap_matrix(seq // CMP_STRIDE)
    h = x.reshape(t, D_MODEL)
    for i in range(depth):
        h, qt, kc, vc, ksel, vsel_t, kwin, vwin_t, gates_t, u = _ffn_proj(
            h, row(norm_ffn1[i]), bf(ffn1_w1[i]), bf(ffn1_w3[i]), bf(ffn1_w2[i]),
            row(norm_mix[i]), _rearrange_w_in(w_in[i]), cos, sin_signed, batch, seq)
        pe_k2, wkt, wkb, wk2 = _compress_weights(cmp_pe_k[i], cmp_wk1[i], cmp_wk2[i])
        pe_v2, wvt, wvb, wv2 = _compress_weights(cmp_pe_v[i], cmp_wv1[i], cmp_wv2[i])
        kcmp, vcmp_t = _compress(kc, vc, pe_k2, pe_v2, wkt, wkb, wvt, wvb, wk2, wv2)
        o_part_t, bias_t = _cmpwin(qt, kcmp, vcmp_t, overlap_t, kwin, vwin_t, gates_t)
        o_nsa = _select(qt, bias_t, ksel, vsel_t, gates_t, o_part_t)

        bt, powers, a1, ltri, ct = _s5_tables(s5_a_re[i], s5_a_im[i], s5_log_dt[i], s5_b_re[i], s5_b_im[i],
                                              s5_c_re[i], s5_c_im[i])
        o_s5 = _s5(u, bt, powers, a1, ltri, ct, row(s5_d[i]), bf(s5_w_glu[i]), row(s5_b_glu[i]))

        h = _ffn_out(h, o_nsa.reshape(t, NSA_WIDTH), o_s5.reshape(t, S5_WIDTH),
                     bf(w_out[i][:NSA_WIDTH]), bf(w_out[i][NSA_WIDTH:]),
                     row(norm_ffn2[i]), bf(ffn2_w1[i]), bf(ffn2_w3[i]), bf(ffn2_w2[i]),
                     row(norm_ple[i]), bf(w_ple_gate[i]), p[i].reshape(t, PLE_DIM), bf(w_ple[i]),
                     row(norm_final) if i == depth - 1 else None)
    return h.reshape(batch, seq, D_MODEL)
```

```python
import functools

import jax
import jax.numpy as jnp
import numpy as np
from jax import lax
from jax.experimental import pallas as pl
from jax.experimental.pallas import tpu as pltpu

D_MODEL = 1024
PLE_DIM = 256
D_FF = 2816
N_HEADS = 8
N_KV_HEADS = 2
HEAD_DIM = 64
HEADS_PER_GROUP = N_HEADS // N_KV_HEADS
NSA_WIDTH = N_HEADS * HEAD_DIM
KV_WIDTH = N_KV_HEADS * HEAD_DIM
S5_WIDTH = D_MODEL - NSA_WIDTH
S5_GROUP = 16
S5_GROUPS = S5_WIDTH // S5_GROUP
S5_STATE = 64
S5_COLS = S5_GROUPS * S5_STATE
CMP_LEN = 32
CMP_STRIDE = 16
CMP_HIDDEN = 256
SEL_BLOCK = 64
SEL_TOPK = 16
WINDOW = 512
ROPE_THETA = 10000.0
RMS_EPS = 1e-6
NEG = -1e30
BIG = 1e9

LANES = 128
V_ROWS = HEAD_DIM + 16
GATE_ROWS = 16
MAX_SEL_BLOCKS = LANES
VMEM_LIMIT = 52 * 1024 * 1024

TM_DENSE = 512
FF_CHUNK = 256
TQ = 256
WIN_SUB = 128
CMP_STEP = 128
TQ_SEL = 512
TK_SEL = 512
S5_TILE = 256
S5_CHUNK = 64
S5_SPLIT = 2

F32 = jnp.float32
BF16 = jnp.bfloat16


def _dot(a, b):
    return jnp.dot(a, b, preferred_element_type=F32)


def _rmsnorm(x, g):
    ms = jnp.mean(x * x, axis=-1, keepdims=True)
    return x * lax.rsqrt(ms + RMS_EPS) * g


def _gelu_tanh(x):
    return 0.5 * x * (1.0 + jnp.tanh(np.sqrt(2.0 / np.pi).astype(np.float32) * (x + 0.044715 * (x * x * x))))


def _params(n_grid):
    return pltpu.CompilerParams(dimension_semantics=("arbitrary",) * n_grid, vmem_limit_bytes=VMEM_LIMIT)


def _resident():
    return pl.BlockSpec(memory_space=pltpu.VMEM)


def _swiglu_half_step(h, g_ref, w1_ref, w3_ref, w2_ref):
    xn = _rmsnorm(h, g_ref[...]).astype(BF16)
    acc = jnp.zeros(h.shape, F32)
    for j in range(D_FF // FF_CHUNK):
        sl = slice(j * FF_CHUNK, (j + 1) * FF_CHUNK)
        a = _dot(xn, w1_ref[:, sl])
        b = _dot(xn, w3_ref[:, sl])
        act = (a * jax.nn.sigmoid(a) * b).astype(BF16)
        acc = acc + _dot(act, w2_ref[sl, :])
    return h + 0.5 * acc


_C_Q = 0
_C_CMP = NSA_WIDTH
_C_SLC = _C_CMP + 2 * KV_WIDTH
_C_WIN = _C_SLC + 2 * KV_WIDTH
_C_U = _C_WIN + 2 * KV_WIDTH
_C_G = _C_U + S5_WIDTH
_C_END = _C_G + N_KV_HEADS * LANES


def _rope_pair(x, cos, sin_signed, first_half):
    fwd = pltpu.roll(x, HEAD_DIM // 2, 1)
    bwd = pltpu.roll(x, LANES - HEAD_DIM // 2, 1)
    return x * cos + jnp.where(first_half, bwd, fwd) * sin_signed


def _ffn_proj_body(x_ref, g1_ref, w1_ref, w3_ref, w2_ref, g_ref, w_ref, cos_ref, sin_ref,
                   h_ref, qt_ref, kc_ref, vc_ref, ks_ref, vst_ref, kw_ref, vwt_ref, gtt_ref, u_ref, kv_ref,
                   *, tiles_per_seq):
    tm = x_ref.shape[0]
    h = _swiglu_half_step(x_ref[...], g1_ref, w1_ref, w3_ref, w2_ref)
    h_ref[...] = h
    xn = _rmsnorm(h, g_ref[...]).astype(BF16)
    cos = cos_ref[...]
    sin = sin_ref[...]
    lane = lax.broadcasted_iota(jnp.int32, (tm, LANES), 1)
    first_half = (lane % HEAD_DIM) < (HEAD_DIM // 2)
    rope = lambda x: _rope_pair(x, cos, sin, first_half)

    scale = HEAD_DIM ** -0.5
    for j in range(NSA_WIDTH // 256):
        z = _dot(xn, w_ref[:, _C_Q + 256 * j:_C_Q + 256 * (j + 1)])
        for k in range(2):
            rt = (rope(z[:, LANES * k:LANES * (k + 1)]) * scale).T.astype(BF16)
            for e in range(2):
                qt_ref[4 * j + 2 * k + e] = rt[e * HEAD_DIM:(e + 1) * HEAD_DIM, :]

    z = _dot(xn, w_ref[:, _C_CMP:_C_CMP + 256])
    kv_ref[0] = rope(z[:, :LANES])
    kv_ref[1] = z[:, LANES:]
    for j in range(CMP_STRIDE):
        every = pl.ds(j, tm // CMP_STRIDE, stride=CMP_STRIDE)
        kc_ref[:, j * KV_WIDTH:(j + 1) * KV_WIDTH] = kv_ref[0, every, :]
        vc_ref[:, j * KV_WIDTH:(j + 1) * KV_WIDTH] = kv_ref[1, every, :]

    zeros = jnp.zeros((tm, HEAD_DIM), BF16)
    pos = (pl.program_id(0) % tiles_per_seq) * tm + lax.broadcasted_iota(jnp.int32, (tm, LANES), 0)
    onehot = jnp.where(lane == pos // SEL_BLOCK, 1.0, 0.0).astype(BF16)

    z = _dot(xn, w_ref[:, _C_SLC:_C_SLC + 256])
    k = rope(z[:, :LANES]).astype(BF16)
    vt = z[:, LANES:].T.astype(BF16)
    for g in range(N_KV_HEADS):
        gs = slice(g * HEAD_DIM, (g + 1) * HEAD_DIM)
        ks_ref[g, :, 0:LANES] = onehot
        ks_ref[g, :, LANES:LANES + HEAD_DIM] = k[:, gs]
        ks_ref[g, :, LANES + HEAD_DIM:] = zeros
        vst_ref[g, 0:HEAD_DIM, :] = vt[gs, :]
        vst_ref[g, HEAD_DIM:, :] = jnp.ones((V_ROWS - HEAD_DIM, tm), BF16)

    z = _dot(xn, w_ref[:, _C_WIN:_C_WIN + 256])
    k = rope(z[:, :LANES]).astype(BF16)
    vt = z[:, LANES:].T.astype(BF16)
    for g in range(N_KV_HEADS):
        gs = slice(g * HEAD_DIM, (g + 1) * HEAD_DIM)
        kw_ref[g] = k[:, gs]
        for c in range(tm // WIN_SUB):
            vwt_ref[g, c, 0:HEAD_DIM, :] = vt[gs, c * WIN_SUB:(c + 1) * WIN_SUB]
            vwt_ref[g, c, HEAD_DIM:, :] = jnp.ones((V_ROWS - HEAD_DIM, WIN_SUB), BF16)

    for j in range(S5_WIDTH // 256):
        u_ref[:, 256 * j:256 * (j + 1)] = _dot(xn, w_ref[:, _C_U + 256 * j:_C_U + 256 * (j + 1)])
    gates_t = jax.nn.sigmoid(_dot(xn, w_ref[:, _C_G:_C_END])).T
    for g in range(N_KV_HEADS):
        gtt_ref[g] = gates_t[g * LANES:g * LANES + GATE_ROWS, :]


def _ffn_proj(x, g_ffn, w1, w3, w2, g, w_in_r, cos, sin_signed, batch, seq):
    t = batch * seq
    tm = TK_SEL
    n = seq // tm
    bl = lambda w: pl.BlockSpec((None, tm, w), lambda i: (i // n, i % n, 0))
    bgl = lambda w: pl.BlockSpec((None, N_KV_HEADS, tm, w), lambda i: (i // n, 0, i % n, 0))
    tab = pl.BlockSpec((tm, LANES), lambda i: (i % n, 0))
    row = pl.BlockSpec((tm, D_MODEL), lambda i: (i, 0))
    strided = pl.BlockSpec((None, tm // CMP_STRIDE, CMP_STRIDE * KV_WIDTH), lambda i: (i // n, i % n, 0))
    sds = jax.ShapeDtypeStruct
    return pl.pallas_call(
        functools.partial(_ffn_proj_body, tiles_per_seq=n),
        grid=(t // tm,),
        in_specs=[row] + [_resident()] * 6 + [tab, tab],
        out_specs=[
            row,
            pl.BlockSpec((None, N_HEADS, HEAD_DIM, tm), lambda i: (i // n, 0, 0, i % n)),
            strided, strided,
            bgl(2 * LANES),
            pl.BlockSpec((None, N_KV_HEADS, None, V_ROWS, tm), lambda i: (i // n, 0, i % n, 0, 0)),
            bgl(HEAD_DIM),
            pl.BlockSpec((None, N_KV_HEADS, tm // WIN_SUB, V_ROWS, WIN_SUB), lambda i: (i // n, 0, i % n, 0, 0)),
            pl.BlockSpec((None, N_KV_HEADS, GATE_ROWS, tm), lambda i: (i // n, 0, 0, i % n)),
            bl(S5_WIDTH),
        ],
        out_shape=[
            sds((t, D_MODEL), F32),
            sds((batch, N_HEADS, HEAD_DIM, seq), BF16),
            sds((batch, seq // CMP_STRIDE, CMP_STRIDE * KV_WIDTH), F32),
            sds((batch, seq // CMP_STRIDE, CMP_STRIDE * KV_WIDTH), F32),
            sds((batch, N_KV_HEADS, seq, 2 * LANES), BF16), sds((batch, N_KV_HEADS, n, V_ROWS, tm), BF16),
            sds((batch, N_KV_HEADS, seq, HEAD_DIM), BF16), sds((batch, N_KV_HEADS, seq // WIN_SUB, V_ROWS, WIN_SUB), BF16),
            sds((batch, N_KV_HEADS, GATE_ROWS, seq), F32), sds((batch, seq, S5_WIDTH), F32),
        ],
        scratch_shapes=[pltpu.VMEM((2, tm, KV_WIDTH), F32)],
        compiler_params=_params(1),
        name="ffn_proj",
    )(x, g_ffn, w1, w3, w2, g, w_in_r, cos, sin_signed)


def _compress_body(kc_ref, vc_ref, pek_ref, pev_ref, wkt_ref, wkb_ref, wvt_ref, wvb_ref, wk2_ref, wv2_ref,
                   ko_ref, vot_ref, v_ref):
    nc = kc_ref.shape[0]

    def hidden(x, pe_ref, wt_ref, wb_ref):
        top = _dot((x + pe_ref[0:1, :]).astype(BF16), wt_ref[...])
        bot = _dot((x + pe_ref[1:2, :]).astype(BF16), wb_ref[...])
        pre = top + pltpu.roll(bot, nc - 1, 0)
        return _gelu_tanh(pre).astype(BF16)

    hk = hidden(kc_ref[...], pek_ref, wkt_ref, wkb_ref)
    hv = hidden(vc_ref[...], pev_ref, wvt_ref, wvb_ref)
    for g in range(N_KV_HEADS):
        gs = slice(g * CMP_HIDDEN, (g + 1) * CMP_HIDDEN)
        ko_ref[g] = _dot(hk[:, gs], wk2_ref[...]).astype(BF16)
        v_ref[...] = _dot(hv[:, gs], wv2_ref[...])
        vot_ref[g] = v_ref[...].T.astype(BF16)


def _compress(kc, vc, pe_k2, pe_v2, wkt, wkb, wvt, wvb, wk2, wv2):
    batch, nc, width = kc.shape
    x_spec = pl.BlockSpec((None, nc, width), lambda b: (b, 0, 0))
    sds = jax.ShapeDtypeStruct
    return pl.pallas_call(
        _compress_body,
        grid=(batch,),
        in_specs=[x_spec, x_spec] + [_resident()] * 8,
        out_specs=[pl.BlockSpec((None, N_KV_HEADS, nc, HEAD_DIM), lambda b: (b, 0, 0, 0)),
                   pl.BlockSpec((None, N_KV_HEADS, HEAD_DIM, nc), lambda b: (b, 0, 0, 0))],
        out_shape=[sds((batch, N_KV_HEADS, nc, HEAD_DIM), BF16), sds((batch, N_KV_HEADS, HEAD_DIM, nc), BF16)],
        scratch_shapes=[pltpu.VMEM((nc, HEAD_DIM), F32)],
        compiler_params=_params(1),
        name="compress",
    )(kc, vc, pe_k2, pe_v2, wkt, wkb, wvt, wvb, wk2, wv2)


def _cmpwin_body(qt_ref, kc_ref, vct_ref, ovt_ref, kw_ref, vwt_ref, gtt_ref, opt_ref, bias_ref,
                 s_ref, ocmp_ref, imp_ref):
    tq = qt_ref.shape[2]
    cols = HEADS_PER_GROUP * tq
    nc = kc_ref.shape[0]
    i = pl.program_id(2)
    q_t = jnp.concatenate([qt_ref[h] for h in range(HEADS_PER_GROUP)], axis=1)
    t_lane = i * tq + lax.broadcasted_iota(jnp.int32, (1, cols), 1) % tq

    def compressed(n):
        s = _dot(kc_ref[0:n, :], q_t)
        c_end = lax.broadcasted_iota(jnp.int32, (n, cols), 0) * CMP_STRIDE + (CMP_LEN - 1)
        s = jnp.where(c_end <= t_lane, s, NEG)
        e = jnp.exp(s - jnp.max(s, axis=0, keepdims=True))
        den = jnp.maximum(jnp.sum(e, axis=0, keepdims=True), 1e-30)
        p = e * jnp.where(t_lane >= CMP_LEN - 1, 1.0 / den, 0.0)
        ocmp_ref[...] = _dot(vct_ref[:, 0:n], p.astype(BF16))
        psum = p[:, 0:tq]
        for h in range(1, HEADS_PER_GROUP):
            psum = psum + p[:, h * tq:(h + 1) * tq]
        p_hi = psum.astype(BF16)
        p_lo = (psum - p_hi.astype(F32)).astype(BF16)
        imp_ref[...] = _dot(ovt_ref[:, 0:n], p_hi) + _dot(ovt_ref[:, 0:n], p_lo)

    finished = (i * tq + tq - CMP_LEN) // CMP_STRIDE + 1
    n_steps = nc // CMP_STEP
    for k in range(1, n_steps + 1):
        in_step = finished > (k - 1) * CMP_STEP
        if k < n_steps:
            in_step = jnp.logical_and(in_step, finished <= k * CMP_STEP)
        pl.when(in_step)(functools.partial(compressed, k * CMP_STEP))
    o_cmp = ocmp_ref[...]
    imp = imp_ref[...]

    blk = lax.broadcasted_iota(jnp.int32, (MAX_SEL_BLOCKS, tq), 0)
    behind = t_lane[:, 0:tq] // SEL_BLOCK - blk
    forced = jnp.where(blk == 0, 1.0, jnp.where(behind == 0, 1.0, jnp.where(behind == 1, 1.0, 0.0)))

    def top_k(score, chosen, rounds):
        for _ in range(rounds):
            best = jnp.max(score, axis=0, keepdims=True)
            first = jnp.min(jnp.where(score == best, blk, MAX_SEL_BLOCKS), axis=0, keepdims=True)
            hit = blk == first
            chosen = jnp.where(hit, 1.0, chosen)
            score = jnp.where(hit, -jnp.inf, score)
        bias_ref[...] = jnp.where(chosen > 0.0, jnp.where(behind >= 0, 0.0, NEG), NEG).astype(BF16)

    n_forced = 3

    @pl.when(i * tq >= (n_forced - 1) * SEL_BLOCK)
    def _():
        top_k(jnp.where(forced > 0.0, -jnp.inf, jnp.where(behind >= 0, imp, -BIG)), forced, SEL_TOPK - n_forced)

    @pl.when(i * tq < (n_forced - 1) * SEL_BLOCK)
    def _():
        score = jnp.where(forced > 0.0, BIG, imp)
        top_k(jnp.where(behind >= 0, score, -BIG), jnp.zeros((MAX_SEL_BLOCKS, tq), F32), SEL_TOPK)

    span = WINDOW + tq
    start = pl.multiple_of(jnp.maximum(i * tq - WINDOW, 0), tq)
    s_ref[...] = _dot(kw_ref[pl.ds(start, span), :], q_t)

    def band(rows):
        kpos = start + rows.start + lax.broadcasted_iota(jnp.int32, (rows.stop - rows.start, cols), 0)
        dist = lax.bitcast_convert_type(t_lane - kpos, jnp.uint32)
        s_ref[rows, :] = jnp.where(dist < WINDOW, s_ref[rows, :], NEG)

    @pl.when(i * tq >= WINDOW)
    def _():
        band(slice(0, tq))
        band(slice(span - tq, span))

    @pl.when(i * tq < WINDOW)
    def _():
        band(slice(0, span))

    s = s_ref[...]
    e = jnp.exp(s - jnp.max(s, axis=0, keepdims=True)).astype(BF16)
    c0 = start // WIN_SUB
    vw_t = jnp.concatenate([vwt_ref[c0 + j] for j in range(span // WIN_SUB)], axis=1)
    acc = _dot(vw_t, e)
    o_win = acc[0:HEAD_DIM, :] / acc[HEAD_DIM:HEAD_DIM + 1, :]

    gates = gtt_ref[...]
    for h in range(HEADS_PER_GROUP):
        hs = slice(h * tq, (h + 1) * tq)
        opt_ref[h * HEAD_DIM:(h + 1) * HEAD_DIM, :] = (gates[3 * h:3 * h + 1, :] * o_cmp[:, hs]
                                                       + gates[3 * h + 2:3 * h + 3, :] * o_win[:, hs])


def _cmpwin(qt, kcmp, vcmp_t, overlap_t, kwin, vwin_t, gates_t):
    batch, _, _, seq = qt.shape
    nc = kcmp.shape[2]
    tq = TQ
    per_bg = lambda *shape: pl.BlockSpec((None, None) + shape, lambda b, g, i: (b, g) + (0,) * len(shape))
    sds = jax.ShapeDtypeStruct
    return pl.pallas_call(
        _cmpwin_body,
        grid=(batch, N_KV_HEADS, seq // tq),
        in_specs=[
            pl.BlockSpec((None, HEADS_PER_GROUP, HEAD_DIM, tq), lambda b, g, i: (b, g, 0, i)),
            per_bg(nc, HEAD_DIM), per_bg(HEAD_DIM, nc), _resident(),
            per_bg(seq, HEAD_DIM), per_bg(seq // WIN_SUB, V_ROWS, WIN_SUB),
            pl.BlockSpec((None, None, GATE_ROWS, tq), lambda b, g, i: (b, g, 0, i)),
        ],
        out_specs=[pl.BlockSpec((None, None, HEADS_PER_GROUP * HEAD_DIM, tq), lambda b, g, i: (b, g, 0, i)),
                   pl.BlockSpec((None, None, MAX_SEL_BLOCKS, tq), lambda b, g, i: (b, g, 0, i))],
        out_shape=[sds((batch, N_KV_HEADS, HEADS_PER_GROUP * HEAD_DIM, seq), F32),
                   sds((batch, N_KV_HEADS, MAX_SEL_BLOCKS, seq), BF16)],
        scratch_shapes=[pltpu.VMEM((WINDOW + tq, HEADS_PER_GROUP * tq), F32),
                        pltpu.VMEM((HEAD_DIM, HEADS_PER_GROUP * tq), F32), pltpu.VMEM((MAX_SEL_BLOCKS, tq), F32)],
        compiler_params=_params(3),
        name="cmpwin",
    )(qt, kcmp, vcmp_t, overlap_t, kwin, vwin_t, gates_t)


def _select_body(qt_ref, bias_ref, ks_ref, vst_ref, gtt_ref, opt_ref, o_ref, qa_ref, m_ref, acc_ref, *head_refs):
    tq = qt_ref.shape[2]
    cols = HEADS_PER_GROUP * tq
    tk = TK_SEL
    i = pl.program_id(2)

    bias = bias_ref[...]
    for h in range(HEADS_PER_GROUP):
        qa_ref[0:LANES, h * tq:(h + 1) * tq] = bias
        qa_ref[LANES:LANES + HEAD_DIM, h * tq:(h + 1) * tq] = qt_ref[h]
    qa_ref[LANES + HEAD_DIM:, :] = jnp.zeros((LANES - HEAD_DIM, cols), BF16)
    m_ref[...] = jnp.full(m_ref.shape, NEG, F32)
    acc_ref[...] = jnp.zeros(acc_ref.shape, F32)

    s_refs, mx_refs = head_refs[:HEADS_PER_GROUP], head_refs[HEADS_PER_GROUP:]

    def scores(c, causal, h):
        hs = slice(h * tq, (h + 1) * tq)
        off = pl.multiple_of(c * tk, tk)
        s = _dot(ks_ref[pl.ds(off, tk), :], qa_ref[:, hs])
        if causal:
            kpos = off + lax.broadcasted_iota(jnp.int32, s.shape, 0)
            t = i * tq + lax.broadcasted_iota(jnp.int32, s.shape, 1) % tq
            s = jnp.where(kpos <= t, s, NEG)
        s_refs[h][c % 2] = s
        mx_refs[h][c % 2] = jnp.max(s, axis=0, keepdims=True)

    def accumulate(c, h):
        hs = slice(h * tq, (h + 1) * tq)
        m_old = m_ref[:, hs]
        m_new = jnp.maximum(m_old, mx_refs[h][c % 2])
        p = jnp.exp(s_refs[h][c % 2] - m_new).astype(BF16)
        acc_ref[:, hs] = jnp.exp(m_old - m_new) * acc_ref[:, hs] + _dot(vst_ref[c], p)
        m_ref[:, hs] = m_new

    heads = range(HEADS_PER_GROUP)

    n_full = (i * tq) // tk

    def pair(c, causal):
        scores(c + 1, causal, HEADS_PER_GROUP - 1)
        for h in heads:
            accumulate(c, h)
            if h > 0:
                scores(c + 1, causal, h - 1)

    @pl.when(n_full > 0)
    def _():
        for h in heads:
            scores(0, False, h)

    def two_pairs(k, carry):
        pair(2 * k, False)
        pair(2 * k + 1, False)
        return carry

    n_plain = jnp.maximum(n_full - 1, 0)
    lax.fori_loop(0, n_plain // 2, two_pairs, 0)

    @pl.when(n_plain % 2 == 1)
    def _():
        pair(n_plain - 1, False)

    @pl.when(n_full > 0)
    def _():
        pair(n_full - 1, True)

    @pl.when(n_full == 0)
    def _():
        for h in heads:
            scores(n_full, True, h)

    for h in heads:
        accumulate(n_full, h)

    acc = acc_ref[...]
    o_slc = acc[0:HEAD_DIM, :] / acc[HEAD_DIM:HEAD_DIM + 1, :]
    gates = gtt_ref[...]
    o_t = jnp.concatenate([gates[3 * h + 1:3 * h + 2, :] * o_slc[:, h * tq:(h + 1) * tq]
                           for h in range(HEADS_PER_GROUP)], axis=0)
    o_ref[...] = (opt_ref[...] + o_t).T.astype(BF16)


def _select(qt, bias_t, ksel, vsel_t, gates_t, o_part_t):
    batch, _, _, seq = qt.shape
    tq = TQ_SEL
    cols = HEADS_PER_GROUP * tq
    n_chunks = seq // TK_SEL
    tile_t = lambda r: pl.BlockSpec((None, None, r, tq), lambda b, g, i: (b, g, 0, i))
    return pl.pallas_call(
        _select_body,
        grid=(batch, N_KV_HEADS, seq // tq),
        in_specs=[
            pl.BlockSpec((None, HEADS_PER_GROUP, HEAD_DIM, tq), lambda b, g, i: (b, g, 0, i)),
            tile_t(MAX_SEL_BLOCKS),
            pl.BlockSpec((None, None, seq, 2 * LANES), lambda b, g, i: (b, g, 0, 0)),
            pl.BlockSpec((None, None, n_chunks, V_ROWS, TK_SEL), lambda b, g, i: (b, g, 0, 0, 0)),
            tile_t(GATE_ROWS), tile_t(HEADS_PER_GROUP * HEAD_DIM),
        ],
        out_specs=pl.BlockSpec((None, tq, HEADS_PER_GROUP * HEAD_DIM), lambda b, g, i: (b, i, g)),
        out_shape=jax.ShapeDtypeStruct((batch, seq, NSA_WIDTH), BF16),
        scratch_shapes=([pltpu.VMEM((2 * LANES, cols), BF16), pltpu.VMEM((1, cols), F32), pltpu.VMEM((V_ROWS, cols), F32)]
                        + [pltpu.VMEM((2, TK_SEL, tq), F32)] * HEADS_PER_GROUP
                        + [pltpu.VMEM((2, 1, tq), F32)] * HEADS_PER_GROUP),
        compiler_params=_params(3),
        name="select",
    )(qt, bias_t, ksel, vsel_t, gates_t, o_part_t)


def _s5_body(u_ref, bt_ref, pw_ref, pwb_ref, a1_ref, ltri_ref, ct_ref, d_ref, wg_ref, bg_ref, o_ref, c_ref, x_ref, h_ref):
    @pl.when(pl.program_id(1) == 0)
    def _():
        c_ref[...] = jnp.zeros(c_ref.shape, F32)

    n = S5_COLS // S5_SPLIT
    u = u_ref[...]
    last = S5_CHUNK - 1
    ys = []
    for hf in range(S5_SPLIT):
        cs = slice(hf * n, (hf + 1) * n)
        p_re, p_im, q_re, q_im = pwb_ref[0, :, cs], pwb_ref[1, :, cs], pwb_ref[2, :, cs], pwb_ref[3, :, cs]
        pl_re, pl_im = pw_ref[0, last:last + 1, cs], pw_ref[1, last:last + 1, cs]
        a_re, a_im = a1_ref[0:1, cs], a1_ref[1:2, cs]
        width = S5_WIDTH // S5_SPLIT
        bu = _dot(u[:, hf * width:(hf + 1) * width].astype(BF16), bt_ref[hf])
        for j in range(u.shape[0] // S5_CHUNK):
            r = slice(j * S5_CHUNK, (j + 1) * S5_CHUNK)
            b_re, b_im = bu[r, 0:n].astype(BF16), bu[r, n:2 * n].astype(BF16)
            x_ref[r, 0:n] = q_re * b_re - q_im * b_im
            x_ref[r, n:2 * n] = q_re * b_im + q_im * b_re
        local = _dot(ltri_ref[...], x_ref[...])
        c_re, c_im = c_ref[0:1, cs], c_ref[1:2, cs]
        for j in range(u.shape[0] // S5_CHUNK):
            r = slice(j * S5_CHUNK, (j + 1) * S5_CHUNK)
            s_re, s_im = local[r, 0:n] + c_re, local[r, n:2 * n] + c_im
            t_re, t_im = s_re.astype(BF16), s_im.astype(BF16)
            h_ref[r, 0:n] = p_re * t_re - p_im * t_im
            h_ref[r, n:2 * n] = p_re * t_im + p_im * t_re
            e_re, e_im = s_re[last:last + 1], s_im[last:last + 1]
            l_re = pl_re * e_re - pl_im * e_im
            l_im = pl_re * e_im + pl_im * e_re
            c_re = a_re * l_re - a_im * l_im
            c_im = a_re * l_im + a_im * l_re
        c_ref[0:1, cs] = c_re
        c_ref[1:2, cs] = c_im
        ys.append(_dot(h_ref[...], ct_ref[hf]))
    y = _gelu_tanh(jnp.concatenate(ys, axis=1) + d_ref[...] * u)
    o_ref[...] = (y * jax.nn.sigmoid(_dot(y.astype(BF16), wg_ref[...]) + bg_ref[...])).astype(BF16)


def _s5(u, bt, powers, a1, ltri, ct, d, w_glu, b_glu):
    batch, seq, _ = u.shape
    ts = S5_TILE
    tile = pl.BlockSpec((None, ts, S5_WIDTH), lambda b, j: (b, j, 0))
    cols = 2 * S5_COLS // S5_SPLIT
    return pl.pallas_call(
        _s5_body,
        grid=(batch, seq // ts),
        in_specs=[tile] + [_resident()] * 9,
        out_specs=tile,
        out_shape=jax.ShapeDtypeStruct((batch, seq, S5_WIDTH), BF16),
        scratch_shapes=[pltpu.VMEM((2, S5_COLS), F32), pltpu.VMEM((ts, cols), BF16), pltpu.VMEM((ts, cols), BF16)],
        compiler_params=_params(2),
        name="s5",
    )(u, bt, powers, powers.astype(BF16), a1, ltri, ct, d, w_glu, b_glu)


def _ffn_out_body(*refs, final_norm):
    (h_ref, on_ref, os_ref, won_ref, wos_ref, g2_ref, w1_ref, w3_ref, w2_ref,
     gp_ref, wg_ref, p_ref, wp_ref) = refs[:13]
    o_ref = refs[-1]
    h = h_ref[...] + _dot(on_ref[...], won_ref[...]) + _dot(os_ref[...], wos_ref[...])
    h = _swiglu_half_step(h, g2_ref, w1_ref, w3_ref, w2_ref)
    gate = jax.nn.sigmoid(_dot(_rmsnorm(h, gp_ref[...]).astype(BF16), wg_ref[...]))
    h = h + gate * _dot(p_ref[...].astype(BF16), wp_ref[...])
    o_ref[...] = _rmsnorm(h, refs[13][...]) if final_norm else h


def _ffn_out(h, o_nsa, o_s5, wo_n, wo_s, g_ffn, w1, w3, w2, g_ple, w_gate, p, w_ple, g_final=None):
    t = h.shape[0]
    tm = min(TM_DENSE, t)
    row = lambda w: pl.BlockSpec((tm, w), lambda i: (i, 0))
    args = [h, o_nsa, o_s5, wo_n, wo_s, g_ffn, w1, w3, w2, g_ple, w_gate, p, w_ple]
    specs = [row(D_MODEL), row(NSA_WIDTH), row(S5_WIDTH)] + [_resident()] * 8 + [row(PLE_DIM), _resident()]
    if g_final is not None:
        args.append(g_final)
        specs.append(_resident())
    return pl.pallas_call(
        functools.partial(_ffn_out_body, final_norm=g_final is not None),
        grid=(t // tm,),
        in_specs=specs,
        out_specs=row(D_MODEL),
        out_shape=jax.ShapeDtypeStruct((t, D_MODEL), F32),
        compiler_params=_params(1),
        name="ffn_out",
    )(*args)


def _rearrange_w_in(w_in):
    g0 = NSA_WIDTH + 6 * KV_WIDTH
    per_group = 3 * HEADS_PER_GROUP
    pad = jnp.zeros((D_MODEL, LANES - per_group), w_in.dtype)
    gate_cols = []
    for g in range(N_KV_HEADS):
        gate_cols += [w_in[:, g0 + g * per_group:g0 + (g + 1) * per_group], pad]
    u0 = g0 + 3 * N_HEADS
    return jnp.concatenate([w_in[:, :g0], w_in[:, u0:]] + gate_cols, axis=1).astype(BF16)


def _rope_tables(seq):
    half = HEAD_DIM // 2
    inv = ROPE_THETA ** (-jnp.arange(half, dtype=F32) / half)
    ang = jnp.arange(seq, dtype=F32)[:, None] * inv[None, :]
    cos = jnp.tile(jnp.cos(ang), (1, LANES // half))
    sin = jnp.tile(jnp.concatenate([-jnp.sin(ang), jnp.sin(ang)], axis=1), (1, LANES // HEAD_DIM))
    return cos, sin


def _compress_weights(pe, w1, w2):
    w1 = w1.reshape(2, CMP_STRIDE, HEAD_DIM, CMP_HIDDEN)
    eye = jnp.eye(N_KV_HEADS, dtype=w1.dtype)
    big = jnp.einsum("sjdh,ge->sjgdeh", w1, eye).reshape(2, CMP_STRIDE * KV_WIDTH, N_KV_HEADS * CMP_HIDDEN)
    pe2 = jnp.broadcast_to(pe.reshape(2, CMP_STRIDE, 1, HEAD_DIM), (2, CMP_STRIDE, N_KV_HEADS, HEAD_DIM))
    return pe2.reshape(2, CMP_STRIDE * KV_WIDTH), big[0].astype(BF16), big[1].astype(BF16), w2.astype(BF16)


def _overlap_matrix(n_cmp_rows):
    c_start = np.arange(n_cmp_rows)[:, None] * CMP_STRIDE
    s_start = np.arange(MAX_SEL_BLOCKS)[None, :] * SEL_BLOCK
    ov = (c_start < s_start + SEL_BLOCK) & (c_start + CMP_LEN > s_start)
    return jnp.asarray(ov.T, dtype=BF16)


def _s5_tables(a_re, a_im, log_dt, b_re, b_im, c_re, c_im):
    dt = jnp.exp(log_dt)[:, None]
    lr, li = a_re * dt, a_im * dt
    mag = jnp.exp(lr)
    ab_re, ab_im = mag * jnp.cos(li), mag * jnp.sin(li)
    den = a_re * a_re + a_im * a_im
    nr, ni = ab_re - 1.0, ab_im
    bc_re = (nr * a_re + ni * a_im) / den
    bc_im = (ni * a_re - nr * a_im) / den
    bt_re = b_re * bc_re[..., None] - b_im * bc_im[..., None]
    bt_im = b_re * bc_im[..., None] + b_im * bc_re[..., None]
    gh = S5_GROUPS // S5_SPLIT
    eye = jnp.eye(gh, dtype=F32)
    split = lambda w: w.reshape((S5_SPLIT, gh) + w.shape[1:])
    blockdiag_in = lambda w: jnp.einsum("hgni,ge->hgien", split(w), eye).reshape(S5_SPLIT, gh * S5_GROUP, gh * S5_STATE)
    bt = jnp.concatenate([blockdiag_in(bt_re), blockdiag_in(bt_im)], axis=2).astype(BF16)
    blockdiag_out = lambda w: jnp.einsum("hgon,ge->hgneo", split(w), eye).reshape(S5_SPLIT, gh * S5_STATE, gh * S5_GROUP)
    ct = jnp.concatenate([blockdiag_out(c_re), -blockdiag_out(c_im)], axis=1).astype(BF16)
    k = jnp.arange(S5_CHUNK, dtype=F32)[:, None]
    lr, li = lr.reshape(1, S5_COLS), li.reshape(1, S5_COLS)
    powers = jnp.stack([jnp.exp(k * lr) * jnp.cos(k * li), jnp.exp(k * lr) * jnp.sin(k * li),
                        jnp.exp(-k * lr) * jnp.cos(k * li), -jnp.exp(-k * lr) * jnp.sin(k * li)])
    a1 = jnp.concatenate([ab_re.reshape(1, S5_COLS), ab_im.reshape(1, S5_COLS)], axis=0)
    ltri = jnp.asarray(np.kron(np.eye(S5_TILE // S5_CHUNK), np.tril(np.ones((S5_CHUNK, S5_CHUNK)))), dtype=BF16)
    return bt, powers, a1, ltri, ct


def kernel(x, p, norm_ffn1, ffn1_w1, ffn1_w3, ffn1_w2, norm_mix, w_in, cmp_pe_k, cmp_pe_v, cmp_wk1, cmp_wk2,
           cmp_wv1, cmp_wv2, s5_a_re, s5_a_im, s5_log_dt, s5_b_re, s5_b_im, s5_c_re, s5_c_im, s5_d, s5_w_glu,
           s5_b_glu, w_out, norm_ffn2, ffn2_w1, ffn2_w3, ffn2_w2, norm_ple, w_ple_gate, w_ple, norm_final):
    batch, seq, _ = x.shape
    depth = p.shape[0]
    t = batch * seq
    assert seq % TK_SEL == 0 and TK_SEL % TQ_SEL == 0 and seq % TQ_SEL == 0 and seq % S5_TILE == 0 and seq >= WINDOW + TQ and seq % (CMP_STRIDE * CMP_STEP) == 0
    assert seq // SEL_BLOCK <= MAX_SEL_BLOCKS and seq // SEL_BLOCK >= SEL_TOPK
    bf = lambda w: w.astype(BF16)
    row = lambda v: v.reshape(1, -1)

    cos, sin_signed = _rope_tables(seq)
    overlap_t = _overlap_matrix(seq // CMP_STRIDE)
    h = x.reshape(t, D_MODEL)
    for i in range(depth):
        h, qt, kc, vc, ksel, vsel_t, kwin, vwin_t, gates_t, u = _ffn_proj(
            h, row(norm_ffn1[i]), bf(ffn1_w1[i]), bf(ffn1_w3[i]), bf(ffn1_w2[i]),
            row(norm_mix[i]), _rearrange_w_in(w_in[i]), cos, sin_signed, batch, seq)
        pe_k2, wkt, wkb, wk2 = _compress_weights(cmp_pe_k[i], cmp_wk1[i], cmp_wk2[i])
        pe_v2, wvt, wvb, wv2 = _compress_weights(cmp_pe_v[i], cmp_wv1[i], cmp_wv2[i])
        kcmp, vcmp_t = _compress(kc, vc, pe_k2, pe_v2, wkt, wkb, wvt, wvb, wk2, wv2)
        o_part_t, bias_t = _cmpwin(qt, kcmp, vcmp_t, overlap_t, kwin, vwin_t, gates_t)
        o_nsa = _select(qt, bias_t, ksel, vsel_t, gates_t, o_part_t)

        bt, powers, a1, ltri, ct = _s5_tables(s5_a_re[i], s5_a_im[i], s5_log_dt[i], s5_b_re[i], s5_b_im[i],
                                              s5_c_re[i], s5_c_im[i])
        o_s5 = _s5(u, bt, powers, a1, ltri, ct, row(s5_d[i]), bf(s5_w_glu[i]), row(s5_b_glu[i]))

        h = _ffn_out(h, o_nsa.reshape(t, NSA_WIDTH), o_s5.reshape(t, S5_WIDTH),
                     bf(w_out[i][:NSA_WIDTH]), bf(w_out[i][NSA_WIDTH:]),
                     row(norm_ffn2[i]), bf(ffn2_w1[i]), bf(ffn2_w3[i]), bf(ffn2_w2[i]),
                     row(norm_ple[i]), bf(w_ple_gate[i]), p[i].reshape(t, PLE_DIM), bf(w_ple[i]),
                     row(norm_final) if i == depth - 1 else None)
    return h.reshape(batch, seq, D_MODEL)
```

```python
import functools

import jax
import jax.numpy as jnp
import numpy as np
from jax import lax
from jax.experimental import pallas as pl
from jax.experimental.pallas import tpu as pltpu

D_MODEL = 1024
PLE_DIM = 256
D_FF = 2816
N_HEADS = 8
N_KV_HEADS = 2
HEAD_DIM = 64
HEADS_PER_GROUP = N_HEADS // N_KV_HEADS
NSA_WIDTH = N_HEADS * HEAD_DIM
KV_WIDTH = N_KV_HEADS * HEAD_DIM
S5_WIDTH = D_MODEL - NSA_WIDTH
S5_GROUP = 16
S5_GROUPS = S5_WIDTH // S5_GROUP
S5_STATE = 64
S5_COLS = S5_GROUPS * S5_STATE
CMP_LEN = 32
CMP_STRIDE = 16
CMP_HIDDEN = 256
SEL_BLOCK = 64
SEL_TOPK = 16
WINDOW = 512
ROPE_THETA = 10000.0
RMS_EPS = 1e-6
NEG = -1e30
BIG = 1e9

LANES = 128
V_ROWS = HEAD_DIM + 16
GATE_ROWS = 16
MAX_SEL_BLOCKS = LANES
VMEM_LIMIT = 52 * 1024 * 1024

TM_DENSE = 512
FF_CHUNK = 256
TQ = 256
WIN_SUB = 128
CMP_STEP = 128
TQ_SEL = 512
TK_SEL = 512
S5_TILE = 256
S5_CHUNK = 64
S5_SPLIT = 2

F32 = jnp.float32
BF16 = jnp.bfloat16


def _dot(a, b):
    return jnp.dot(a, b, preferred_element_type=F32)


def _rmsnorm(x, g):
    ms = jnp.mean(x * x, axis=-1, keepdims=True)
    return x * lax.rsqrt(ms + RMS_EPS) * g


def _gelu_tanh(x):
    return 0.5 * x * (1.0 + jnp.tanh(np.sqrt(2.0 / np.pi).astype(np.float32) * (x + 0.044715 * (x * x * x))))


def _params(n_grid):
    return pltpu.CompilerParams(dimension_semantics=("arbitrary",) * n_grid, vmem_limit_bytes=VMEM_LIMIT)


def _resident():
    return pl.BlockSpec(memory_space=pltpu.VMEM)


def _swiglu_half_step(h, g_ref, w1_ref, w3_ref, w2_ref):
    xn = _rmsnorm(h, g_ref[...]).astype(BF16)
    acc = jnp.zeros(h.shape, F32)
    for j in range(D_FF // FF_CHUNK):
        sl = slice(j * FF_CHUNK, (j + 1) * FF_CHUNK)
        a = _dot(xn, w1_ref[:, sl])
        b = _dot(xn, w3_ref[:, sl])
        act = (a * jax.nn.sigmoid(a) * b).astype(BF16)
        acc = acc + _dot(act, w2_ref[sl, :])
    return h + 0.5 * acc


_C_Q = 0
_C_CMP = NSA_WIDTH
_C_SLC = _C_CMP + 2 * KV_WIDTH
_C_WIN = _C_SLC + 2 * KV_WIDTH
_C_U = _C_WIN + 2 * KV_WIDTH
_C_G = _C_U + S5_WIDTH
_C_END = _C_G + N_KV_HEADS * LANES


def _rope_pair(x, cos, sin_signed, first_half):
    fwd = pltpu.roll(x, HEAD_DIM // 2, 1)
    bwd = pltpu.roll(x, LANES - HEAD_DIM // 2, 1)
    return x * cos + jnp.where(first_half, bwd, fwd) * sin_signed


def _ffn_proj_body(x_ref, g1_ref, w1_ref, w3_ref, w2_ref, g_ref, w_ref, cos_ref, sin_ref,
                   h_ref, qt_ref, kc_ref, vc_ref, ks_ref, vst_ref, kw_ref, vwt_ref, gtt_ref, u_ref, kv_ref,
                   *, tiles_per_seq):
    tm = x_ref.shape[0]
    h = _swiglu_half_step(x_ref[...], g1_ref, w1_ref, w3_ref, w2_ref)
    h_ref[...] = h
    xn = _rmsnorm(h, g_ref[...]).astype(BF16)
    cos = cos_ref[...]
    sin = sin_ref[...]
    lane = lax.broadcasted_iota(jnp.int32, (tm, LANES), 1)
    first_half = (lane % HEAD_DIM) < (HEAD_DIM // 2)
    rope = lambda x: _rope_pair(x, cos, sin, first_half)

    scale = HEAD_DIM ** -0.5
    for j in range(NSA_WIDTH // 256):
        z = _dot(xn, w_ref[:, _C_Q + 256 * j:_C_Q + 256 * (j + 1)])
        for k in range(2):
            rt = (rope(z[:, LANES * k:LANES * (k + 1)]) * scale).T.astype(BF16)
            for e in range(2):
                qt_ref[4 * j + 2 * k + e] = rt[e * HEAD_DIM:(e + 1) * HEAD_DIM, :]

    z = _dot(xn, w_ref[:, _C_CMP:_C_CMP + 256])
    kv_ref[0] = rope(z[:, :LANES])
    kv_ref[1] = z[:, LANES:]
    for j in range(CMP_STRIDE):
        every = pl.ds(j, tm // CMP_STRIDE, stride=CMP_STRIDE)
        kc_ref[:, j * KV_WIDTH:(j + 1) * KV_WIDTH] = kv_ref[0, every, :]
        vc_ref[:, j * KV_WIDTH:(j + 1) * KV_WIDTH] = kv_ref[1, every, :]

    zeros = jnp.zeros((tm, HEAD_DIM), BF16)
    pos = (pl.program_id(0) % tiles_per_seq) * tm + lax.broadcasted_iota(jnp.int32, (tm, LANES), 0)
    onehot = jnp.where(lane == pos // SEL_BLOCK, 1.0, 0.0).astype(BF16)

    z = _dot(xn, w_ref[:, _C_SLC:_C_SLC + 256])
    k = rope(z[:, :LANES]).astype(BF16)
    vt = z[:, LANES:].T.astype(BF16)
    for g in range(N_KV_HEADS):
        gs = slice(g * HEAD_DIM, (g + 1) * HEAD_DIM)
        ks_ref[g, :, 0:LANES] = onehot
        ks_ref[g, :, LANES:LANES + HEAD_DIM] = k[:, gs]
        ks_ref[g, :, LANES + HEAD_DIM:] = zeros
        vst_ref[g, 0:HEAD_DIM, :] = vt[gs, :]
        vst_ref[g, HEAD_DIM:, :] = jnp.ones((V_ROWS - HEAD_DIM, tm), BF16)

    z = _dot(xn, w_ref[:, _C_WIN:_C_WIN + 256])
    k = rope(z[:, :LANES]).astype(BF16)
    vt = z[:, LANES:].T.astype(BF16)
    for g in range(N_KV_HEADS):
        gs = slice(g * HEAD_DIM, (g + 1) * HEAD_DIM)
        kw_ref[g] = k[:, gs]
        for c in range(tm // WIN_SUB):
            vwt_ref[g, c, 0:HEAD_DIM, :] = vt[gs, c * WIN_SUB:(c + 1) * WIN_SUB]
            vwt_ref[g, c, HEAD_DIM:, :] = jnp.ones((V_ROWS - HEAD_DIM, WIN_SUB), BF16)

    for j in range(S5_WIDTH // 256):
        u_ref[:, 256 * j:256 * (j + 1)] = _dot(xn, w_ref[:, _C_U + 256 * j:_C_U + 256 * (j + 1)])
    gates_t = jax.nn.sigmoid(_dot(xn, w_ref[:, _C_G:_C_END])).T
    for g in range(N_KV_HEADS):
        gtt_ref[g] = gates_t[g * LANES:g * LANES + GATE_ROWS, :]


def _ffn_proj(x, g_ffn, w1, w3, w2, g, w_in_r, cos, sin_signed, batch, seq):
    t = batch * seq
    tm = TK_SEL
    n = seq // tm
    bl = lambda w: pl.BlockSpec((None, tm, w), lambda i: (i // n, i % n, 0))
    bgl = lambda w: pl.BlockSpec((None, N_KV_HEADS, tm, w), lambda i: (i // n, 0, i % n, 0))
    tab = pl.BlockSpec((tm, LANES), lambda i: (i % n, 0))
    row = pl.BlockSpec((tm, D_MODEL), lambda i: (i, 0))
    strided = pl.BlockSpec((None, tm // CMP_STRIDE, CMP_STRIDE * KV_WIDTH), lambda i: (i // n, i % n, 0))
    sds = jax.ShapeDtypeStruct
    return pl.pallas_call(
        functools.partial(_ffn_proj_body, tiles_per_seq=n),
        grid=(t // tm,),
        in_specs=[row] + [_resident()] * 6 + [tab, tab],
        out_specs=[
            row,
            pl.BlockSpec((None, N_HEADS, HEAD_DIM, tm), lambda i: (i // n, 0, 0, i % n)),
            strided, strided,
            bgl(2 * LANES),
            pl.BlockSpec((None, N_KV_HEADS, None, V_ROWS, tm), lambda i: (i // n, 0, i % n, 0, 0)),
            bgl(HEAD_DIM),
            pl.BlockSpec((None, N_KV_HEADS, tm // WIN_SUB, V_ROWS, WIN_SUB), lambda i: (i // n, 0, i % n, 0, 0)),
            pl.BlockSpec((None, N_KV_HEADS, GATE_ROWS, tm), lambda i: (i // n, 0, 0, i % n)),
            bl(S5_WIDTH),
        ],
        out_shape=[
            sds((t, D_MODEL), F32),
            sds((batch, N_HEADS, HEAD_DIM, seq), BF16),
            sds((batch, seq // CMP_STRIDE, CMP_STRIDE * KV_WIDTH), F32),
            sds((batch, seq // CMP_STRIDE, CMP_STRIDE * KV_WIDTH), F32),
            sds((batch, N_KV_HEADS, seq, 2 * LANES), BF16), sds((batch, N_KV_HEADS, n, V_ROWS, tm), BF16),
            sds((batch, N_KV_HEADS, seq, HEAD_DIM), BF16), sds((batch, N_KV_HEADS, seq // WIN_SUB, V_ROWS, WIN_SUB), BF16),
            sds((batch, N_KV_HEADS, GATE_ROWS, seq), F32), sds((batch, seq, S5_WIDTH), F32),
        ],
        scratch_shapes=[pltpu.VMEM((2, tm, KV_WIDTH), F32)],
        compiler_params=_params(1),
        name="ffn_proj",
    )(x, g_ffn, w1, w3, w2, g, w_in_r, cos, sin_signed)


def _compress_body(kc_ref, vc_ref, pek_ref, pev_ref, wkt_ref, wkb_ref, wvt_ref, wvb_ref, wk2_ref, wv2_ref,
                   ko_ref, vot_ref, v_ref):
    nc = kc_ref.shape[0]

    def hidden(x, pe_ref, wt_ref, wb_ref):
        top = _dot((x + pe_ref[0:1, :]).astype(BF16), wt_ref[...])
        bot = _dot((x + pe_ref[1:2, :]).astype(BF16), wb_ref[...])
        pre = top + pltpu.roll(bot, nc - 1, 0)
        return _gelu_tanh(pre).astype(BF16)

    hk = hidden(kc_ref[...], pek_ref, wkt_ref, wkb_ref)
    hv = hidden(vc_ref[...], pev_ref, wvt_ref, wvb_ref)
    for g in range(N_KV_HEADS):
        gs = slice(g * CMP_HIDDEN, (g + 1) * CMP_HIDDEN)
        ko_ref[g] = _dot(hk[:, gs], wk2_ref[...]).astype(BF16)
        v_ref[...] = _dot(hv[:, gs], wv2_ref[...])
        vot_ref[g] = v_ref[...].T.astype(BF16)


def _compress(kc, vc, pe_k2, pe_v2, wkt, wkb, wvt, wvb, wk2, wv2):
    batch, nc, width = kc.shape
    x_spec = pl.BlockSpec((None, nc, width), lambda b: (b, 0, 0))
    sds = jax.ShapeDtypeStruct
    return pl.pallas_call(
        _compress_body,
        grid=(batch,),
        in_specs=[x_spec, x_spec] + [_resident()] * 8,
        out_specs=[pl.BlockSpec((None, N_KV_HEADS, nc, HEAD_DIM), lambda b: (b, 0, 0, 0)),
                   pl.BlockSpec((None, N_KV_HEADS, HEAD_DIM, nc), lambda b: (b, 0, 0, 0))],
        out_shape=[sds((batch, N_KV_HEADS, nc, HEAD_DIM), BF16), sds((batch, N_KV_HEADS, HEAD_DIM, nc), BF16)],
        scratch_shapes=[pltpu.VMEM((nc, HEAD_DIM), F32)],
        compiler_params=_params(1),
        name="compress",
    )(kc, vc, pe_k2, pe_v2, wkt, wkb, wvt, wvb, wk2, wv2)


def _cmpwin_body(qt_ref, kc_ref, vct_ref, ovt_ref, kw_ref, vwt_ref, gtt_ref, opt_ref, bias_ref,
                 s_ref, ocmp_ref, imp_ref):
    tq = qt_ref.shape[2]
    cols = HEADS_PER_GROUP * tq
    nc = kc_ref.shape[0]
    i = pl.program_id(2)
    q_t = jnp.concatenate([qt_ref[h] for h in range(HEADS_PER_GROUP)], axis=1)
    t_lane = i * tq + lax.broadcasted_iota(jnp.int32, (1, cols), 1) % tq

    def compressed(n):
        s = _dot(kc_ref[0:n, :], q_t)
        c_end = lax.broadcasted_iota(jnp.int32, (n, cols), 0) * CMP_STRIDE + (CMP_LEN - 1)
        s = jnp.where(c_end <= t_lane, s, NEG)
        e = jnp.exp(s - jnp.max(s, axis=0, keepdims=True))
        den = jnp.maximum(jnp.sum(e, axis=0, keepdims=True), 1e-30)
        p = e * jnp.where(t_lane >= CMP_LEN - 1, 1.0 / den, 0.0)
        ocmp_ref[...] = _dot(vct_ref[:, 0:n], p.astype(BF16))
        psum = p[:, 0:tq]
        for h in range(1, HEADS_PER_GROUP):
            psum = psum + p[:, h * tq:(h + 1) * tq]
        p_hi = psum.astype(BF16)
        p_lo = (psum - p_hi.astype(F32)).astype(BF16)
        imp_ref[...] = _dot(ovt_ref[:, 0:n], p_hi) + _dot(ovt_ref[:, 0:n], p_lo)

    finished = (i * tq + tq - CMP_LEN) // CMP_STRIDE + 1
    n_steps = nc // CMP_STEP
    for k in range(1, n_steps + 1):
        in_step = finished > (k - 1) * CMP_STEP
        if k < n_steps:
            in_step = jnp.logical_and(in_step, finished <= k * CMP_STEP)
        pl.when(in_step)(functools.partial(compressed, k * CMP_STEP))
    o_cmp = ocmp_ref[...]
    imp = imp_ref[...]

    blk = lax.broadcasted_iota(jnp.int32, (MAX_SEL_BLOCKS, tq), 0)
    behind = t_lane[:, 0:tq] // SEL_BLOCK - blk
    score = jnp.where(blk == 0, BIG, jnp.where(behind == 0, BIG, jnp.where(behind == 1, BIG, imp)))
    score = jnp.where(behind >= 0, score, -BIG)
    chosen = jnp.zeros((MAX_SEL_BLOCKS, tq), F32)
    for _ in range(SEL_TOPK):
        best = jnp.max(score, axis=0, keepdims=True)
        first = jnp.min(jnp.where(score == best, blk, MAX_SEL_BLOCKS), axis=0, keepdims=True)
        hit = blk == first
        chosen = jnp.where(hit, 1.0, chosen)
        score = jnp.where(hit, -jnp.inf, score)
    bias_ref[...] = jnp.where(chosen > 0.0, jnp.where(behind >= 0, 0.0, NEG), NEG).astype(BF16)

    span = WINDOW + tq
    start = pl.multiple_of(jnp.maximum(i * tq - WINDOW, 0), tq)
    s_ref[...] = _dot(kw_ref[pl.ds(start, span), :], q_t)

    def band(rows):
        kpos = start + rows.start + lax.broadcasted_iota(jnp.int32, (rows.stop - rows.start, cols), 0)
        dist = lax.bitcast_convert_type(t_lane - kpos, jnp.uint32)
        s_ref[rows, :] = jnp.where(dist < WINDOW, s_ref[rows, :], NEG)

    @pl.when(i * tq >= WINDOW)
    def _():
        band(slice(0, tq))
        band(slice(span - tq, span))

    @pl.when(i * tq < WINDOW)
    def _():
        band(slice(0, span))

    s = s_ref[...]
    e = jnp.exp(s - jnp.max(s, axis=0, keepdims=True)).astype(BF16)
    c0 = start // WIN_SUB
    vw_t = jnp.concatenate([vwt_ref[c0 + j] for j in range(span // WIN_SUB)], axis=1)
    acc = _dot(vw_t, e)
    o_win = acc[0:HEAD_DIM, :] / acc[HEAD_DIM:HEAD_DIM + 1, :]

    gates = gtt_ref[...]
    for h in range(HEADS_PER_GROUP):
        hs = slice(h * tq, (h + 1) * tq)
        opt_ref[h * HEAD_DIM:(h + 1) * HEAD_DIM, :] = (gates[3 * h:3 * h + 1, :] * o_cmp[:, hs]
                                                       + gates[3 * h + 2:3 * h + 3, :] * o_win[:, hs])


def _cmpwin(qt, kcmp, vcmp_t, overlap_t, kwin, vwin_t, gates_t):
    batch, _, _, seq = qt.shape
    nc = kcmp.shape[2]
    tq = TQ
    per_bg = lambda *shape: pl.BlockSpec((None, None) + shape, lambda b, g, i: (b, g) + (0,) * len(shape))
    sds = jax.ShapeDtypeStruct
    return pl.pallas_call(
        _cmpwin_body,
        grid=(batch, N_KV_HEADS, seq // tq),
        in_specs=[
            pl.BlockSpec((None, HEADS_PER_GROUP, HEAD_DIM, tq), lambda b, g, i: (b, g, 0, i)),
            per_bg(nc, HEAD_DIM), per_bg(HEAD_DIM, nc), _resident(),
            per_bg(seq, HEAD_DIM), per_bg(seq // WIN_SUB, V_ROWS, WIN_SUB),
            pl.BlockSpec((None, None, GATE_ROWS, tq), lambda b, g, i: (b, g, 0, i)),
        ],
        out_specs=[pl.BlockSpec((None, None, HEADS_PER_GROUP * HEAD_DIM, tq), lambda b, g, i: (b, g, 0, i)),
                   pl.BlockSpec((None, None, MAX_SEL_BLOCKS, tq), lambda b, g, i: (b, g, 0, i))],
        out_shape=[sds((batch, N_KV_HEADS, HEADS_PER_GROUP * HEAD_DIM, seq), F32),
                   sds((batch, N_KV_HEADS, MAX_SEL_BLOCKS, seq), BF16)],
        scratch_shapes=[pltpu.VMEM((WINDOW + tq, HEADS_PER_GROUP * tq), F32),
                        pltpu.VMEM((HEAD_DIM, HEADS_PER_GROUP * tq), F32), pltpu.VMEM((MAX_SEL_BLOCKS, tq), F32)],
        compiler_params=_params(3),
        name="cmpwin",
    )(qt, kcmp, vcmp_t, overlap_t, kwin, vwin_t, gates_t)


def _select_body(qt_ref, bias_ref, ks_ref, vst_ref, gtt_ref, opt_ref, o_ref, qa_ref, m_ref, acc_ref, *head_refs):
    tq = qt_ref.shape[2]
    cols = HEADS_PER_GROUP * tq
    tk = TK_SEL
    i = pl.program_id(2)

    bias = bias_ref[...]
    for h in range(HEADS_PER_GROUP):
        qa_ref[0:LANES, h * tq:(h + 1) * tq] = bias
        qa_ref[LANES:LANES + HEAD_DIM, h * tq:(h + 1) * tq] = qt_ref[h]
    qa_ref[LANES + HEAD_DIM:, :] = jnp.zeros((LANES - HEAD_DIM, cols), BF16)
    m_ref[...] = jnp.full(m_ref.shape, NEG, F32)
    acc_ref[...] = jnp.zeros(acc_ref.shape, F32)

    s_refs, mx_refs = head_refs[:HEADS_PER_GROUP], head_refs[HEADS_PER_GROUP:]

    def scores(c, causal, h):
        hs = slice(h * tq, (h + 1) * tq)
        off = pl.multiple_of(c * tk, tk)
        s = _dot(ks_ref[pl.ds(off, tk), :], qa_ref[:, hs])
        if causal:
            kpos = off + lax.broadcasted_iota(jnp.int32, s.shape, 0)
            t = i * tq + lax.broadcasted_iota(jnp.int32, s.shape, 1) % tq
            s = jnp.where(kpos <= t, s, NEG)
        s_refs[h][c % 2] = s
        mx_refs[h][c % 2] = jnp.max(s, axis=0, keepdims=True)

    def accumulate(c, h):
        hs = slice(h * tq, (h + 1) * tq)
        m_old = m_ref[:, hs]
        m_new = jnp.maximum(m_old, mx_refs[h][c % 2])
        p = jnp.exp(s_refs[h][c % 2] - m_new).astype(BF16)
        acc_ref[:, hs] = jnp.exp(m_old - m_new) * acc_ref[:, hs] + _dot(vst_ref[c], p)
        m_ref[:, hs] = m_new

    heads = range(HEADS_PER_GROUP)

    n_full = (i * tq) // tk

    def pair(c, causal):
        scores(c + 1, causal, HEADS_PER_GROUP - 1)
        for h in heads:
            accumulate(c, h)
            if h > 0:
                scores(c + 1, causal, h - 1)

    @pl.when(n_full > 0)
    def _():
        for h in heads:
            scores(0, False, h)

    def two_pairs(k, carry):
        pair(2 * k, False)
        pair(2 * k + 1, False)
        return carry

    n_plain = jnp.maximum(n_full - 1, 0)
    lax.fori_loop(0, n_plain // 2, two_pairs, 0)

    @pl.when(n_plain % 2 == 1)
    def _():
        pair(n_plain - 1, False)

    @pl.when(n_full > 0)
    def _():
        pair(n_full - 1, True)

    @pl.when(n_full == 0)
    def _():
        for h in heads:
            scores(n_full, True, h)

    for h in heads:
        accumulate(n_full, h)

    acc = acc_ref[...]
    o_slc = acc[0:HEAD_DIM, :] / acc[HEAD_DIM:HEAD_DIM + 1, :]
    gates = gtt_ref[...]
    o_t = jnp.concatenate([gates[3 * h + 1:3 * h + 2, :] * o_slc[:, h * tq:(h + 1) * tq]
                           for h in range(HEADS_PER_GROUP)], axis=0)
    o_ref[...] = (opt_ref[...] + o_t).T.astype(BF16)


def _select(qt, bias_t, ksel, vsel_t, gates_t, o_part_t):
    batch, _, _, seq = qt.shape
    tq = TQ_SEL
    cols = HEADS_PER_GROUP * tq
    n_chunks = seq // TK_SEL
    tile_t = lambda r: pl.BlockSpec((None, None, r, tq), lambda b, g, i: (b, g, 0, i))
    return pl.pallas_call(
        _select_body,
        grid=(batch, N_KV_HEADS, seq // tq),
        in_specs=[
            pl.BlockSpec((None, HEADS_PER_GROUP, HEAD_DIM, tq), lambda b, g, i: (b, g, 0, i)),
            tile_t(MAX_SEL_BLOCKS),
            pl.BlockSpec((None, None, seq, 2 * LANES), lambda b, g, i: (b, g, 0, 0)),
            pl.BlockSpec((None, None, n_chunks, V_ROWS, TK_SEL), lambda b, g, i: (b, g, 0, 0, 0)),
            tile_t(GATE_ROWS), tile_t(HEADS_PER_GROUP * HEAD_DIM),
        ],
        out_specs=pl.BlockSpec((None, tq, HEADS_PER_GROUP * HEAD_DIM), lambda b, g, i: (b, i, g)),
        out_shape=jax.ShapeDtypeStruct((batch, seq, NSA_WIDTH), BF16),
        scratch_shapes=([pltpu.VMEM((2 * LANES, cols), BF16), pltpu.VMEM((1, cols), F32), pltpu.VMEM((V_ROWS, cols), F32)]
                        + [pltpu.VMEM((2, TK_SEL, tq), F32)] * HEADS_PER_GROUP
                        + [pltpu.VMEM((2, 1, tq), F32)] * HEADS_PER_GROUP),
        compiler_params=_params(3),
        name="select",
    )(qt, bias_t, ksel, vsel_t, gates_t, o_part_t)


def _s5_body(u_ref, bt_ref, pw_ref, pwb_ref, a1_ref, ltri_ref, ct_ref, d_ref, wg_ref, bg_ref, o_ref, c_ref, x_ref, h_ref):
    @pl.when(pl.program_id(1) == 0)
    def _():
        c_ref[...] = jnp.zeros(c_ref.shape, F32)

    n = S5_COLS // S5_SPLIT
    u = u_ref[...]
    last = S5_CHUNK - 1
    ys = []
    for hf in range(S5_SPLIT):
        cs = slice(hf * n, (hf + 1) * n)
        p_re, p_im, q_re, q_im = pwb_ref[0, :, cs], pwb_ref[1, :, cs], pwb_ref[2, :, cs], pwb_ref[3, :, cs]
        pl_re, pl_im = pw_ref[0, last:last + 1, cs], pw_ref[1, last:last + 1, cs]
        a_re, a_im = a1_ref[0:1, cs], a1_ref[1:2, cs]
        width = S5_WIDTH // S5_SPLIT
        bu = _dot(u[:, hf * width:(hf + 1) * width].astype(BF16), bt_ref[hf])
        for j in range(u.shape[0] // S5_CHUNK):
            r = slice(j * S5_CHUNK, (j + 1) * S5_CHUNK)
            b_re, b_im = bu[r, 0:n].astype(BF16), bu[r, n:2 * n].astype(BF16)
            x_ref[r, 0:n] = q_re * b_re - q_im * b_im
            x_ref[r, n:2 * n] = q_re * b_im + q_im * b_re
        local = _dot(ltri_ref[...], x_ref[...])
        c_re, c_im = c_ref[0:1, cs], c_ref[1:2, cs]
        for j in range(u.shape[0] // S5_CHUNK):
            r = slice(j * S5_CHUNK, (j + 1) * S5_CHUNK)
            s_re, s_im = local[r, 0:n] + c_re, local[r, n:2 * n] + c_im
            t_re, t_im = s_re.astype(BF16), s_im.astype(BF16)
            h_ref[r, 0:n] = p_re * t_re - p_im * t_im
            h_ref[r, n:2 * n] = p_re * t_im + p_im * t_re
            e_re, e_im = s_re[last:last + 1], s_im[last:last + 1]
            l_re = pl_re * e_re - pl_im * e_im
            l_im = pl_re * e_im + pl_im * e_re
            c_re = a_re * l_re - a_im * l_im
            c_im = a_re * l_im + a_im * l_re
        c_ref[0:1, cs] = c_re
        c_ref[1:2, cs] = c_im
        ys.append(_dot(h_ref[...], ct_ref[hf]))
    y = _gelu_tanh(jnp.concatenate(ys, axis=1) + d_ref[...] * u)
    o_ref[...] = (y * jax.nn.sigmoid(_dot(y.astype(BF16), wg_ref[...]) + bg_ref[...])).astype(BF16)


def _s5(u, bt, powers, a1, ltri, ct, d, w_glu, b_glu):
    batch, seq, _ = u.shape
    ts = S5_TILE
    tile = pl.BlockSpec((None, ts, S5_WIDTH), lambda b, j: (b, j, 0))
    cols = 2 * S5_COLS // S5_SPLIT
    return pl.pallas_call(
        _s5_body,
        grid=(batch, seq // ts),
        in_specs=[tile] + [_resident()] * 9,
        out_specs=tile,
        out_shape=jax.ShapeDtypeStruct((batch, seq, S5_WIDTH), BF16),
        scratch_shapes=[pltpu.VMEM((2, S5_COLS), F32), pltpu.VMEM((ts, cols), BF16), pltpu.VMEM((ts, cols), BF16)],
        compiler_params=_params(2),
        name="s5",
    )(u, bt, powers, powers.astype(BF16), a1, ltri, ct, d, w_glu, b_glu)


def _ffn_out_body(*refs, final_norm):
    (h_ref, on_ref, os_ref, won_ref, wos_ref, g2_ref, w1_ref, w3_ref, w2_ref,
     gp_ref, wg_ref, p_ref, wp_ref) = refs[:13]
    o_ref = refs[-1]
    h = h_ref[...] + _dot(on_ref[...], won_ref[...]) + _dot(os_ref[...], wos_ref[...])
    h = _swiglu_half_step(h, g2_ref, w1_ref, w3_ref, w2_ref)
    gate = jax.nn.sigmoid(_dot(_rmsnorm(h, gp_ref[...]).astype(BF16), wg_ref[...]))
    h = h + gate * _dot(p_ref[...].astype(BF16), wp_ref[...])
    o_ref[...] = _rmsnorm(h, refs[13][...]) if final_norm else h


def _ffn_out(h, o_nsa, o_s5, wo_n, wo_s, g_ffn, w1, w3, w2, g_ple, w_gate, p, w_ple, g_final=None):
    t = h.shape[0]
    tm = min(TM_DENSE, t)
    row = lambda w: pl.BlockSpec((tm, w), lambda i: (i, 0))
    args = [h, o_nsa, o_s5, wo_n, wo_s, g_ffn, w1, w3, w2, g_ple, w_gate, p, w_ple]
    specs = [row(D_MODEL), row(NSA_WIDTH), row(S5_WIDTH)] + [_resident()] * 8 + [row(PLE_DIM), _resident()]
    if g_final is not None:
        args.append(g_final)
        specs.append(_resident())
    return pl.pallas_call(
        functools.partial(_ffn_out_body, final_norm=g_final is not None),
        grid=(t // tm,),
        in_specs=specs,
        out_specs=row(D_MODEL),
        out_shape=jax.ShapeDtypeStruct((t, D_MODEL), F32),
        compiler_params=_params(1),
        name="ffn_out",
    )(*args)


def _rearrange_w_in(w_in):
    g0 = NSA_WIDTH + 6 * KV_WIDTH
    per_group = 3 * HEADS_PER_GROUP
    pad = jnp.zeros((D_MODEL, LANES - per_group), w_in.dtype)
    gate_cols = []
    for g in range(N_KV_HEADS):
        gate_cols += [w_in[:, g0 + g * per_group:g0 + (g + 1) * per_group], pad]
    u0 = g0 + 3 * N_HEADS
    return jnp.concatenate([w_in[:, :g0], w_in[:, u0:]] + gate_cols, axis=1).astype(BF16)


def _rope_tables(seq):
    half = HEAD_DIM // 2
    inv = ROPE_THETA ** (-jnp.arange(half, dtype=F32) / half)
    ang = jnp.arange(seq, dtype=F32)[:, None] * inv[None, :]
    cos = jnp.tile(jnp.cos(ang), (1, LANES // half))
    sin = jnp.tile(jnp.concatenate([-jnp.sin(ang), jnp.sin(ang)], axis=1), (1, LANES // HEAD_DIM))
    return cos, sin


def _compress_weights(pe, w1, w2):
    w = w1.reshape(2, CMP_STRIDE, HEAD_DIM, CMP_HIDDEN).astype(BF16)
    zero = jnp.zeros_like(w)
    rows = [jnp.concatenate([w if e == g else zero for e in range(N_KV_HEADS)], axis=-1) for g in range(N_KV_HEADS)]
    big = jnp.stack(rows, axis=2).reshape(2, CMP_STRIDE * KV_WIDTH, N_KV_HEADS * CMP_HIDDEN)
    pe2 = jnp.broadcast_to(pe.reshape(2, CMP_STRIDE, 1, HEAD_DIM), (2, CMP_STRIDE, N_KV_HEADS, HEAD_DIM))
    return pe2.reshape(2, CMP_STRIDE * KV_WIDTH), big[0], big[1], w2.astype(BF16)


def _overlap_matrix(n_cmp_rows):
    c_start = np.arange(n_cmp_rows)[:, None] * CMP_STRIDE
    s_start = np.arange(MAX_SEL_BLOCKS)[None, :] * SEL_BLOCK
    ov = (c_start < s_start + SEL_BLOCK) & (c_start + CMP_LEN > s_start)
    return jnp.asarray(ov.T, dtype=BF16)


def _s5_tables(a_re, a_im, log_dt, b_re, b_im, c_re, c_im):
    dt = jnp.exp(log_dt)[:, None]
    lr, li = a_re * dt, a_im * dt
    mag = jnp.exp(lr)
    ab_re, ab_im = mag * jnp.cos(li), mag * jnp.sin(li)
    den = a_re * a_re + a_im * a_im
    nr, ni = ab_re - 1.0, ab_im
    bc_re = (nr * a_re + ni * a_im) / den
    bc_im = (ni * a_re - nr * a_im) / den
    bt_re = b_re * bc_re[..., None] - b_im * bc_im[..., None]
    bt_im = b_re * bc_im[..., None] + b_im * bc_re[..., None]
    gh = S5_GROUPS // S5_SPLIT
    eye = jnp.eye(gh, dtype=F32)
    split = lambda w: w.reshape((S5_SPLIT, gh) + w.shape[1:])
    blockdiag_in = lambda w: jnp.einsum("hgni,ge->hgien", split(w), eye).reshape(S5_SPLIT, gh * S5_GROUP, gh * S5_STATE)
    bt = jnp.concatenate([blockdiag_in(bt_re), blockdiag_in(bt_im)], axis=2).astype(BF16)
    blockdiag_out = lambda w: jnp.einsum("hgon,ge->hgneo", split(w), eye).reshape(S5_SPLIT, gh * S5_STATE, gh * S5_GROUP)
    ct = jnp.concatenate([blockdiag_out(c_re), -blockdiag_out(c_im)], axis=1).astype(BF16)
    k = jnp.arange(S5_CHUNK, dtype=F32)[:, None]
    lr, li = lr.reshape(1, S5_COLS), li.reshape(1, S5_COLS)
    powers = jnp.stack([jnp.exp(k * lr) * jnp.cos(k * li), jnp.exp(k * lr) * jnp.sin(k * li),
                        jnp.exp(-k * lr) * jnp.cos(k * li), -jnp.exp(-k * lr) * jnp.sin(k * li)])
    a1 = jnp.concatenate([ab_re.reshape(1, S5_COLS), ab_im.reshape(1, S5_COLS)], axis=0)
    ltri = jnp.asarray(np.kron(np.eye(S5_TILE // S5_CHUNK), np.tril(np.ones((S5_CHUNK, S5_CHUNK)))), dtype=BF16)
    return bt, powers, a1, ltri, ct


def kernel(x, p, norm_ffn1, ffn1_w1, ffn1_w3, ffn1_w2, norm_mix, w_in, cmp_pe_k, cmp_pe_v, cmp_wk1, cmp_wk2,
           cmp_wv1, cmp_wv2, s5_a_re, s5_a_im, s5_log_dt, s5_b_re, s5_b_im, s5_c_re, s5_c_im, s5_d, s5_w_glu,
           s5_b_glu, w_out, norm_ffn2, ffn2_w1, ffn2_w3, ffn2_w2, norm_ple, w_ple_gate, w_ple, norm_final):
    batch, seq, _ = x.shape
    depth = p.shape[0]
    t = batch * seq
    assert seq % TK_SEL == 0 and TK_SEL % TQ_SEL == 0 and seq % TQ_SEL == 0 and seq % S5_TILE == 0 and seq >= WINDOW + TQ and seq % (CMP_STRIDE * CMP_STEP) == 0
    assert seq // SEL_BLOCK <= MAX_SEL_BLOCKS and seq // SEL_BLOCK >= SEL_TOPK
    bf = lambda w: w.astype(BF16)
    row = lambda v: v.reshape(1, -1)

    cos, sin_signed = _rope_tables(seq)
    overlap_t = _overlap_matrix(seq // CMP_STRIDE)
    h = x.reshape(t, D_MODEL)
    for i in range(depth):
        h, qt, kc, vc, ksel, vsel_t, kwin, vwin_t, gates_t, u = _ffn_proj(
            h, row(norm_ffn1[i]), bf(ffn1_w1[i]), bf(ffn1_w3[i]), bf(ffn1_w2[i]),
            row(norm_mix[i]), _rearrange_w_in(w_in[i]), cos, sin_signed, batch, seq)
        pe_k2, wkt, wkb, wk2 = _compress_weights(cmp_pe_k[i], cmp_wk1[i], cmp_wk2[i])
        pe_v2, wvt, wvb, wv2 = _compress_weights(cmp_pe_v[i], cmp_wv1[i], cmp_wv2[i])
        kcmp, vcmp_t = _compress(kc, vc, pe_k2, pe_v2, wkt, wkb, wvt, wvb, wk2, wv2)
        o_part_t, bias_t = _cmpwin(qt, kcmp, vcmp_t, overlap_t, kwin, vwin_t, gates_t)
        o_nsa = _select(qt, bias_t, ksel, vsel_t, gates_t, o_part_t)

        bt, powers, a1, ltri, ct = _s5_tables(s5_a_re[i], s5_a_im[i], s5_log_dt[i], s5_b_re[i], s5_b_im[i],
                                              s5_c_re[i], s5_c_im[i])
        o_s5 = _s5(u, bt, powers, a1, ltri, ct, row(s5_d[i]), bf(s5_w_glu[i]), row(s5_b_glu[i]))

        h = _ffn_out(h, o_nsa.reshape(t, NSA_WIDTH), o_s5.reshape(t, S5_WIDTH),
                     bf(w_out[i][:NSA_WIDTH]), bf(w_out[i][NSA_WIDTH:]),
                     row(norm_ffn2[i]), bf(ffn2_w1[i]), bf(ffn2_w3[i]), bf(ffn2_w2[i]),
                     row(norm_ple[i]), bf(w_ple_gate[i]), p[i].reshape(t, PLE_DIM), bf(w_ple[i]),
                     row(norm_final) if i == depth - 1 else None)
    return h.reshape(batch, seq, D_MODEL)
```

```python
import functools

import jax
import jax.numpy as jnp
import numpy as np
from jax import lax
from jax.experimental import pallas as pl
from jax.experimental.pallas import tpu as pltpu

D_MODEL = 1024
PLE_DIM = 256
D_FF = 2816
N_HEADS = 8
N_KV_HEADS = 2
HEAD_DIM = 64
HEADS_PER_GROUP = N_HEADS // N_KV_HEADS
NSA_WIDTH = N_HEADS * HEAD_DIM
KV_WIDTH = N_KV_HEADS * HEAD_DIM
S5_WIDTH = D_MODEL - NSA_WIDTH
S5_GROUP = 16
S5_GROUPS = S5_WIDTH // S5_GROUP
S5_STATE = 64
S5_COLS = S5_GROUPS * S5_STATE
CMP_LEN = 32
CMP_STRIDE = 16
CMP_HIDDEN = 256
SEL_BLOCK = 64
SEL_TOPK = 16
WINDOW = 512
ROPE_THETA = 10000.0
RMS_EPS = 1e-6
NEG = -1e30
BIG = 1e9

LANES = 128
V_ROWS = HEAD_DIM + 16
GATE_ROWS = 16
MAX_SEL_BLOCKS = LANES
VMEM_LIMIT = 52 * 1024 * 1024

TM_DENSE = 512
FF_CHUNK = 256
TQ = 256
WIN_SUB = 128
CMP_STEP = 128
TQ_SEL = 512
TK_SEL = 512
S5_TILE = 256
S5_CHUNK = 64
S5_SPLIT = 2

F32 = jnp.float32
BF16 = jnp.bfloat16


def _dot(a, b):
    return jnp.dot(a, b, preferred_element_type=F32)


def _rmsnorm(x, g):
    ms = jnp.mean(x * x, axis=-1, keepdims=True)
    return x * lax.rsqrt(ms + RMS_EPS) * g


def _gelu_tanh(x):
    return 0.5 * x * (1.0 + jnp.tanh(np.sqrt(2.0 / np.pi).astype(np.float32) * (x + 0.044715 * (x * x * x))))


def _params(n_grid):
    return pltpu.CompilerParams(dimension_semantics=("arbitrary",) * n_grid, vmem_limit_bytes=VMEM_LIMIT)


def _resident():
    return pl.BlockSpec(memory_space=pltpu.VMEM)


def _swiglu_half_step(h, g_ref, w1_ref, w3_ref, w2_ref):
    xn = _rmsnorm(h, g_ref[...]).astype(BF16)
    acc = jnp.zeros(h.shape, F32)
    for j in range(D_FF // FF_CHUNK):
        sl = slice(j * FF_CHUNK, (j + 1) * FF_CHUNK)
        a = _dot(xn, w1_ref[:, sl])
        b = _dot(xn, w3_ref[:, sl])
        act = (a * jax.nn.sigmoid(a) * b).astype(BF16)
        acc = acc + _dot(act, w2_ref[sl, :])
    return h + 0.5 * acc


_C_Q = 0
_C_CMP = NSA_WIDTH
_C_SLC = _C_CMP + 2 * KV_WIDTH
_C_WIN = _C_SLC + 2 * KV_WIDTH
_C_U = _C_WIN + 2 * KV_WIDTH
_C_G = _C_U + S5_WIDTH
_C_END = _C_G + N_KV_HEADS * LANES


def _rope_pair(x, cos, sin_signed, first_half):
    fwd = pltpu.roll(x, HEAD_DIM // 2, 1)
    bwd = pltpu.roll(x, LANES - HEAD_DIM // 2, 1)
    return x * cos + jnp.where(first_half, bwd, fwd) * sin_signed


def _ffn_proj_body(x_ref, g1_ref, w1_ref, w3_ref, w2_ref, g_ref, w_ref, cos_ref, sin_ref,
                   h_ref, qt_ref, kc_ref, vc_ref, ks_ref, vst_ref, kw_ref, vwt_ref, gtt_ref, u_ref, kv_ref,
                   *, tiles_per_seq):
    tm = x_ref.shape[0]
    h = _swiglu_half_step(x_ref[...], g1_ref, w1_ref, w3_ref, w2_ref)
    h_ref[...] = h
    xn = _rmsnorm(h, g_ref[...]).astype(BF16)
    cos = cos_ref[...]
    sin = sin_ref[...]
    lane = lax.broadcasted_iota(jnp.int32, (tm, LANES), 1)
    first_half = (lane % HEAD_DIM) < (HEAD_DIM // 2)
    rope = lambda x: _rope_pair(x, cos, sin, first_half)

    scale = HEAD_DIM ** -0.5
    for j in range(NSA_WIDTH // 256):
        z = _dot(xn, w_ref[:, _C_Q + 256 * j:_C_Q + 256 * (j + 1)])
        for k in range(2):
            rt = (rope(z[:, LANES * k:LANES * (k + 1)]) * scale).T.astype(BF16)
            for e in range(2):
                qt_ref[4 * j + 2 * k + e] = rt[e * HEAD_DIM:(e + 1) * HEAD_DIM, :]

    z = _dot(xn, w_ref[:, _C_CMP:_C_CMP + 256])
    kv_ref[0] = rope(z[:, :LANES])
    kv_ref[1] = z[:, LANES:]
    for j in range(CMP_STRIDE):
        every = pl.ds(j, tm // CMP_STRIDE, stride=CMP_STRIDE)
        kc_ref[:, j * KV_WIDTH:(j + 1) * KV_WIDTH] = kv_ref[0, every, :]
        vc_ref[:, j * KV_WIDTH:(j + 1) * KV_WIDTH] = kv_ref[1, every, :]

    zeros = jnp.zeros((tm, HEAD_DIM), BF16)
    pos = (pl.program_id(0) % tiles_per_seq) * tm + lax.broadcasted_iota(jnp.int32, (tm, LANES), 0)
    onehot = jnp.where(lane == pos // SEL_BLOCK, 1.0, 0.0).astype(BF16)

    z = _dot(xn, w_ref[:, _C_SLC:_C_SLC + 256])
    k = rope(z[:, :LANES]).astype(BF16)
    vt = z[:, LANES:].T.astype(BF16)
    for g in range(N_KV_HEADS):
        gs = slice(g * HEAD_DIM, (g + 1) * HEAD_DIM)
        ks_ref[g, :, 0:LANES] = onehot
        ks_ref[g, :, LANES:LANES + HEAD_DIM] = k[:, gs]
        ks_ref[g, :, LANES + HEAD_DIM:] = zeros
        vst_ref[g, 0:HEAD_DIM, :] = vt[gs, :]
        vst_ref[g, HEAD_DIM:, :] = jnp.ones((V_ROWS - HEAD_DIM, tm), BF16)

    z = _dot(xn, w_ref[:, _C_WIN:_C_WIN + 256])
    k = rope(z[:, :LANES]).astype(BF16)
    vt = z[:, LANES:].T.astype(BF16)
    for g in range(N_KV_HEADS):
        gs = slice(g * HEAD_DIM, (g + 1) * HEAD_DIM)
        kw_ref[g] = k[:, gs]
        for c in range(tm // WIN_SUB):
            vwt_ref[g, c, 0:HEAD_DIM, :] = vt[gs, c * WIN_SUB:(c + 1) * WIN_SUB]
            vwt_ref[g, c, HEAD_DIM:, :] = jnp.ones((V_ROWS - HEAD_DIM, WIN_SUB), BF16)

    for j in range(S5_WIDTH // 256):
        u_ref[:, 256 * j:256 * (j + 1)] = _dot(xn, w_ref[:, _C_U + 256 * j:_C_U + 256 * (j + 1)])
    gates_t = jax.nn.sigmoid(_dot(xn, w_ref[:, _C_G:_C_END])).T
    for g in range(N_KV_HEADS):
        gtt_ref[g] = gates_t[g * LANES:g * LANES + GATE_ROWS, :]


def _ffn_proj(x, g_ffn, w1, w3, w2, g, w_in_r, cos, sin_signed, batch, seq):
    t = batch * seq
    tm = TK_SEL
    n = seq // tm
    bl = lambda w: pl.BlockSpec((None, tm, w), lambda i: (i // n, i % n, 0))
    bgl = lambda w: pl.BlockSpec((None, N_KV_HEADS, tm, w), lambda i: (i // n, 0, i % n, 0))
    tab = pl.BlockSpec((tm, LANES), lambda i: (i % n, 0))
    row = pl.BlockSpec((tm, D_MODEL), lambda i: (i, 0))
    strided = pl.BlockSpec((None, tm // CMP_STRIDE, CMP_STRIDE * KV_WIDTH), lambda i: (i // n, i % n, 0))
    sds = jax.ShapeDtypeStruct
    return pl.pallas_call(
        functools.partial(_ffn_proj_body, tiles_per_seq=n),
        grid=(t // tm,),
        in_specs=[row] + [_resident()] * 6 + [tab, tab],
        out_specs=[
            row,
            pl.BlockSpec((None, N_HEADS, HEAD_DIM, tm), lambda i: (i // n, 0, 0, i % n)),
            strided, strided,
            bgl(2 * LANES),
            pl.BlockSpec((None, N_KV_HEADS, None, V_ROWS, tm), lambda i: (i // n, 0, i % n, 0, 0)),
            bgl(HEAD_DIM),
            pl.BlockSpec((None, N_KV_HEADS, tm // WIN_SUB, V_ROWS, WIN_SUB), lambda i: (i // n, 0, i % n, 0, 0)),
            pl.BlockSpec((None, N_KV_HEADS, GATE_ROWS, tm), lambda i: (i // n, 0, 0, i % n)),
            bl(S5_WIDTH),
        ],
        out_shape=[
            sds((t, D_MODEL), F32),
            sds((batch, N_HEADS, HEAD_DIM, seq), BF16),
            sds((batch, seq // CMP_STRIDE, CMP_STRIDE * KV_WIDTH), F32),
            sds((batch, seq // CMP_STRIDE, CMP_STRIDE * KV_WIDTH), F32),
            sds((batch, N_KV_HEADS, seq, 2 * LANES), BF16), sds((batch, N_KV_HEADS, n, V_ROWS, tm), BF16),
            sds((batch, N_KV_HEADS, seq, HEAD_DIM), BF16), sds((batch, N_KV_HEADS, seq // WIN_SUB, V_ROWS, WIN_SUB), BF16),
            sds((batch, N_KV_HEADS, GATE_ROWS, seq), F32), sds((batch, seq, S5_WIDTH), F32),
        ],
        scratch_shapes=[pltpu.VMEM((2, tm, KV_WIDTH), F32)],
        compiler_params=_params(1),
        name="ffn_proj",
    )(x, g_ffn, w1, w3, w2, g, w_in_r, cos, sin_signed)


def _compress_body(kc_ref, vc_ref, pek_ref, pev_ref, wkt_ref, wkb_ref, wvt_ref, wvb_ref, wk2_ref, wv2_ref,
                   ko_ref, vot_ref, v_ref):
    nc = kc_ref.shape[0]

    def hidden(x, pe_ref, wt_ref, wb_ref):
        top = _dot((x + pe_ref[0:1, :]).astype(BF16), wt_ref[...])
        bot = _dot((x + pe_ref[1:2, :]).astype(BF16), wb_ref[...])
        pre = top + pltpu.roll(bot, nc - 1, 0)
        return _gelu_tanh(pre).astype(BF16)

    hk = hidden(kc_ref[...], pek_ref, wkt_ref, wkb_ref)
    hv = hidden(vc_ref[...], pev_ref, wvt_ref, wvb_ref)
    for g in range(N_KV_HEADS):
        gs = slice(g * CMP_HIDDEN, (g + 1) * CMP_HIDDEN)
        ko_ref[g] = _dot(hk[:, gs], wk2_ref[...]).astype(BF16)
        v_ref[...] = _dot(hv[:, gs], wv2_ref[...])
        vot_ref[g] = v_ref[...].T.astype(BF16)


def _compress(kc, vc, pe_k2, pe_v2, wkt, wkb, wvt, wvb, wk2, wv2):
    batch, nc, width = kc.shape
    x_spec = pl.BlockSpec((None, nc, width), lambda b: (b, 0, 0))
    sds = jax.ShapeDtypeStruct
    return pl.pallas_call(
        _compress_body,
        grid=(batch,),
        in_specs=[x_spec, x_spec] + [_resident()] * 8,
        out_specs=[pl.BlockSpec((None, N_KV_HEADS, nc, HEAD_DIM), lambda b: (b, 0, 0, 0)),
                   pl.BlockSpec((None, N_KV_HEADS, HEAD_DIM, nc), lambda b: (b, 0, 0, 0))],
        out_shape=[sds((batch, N_KV_HEADS, nc, HEAD_DIM), BF16), sds((batch, N_KV_HEADS, HEAD_DIM, nc), BF16)],
        scratch_shapes=[pltpu.VMEM((nc, HEAD_DIM), F32)],
        compiler_params=_params(1),
        name="compress",
    )(kc, vc, pe_k2, pe_v2, wkt, wkb, wvt, wvb, wk2, wv2)


def _cmpwin_body(qt_ref, kc_ref, vct_ref, ovt_ref, edge_ref, kw_ref, vwt_ref, gtt_ref, opt_ref, bias_ref,
                 s_ref, ocmp_ref, imp_ref):
    tq = qt_ref.shape[2]
    cols = HEADS_PER_GROUP * tq
    nc = kc_ref.shape[0]
    i = pl.program_id(2)
    q_t = jnp.concatenate([qt_ref[h] for h in range(HEADS_PER_GROUP)], axis=1)
    t_lane = i * tq + lax.broadcasted_iota(jnp.int32, (1, cols), 1) % tq

    def compressed(n):
        s = _dot(kc_ref[0:n, :], q_t)
        c_end = lax.broadcasted_iota(jnp.int32, (n, cols), 0) * CMP_STRIDE + (CMP_LEN - 1)
        s = jnp.where(c_end <= t_lane, s, NEG)
        e = jnp.exp(s - jnp.max(s, axis=0, keepdims=True))
        den = jnp.maximum(jnp.sum(e, axis=0, keepdims=True), 1e-30)
        p = e * jnp.where(t_lane >= CMP_LEN - 1, 1.0 / den, 0.0)
        ocmp_ref[...] = _dot(vct_ref[:, 0:n], p.astype(BF16))
        psum = p[:, 0:tq]
        for h in range(1, HEADS_PER_GROUP):
            psum = psum + p[:, h * tq:(h + 1) * tq]
        p_hi = psum.astype(BF16)
        p_lo = (psum - p_hi.astype(F32)).astype(BF16)
        imp_ref[...] = _dot(ovt_ref[:, 0:n], p_hi) + _dot(ovt_ref[:, 0:n], p_lo)

    finished = (i * tq + tq - CMP_LEN) // CMP_STRIDE + 1
    n_steps = nc // CMP_STEP
    for k in range(1, n_steps + 1):
        in_step = finished > (k - 1) * CMP_STEP
        if k < n_steps:
            in_step = jnp.logical_and(in_step, finished <= k * CMP_STEP)
        pl.when(in_step)(functools.partial(compressed, k * CMP_STEP))
    o_cmp = ocmp_ref[...]
    imp = imp_ref[...]

    blk = lax.broadcasted_iota(jnp.int32, (MAX_SEL_BLOCKS, tq), 0)
    behind = t_lane[:, 0:tq] // SEL_BLOCK - blk
    score = jnp.where(blk == 0, BIG, jnp.where(behind == 0, BIG, jnp.where(behind == 1, BIG, imp)))
    score = jnp.where(behind >= 0, score, -BIG)
    chosen = jnp.zeros((MAX_SEL_BLOCKS, tq), F32)
    for _ in range(SEL_TOPK):
        best = jnp.max(score, axis=0, keepdims=True)
        first = jnp.min(jnp.where(score == best, blk, MAX_SEL_BLOCKS), axis=0, keepdims=True)
        hit = blk == first
        chosen = jnp.where(hit, 1.0, chosen)
        score = jnp.where(hit, -jnp.inf, score)
    bias_ref[...] = jnp.where(chosen > 0.0, jnp.where(behind >= 0, 0.0, NEG), NEG).astype(BF16)

    span = WINDOW + tq
    start = pl.multiple_of(jnp.maximum(i * tq - WINDOW, 0), tq)

    @pl.when(i * tq >= WINDOW)
    def _():
        s_ref[...] = edge_ref[...] + _dot(kw_ref[pl.ds(start, span), :], q_t)

    @pl.when(i * tq < WINDOW)
    def _():
        kpos = start + lax.broadcasted_iota(jnp.int32, (span, cols), 0)
        dist = lax.bitcast_convert_type(t_lane - kpos, jnp.uint32)
        s_ref[...] = jnp.where(dist < WINDOW, _dot(kw_ref[pl.ds(start, span), :], q_t), NEG)

    s = s_ref[...]
    e = jnp.exp(s - jnp.max(s, axis=0, keepdims=True)).astype(BF16)
    c0 = start // WIN_SUB
    vw_t = jnp.concatenate([vwt_ref[c0 + j] for j in range(span // WIN_SUB)], axis=1)
    acc = _dot(vw_t, e)
    o_win = acc[0:HEAD_DIM, :] / acc[HEAD_DIM:HEAD_DIM + 1, :]

    gates = gtt_ref[...]
    for h in range(HEADS_PER_GROUP):
        hs = slice(h * tq, (h + 1) * tq)
        opt_ref[h * HEAD_DIM:(h + 1) * HEAD_DIM, :] = (gates[3 * h:3 * h + 1, :] * o_cmp[:, hs]
                                                       + gates[3 * h + 2:3 * h + 3, :] * o_win[:, hs])


def _window_edge():
    r = np.arange(WINDOW + TQ)[:, None]
    c = np.arange(HEADS_PER_GROUP * TQ)[None, :] % TQ
    dist = WINDOW + c - r
    return jnp.asarray(np.where((dist >= 0) & (dist < WINDOW), 0.0, NEG), dtype=F32)


def _cmpwin(qt, kcmp, vcmp_t, overlap_t, kwin, vwin_t, gates_t):
    batch, _, _, seq = qt.shape
    nc = kcmp.shape[2]
    tq = TQ
    per_bg = lambda *shape: pl.BlockSpec((None, None) + shape, lambda b, g, i: (b, g) + (0,) * len(shape))
    sds = jax.ShapeDtypeStruct
    return pl.pallas_call(
        _cmpwin_body,
        grid=(batch, N_KV_HEADS, seq // tq),
        in_specs=[
            pl.BlockSpec((None, HEADS_PER_GROUP, HEAD_DIM, tq), lambda b, g, i: (b, g, 0, i)),
            per_bg(nc, HEAD_DIM), per_bg(HEAD_DIM, nc), _resident(), _resident(),
            per_bg(seq, HEAD_DIM), per_bg(seq // WIN_SUB, V_ROWS, WIN_SUB),
            pl.BlockSpec((None, None, GATE_ROWS, tq), lambda b, g, i: (b, g, 0, i)),
        ],
        out_specs=[pl.BlockSpec((None, None, HEADS_PER_GROUP * HEAD_DIM, tq), lambda b, g, i: (b, g, 0, i)),
                   pl.BlockSpec((None, None, MAX_SEL_BLOCKS, tq), lambda b, g, i: (b, g, 0, i))],
        out_shape=[sds((batch, N_KV_HEADS, HEADS_PER_GROUP * HEAD_DIM, seq), F32),
                   sds((batch, N_KV_HEADS, MAX_SEL_BLOCKS, seq), BF16)],
        scratch_shapes=[pltpu.VMEM((WINDOW + tq, HEADS_PER_GROUP * tq), F32),
                        pltpu.VMEM((HEAD_DIM, HEADS_PER_GROUP * tq), F32), pltpu.VMEM((MAX_SEL_BLOCKS, tq), F32)],
        compiler_params=_params(3),
        name="cmpwin",
    )(qt, kcmp, vcmp_t, overlap_t, _window_edge(), kwin, vwin_t, gates_t)


def _select_body(qt_ref, bias_ref, ks_ref, vst_ref, gtt_ref, opt_ref, o_ref, qa_ref, m_ref, acc_ref, *head_refs):
    tq = qt_ref.shape[2]
    cols = HEADS_PER_GROUP * tq
    tk = TK_SEL
    i = pl.program_id(2)

    bias = bias_ref[...]
    for h in range(HEADS_PER_GROUP):
        qa_ref[0:LANES, h * tq:(h + 1) * tq] = bias
        qa_ref[LANES:LANES + HEAD_DIM, h * tq:(h + 1) * tq] = qt_ref[h]
    qa_ref[LANES + HEAD_DIM:, :] = jnp.zeros((LANES - HEAD_DIM, cols), BF16)
    m_ref[...] = jnp.full(m_ref.shape, NEG, F32)
    acc_ref[...] = jnp.zeros(acc_ref.shape, F32)

    s_refs, mx_refs = head_refs[:HEADS_PER_GROUP], head_refs[HEADS_PER_GROUP:]

    def scores(c, causal, h):
        hs = slice(h * tq, (h + 1) * tq)
        off = pl.multiple_of(c * tk, tk)
        s = _dot(ks_ref[pl.ds(off, tk), :], qa_ref[:, hs])
        if causal:
            kpos = off + lax.broadcasted_iota(jnp.int32, s.shape, 0)
            t = i * tq + lax.broadcasted_iota(jnp.int32, s.shape, 1) % tq
            s = jnp.where(kpos <= t, s, NEG)
        s_refs[h][c % 2] = s
        mx_refs[h][c % 2] = jnp.max(s, axis=0, keepdims=True)

    def accumulate(c, h):
        hs = slice(h * tq, (h + 1) * tq)
        m_old = m_ref[:, hs]
        m_new = jnp.maximum(m_old, mx_refs[h][c % 2])
        p = jnp.exp(s_refs[h][c % 2] - m_new).astype(BF16)
        acc_ref[:, hs] = jnp.exp(m_old - m_new) * acc_ref[:, hs] + _dot(vst_ref[c], p)
        m_ref[:, hs] = m_new

    heads = range(HEADS_PER_GROUP)

    n_full = (i * tq) // tk

    def pair(c, causal):
        scores(c + 1, causal, HEADS_PER_GROUP - 1)
        for h in heads:
            accumulate(c, h)
            if h > 0:
                scores(c + 1, causal, h - 1)

    @pl.when(n_full > 0)
    def _():
        for h in heads:
            scores(0, False, h)

    def two_pairs(k, carry):
        pair(2 * k, False)
        pair(2 * k + 1, False)
        return carry

    n_plain = jnp.maximum(n_full - 1, 0)
    lax.fori_loop(0, n_plain // 2, two_pairs, 0)

    @pl.when(n_plain % 2 == 1)
    def _():
        pair(n_plain - 1, False)

    @pl.when(n_full > 0)
    def _():
        pair(n_full - 1, True)

    @pl.when(n_full == 0)
    def _():
        for h in heads:
            scores(n_full, True, h)

    for h in heads:
        accumulate(n_full, h)

    acc = acc_ref[...]
    o_slc = acc[0:HEAD_DIM, :] / acc[HEAD_DIM:HEAD_DIM + 1, :]
    gates = gtt_ref[...]
    o_t = jnp.concatenate([gates[3 * h + 1:3 * h + 2, :] * o_slc[:, h * tq:(h + 1) * tq]
                           for h in range(HEADS_PER_GROUP)], axis=0)
    o_ref[...] = (opt_ref[...] + o_t).T.astype(BF16)


def _select(qt, bias_t, ksel, vsel_t, gates_t, o_part_t):
    batch, _, _, seq = qt.shape
    tq = TQ_SEL
    cols = HEADS_PER_GROUP * tq
    n_chunks = seq // TK_SEL
    tile_t = lambda r: pl.BlockSpec((None, None, r, tq), lambda b, g, i: (b, g, 0, i))
    return pl.pallas_call(
        _select_body,
        grid=(batch, N_KV_HEADS, seq // tq),
        in_specs=[
            pl.BlockSpec((None, HEADS_PER_GROUP, HEAD_DIM, tq), lambda b, g, i: (b, g, 0, i)),
            tile_t(MAX_SEL_BLOCKS),
            pl.BlockSpec((None, None, seq, 2 * LANES), lambda b, g, i: (b, g, 0, 0)),
            pl.BlockSpec((None, None, n_chunks, V_ROWS, TK_SEL), lambda b, g, i: (b, g, 0, 0, 0)),
            tile_t(GATE_ROWS), tile_t(HEADS_PER_GROUP * HEAD_DIM),
        ],
        out_specs=pl.BlockSpec((None, tq, HEADS_PER_GROUP * HEAD_DIM), lambda b, g, i: (b, i, g)),
        out_shape=jax.ShapeDtypeStruct((batch, seq, NSA_WIDTH), BF16),
        scratch_shapes=([pltpu.VMEM((2 * LANES, cols), BF16), pltpu.VMEM((1, cols), F32), pltpu.VMEM((V_ROWS, cols), F32)]
                        + [pltpu.VMEM((2, TK_SEL, tq), F32)] * HEADS_PER_GROUP
                        + [pltpu.VMEM((2, 1, tq), F32)] * HEADS_PER_GROUP),
        compiler_params=_params(3),
        name="select",
    )(qt, bias_t, ksel, vsel_t, gates_t, o_part_t)


def _s5_body(u_ref, bt_ref, pw_ref, pwb_ref, a1_ref, ltri_ref, ct_ref, d_ref, wg_ref, bg_ref, o_ref, c_ref, x_ref, h_ref):
    @pl.when(pl.program_id(1) == 0)
    def _():
        c_ref[...] = jnp.zeros(c_ref.shape, F32)

    n = S5_COLS // S5_SPLIT
    u = u_ref[...]
    last = S5_CHUNK - 1
    ys = []
    for hf in range(S5_SPLIT):
        cs = slice(hf * n, (hf + 1) * n)
        p_re, p_im, q_re, q_im = pwb_ref[0, :, cs], pwb_ref[1, :, cs], pwb_ref[2, :, cs], pwb_ref[3, :, cs]
        pl_re, pl_im = pw_ref[0, last:last + 1, cs], pw_ref[1, last:last + 1, cs]
        a_re, a_im = a1_ref[0:1, cs], a1_ref[1:2, cs]
        width = S5_WIDTH // S5_SPLIT
        bu = _dot(u[:, hf * width:(hf + 1) * width].astype(BF16), bt_ref[hf])
        for j in range(u.shape[0] // S5_CHUNK):
            r = slice(j * S5_CHUNK, (j + 1) * S5_CHUNK)
            b_re, b_im = bu[r, 0:n].astype(BF16), bu[r, n:2 * n].astype(BF16)
            x_ref[r, 0:n] = q_re * b_re - q_im * b_im
            x_ref[r, n:2 * n] = q_re * b_im + q_im * b_re
        local = _dot(ltri_ref[...], x_ref[...])
        c_re, c_im = c_ref[0:1, cs], c_ref[1:2, cs]
        for j in range(u.shape[0] // S5_CHUNK):
            r = slice(j * S5_CHUNK, (j + 1) * S5_CHUNK)
            s_re, s_im = local[r, 0:n] + c_re, local[r, n:2 * n] + c_im
            t_re, t_im = s_re.astype(BF16), s_im.astype(BF16)
            h_ref[r, 0:n] = p_re * t_re - p_im * t_im
            h_ref[r, n:2 * n] = p_re * t_im + p_im * t_re
            e_re, e_im = s_re[last:last + 1], s_im[last:last + 1]
            l_re = pl_re * e_re - pl_im * e_im
            l_im = pl_re * e_im + pl_im * e_re
            c_re = a_re * l_re - a_im * l_im
            c_im = a_re * l_im + a_im * l_re
        c_ref[0:1, cs] = c_re
        c_ref[1:2, cs] = c_im
        ys.append(_dot(h_ref[...], ct_ref[hf]))
    y = _gelu_tanh(jnp.concatenate(ys, axis=1) + d_ref[...] * u)
    o_ref[...] = (y * jax.nn.sigmoid(_dot(y.astype(BF16), wg_ref[...]) + bg_ref[...])).astype(BF16)


def _s5(u, bt, powers, a1, ltri, ct, d, w_glu, b_glu):
    batch, seq, _ = u.shape
    ts = S5_TILE
    tile = pl.BlockSpec((None, ts, S5_WIDTH), lambda b, j: (b, j, 0))
    cols = 2 * S5_COLS // S5_SPLIT
    return pl.pallas_call(
        _s5_body,
        grid=(batch, seq // ts),
        in_specs=[tile] + [_resident()] * 9,
        out_specs=tile,
        out_shape=jax.ShapeDtypeStruct((batch, seq, S5_WIDTH), BF16),
        scratch_shapes=[pltpu.VMEM((2, S5_COLS), F32), pltpu.VMEM((ts, cols), BF16), pltpu.VMEM((ts, cols), BF16)],
        compiler_params=_params(2),
        name="s5",
    )(u, bt, powers, powers.astype(BF16), a1, ltri, ct, d, w_glu, b_glu)


def _ffn_out_body(*refs, final_norm):
    (h_ref, on_ref, os_ref, won_ref, wos_ref, g2_ref, w1_ref, w3_ref, w2_ref,
     gp_ref, wg_ref, p_ref, wp_ref) = refs[:13]
    o_ref = refs[-1]
    h = h_ref[...] + _dot(on_ref[...], won_ref[...]) + _dot(os_ref[...], wos_ref[...])
    h = _swiglu_half_step(h, g2_ref, w1_ref, w3_ref, w2_ref)
    gate = jax.nn.sigmoid(_dot(_rmsnorm(h, gp_ref[...]).astype(BF16), wg_ref[...]))
    h = h + gate * _dot(p_ref[...].astype(BF16), wp_ref[...])
    o_ref[...] = _rmsnorm(h, refs[13][...]) if final_norm else h


def _ffn_out(h, o_nsa, o_s5, wo_n, wo_s, g_ffn, w1, w3, w2, g_ple, w_gate, p, w_ple, g_final=None):
    t = h.shape[0]
    tm = min(TM_DENSE, t)
    row = lambda w: pl.BlockSpec((tm, w), lambda i: (i, 0))
    args = [h, o_nsa, o_s5, wo_n, wo_s, g_ffn, w1, w3, w2, g_ple, w_gate, p, w_ple]
    specs = [row(D_MODEL), row(NSA_WIDTH), row(S5_WIDTH)] + [_resident()] * 8 + [row(PLE_DIM), _resident()]
    if g_final is not None:
        args.append(g_final)
        specs.append(_resident())
    return pl.pallas_call(
        functools.partial(_ffn_out_body, final_norm=g_final is not None),
        grid=(t // tm,),
        in_specs=specs,
        out_specs=row(D_MODEL),
        out_shape=jax.ShapeDtypeStruct((t, D_MODEL), F32),
        compiler_params=_params(1),
        name="ffn_out",
    )(*args)


def _rearrange_w_in(w_in):
    g0 = NSA_WIDTH + 6 * KV_WIDTH
    per_group = 3 * HEADS_PER_GROUP
    pad = jnp.zeros((D_MODEL, LANES - per_group), w_in.dtype)
    gate_cols = []
    for g in range(N_KV_HEADS):
        gate_cols += [w_in[:, g0 + g * per_group:g0 + (g + 1) * per_group], pad]
    u0 = g0 + 3 * N_HEADS
    return jnp.concatenate([w_in[:, :g0], w_in[:, u0:]] + gate_cols, axis=1).astype(BF16)


def _rope_tables(seq):
    half = HEAD_DIM // 2
    inv = ROPE_THETA ** (-jnp.arange(half, dtype=F32) / half)
    ang = jnp.arange(seq, dtype=F32)[:, None] * inv[None, :]
    cos = jnp.tile(jnp.cos(ang), (1, LANES // half))
    sin = jnp.tile(jnp.concatenate([-jnp.sin(ang), jnp.sin(ang)], axis=1), (1, LANES // HEAD_DIM))
    return cos, sin


def _compress_weights(pe, w1, w2):
    w = w1.reshape(2, CMP_STRIDE, HEAD_DIM, CMP_HIDDEN).astype(BF16)
    zero = jnp.zeros_like(w)
    rows = [jnp.concatenate([w if e == g else zero for e in range(N_KV_HEADS)], axis=-1) for g in range(N_KV_HEADS)]
    big = jnp.stack(rows, axis=2).reshape(2, CMP_STRIDE * KV_WIDTH, N_KV_HEADS * CMP_HIDDEN)
    pe2 = jnp.broadcast_to(pe.reshape(2, CMP_STRIDE, 1, HEAD_DIM), (2, CMP_STRIDE, N_KV_HEADS, HEAD_DIM))
    return pe2.reshape(2, CMP_STRIDE * KV_WIDTH), big[0], big[1], w2.astype(BF16)


def _overlap_matrix(n_cmp_rows):
    c_start = np.arange(n_cmp_rows)[:, None] * CMP_STRIDE
    s_start = np.arange(MAX_SEL_BLOCKS)[None, :] * SEL_BLOCK
    ov = (c_start < s_start + SEL_BLOCK) & (c_start + CMP_LEN > s_start)
    return jnp.asarray(ov.T, dtype=BF16)


def _s5_tables(a_re, a_im, log_dt, b_re, b_im, c_re, c_im):
    dt = jnp.exp(log_dt)[:, None]
    lr, li = a_re * dt, a_im * dt
    mag = jnp.exp(lr)
    ab_re, ab_im = mag * jnp.cos(li), mag * jnp.sin(li)
    den = a_re * a_re + a_im * a_im
    nr, ni = ab_re - 1.0, ab_im
    bc_re = (nr * a_re + ni * a_im) / den
    bc_im = (ni * a_re - nr * a_im) / den
    bt_re = b_re * bc_re[..., None] - b_im * bc_im[..., None]
    bt_im = b_re * bc_im[..., None] + b_im * bc_re[..., None]
    gh = S5_GROUPS // S5_SPLIT
    eye = jnp.eye(gh, dtype=F32)
    split = lambda w: w.reshape((S5_SPLIT, gh) + w.shape[1:])
    blockdiag_in = lambda w: jnp.einsum("hgni,ge->hgien", split(w), eye).reshape(S5_SPLIT, gh * S5_GROUP, gh * S5_STATE)
    bt = jnp.concatenate([blockdiag_in(bt_re), blockdiag_in(bt_im)], axis=2).astype(BF16)
    blockdiag_out = lambda w: jnp.einsum("hgon,ge->hgneo", split(w), eye).reshape(S5_SPLIT, gh * S5_STATE, gh * S5_GROUP)
    ct = jnp.concatenate([blockdiag_out(c_re), -blockdiag_out(c_im)], axis=1).astype(BF16)
    k = jnp.arange(S5_CHUNK, dtype=F32)[:, None]
    lr, li = lr.reshape(1, S5_COLS), li.reshape(1, S5_COLS)
    powers = jnp.stack([jnp.exp(k * lr) * jnp.cos(k * li), jnp.exp(k * lr) * jnp.sin(k * li),
                        jnp.exp(-k * lr) * jnp.cos(k * li), -jnp.exp(-k * lr) * jnp.sin(k * li)])
    a1 = jnp.concatenate([ab_re.reshape(1, S5_COLS), ab_im.reshape(1, S5_COLS)], axis=0)
    ltri = jnp.asarray(np.kron(np.eye(S5_TILE // S5_CHUNK), np.tril(np.ones((S5_CHUNK, S5_CHUNK)))), dtype=BF16)
    return bt, powers, a1, ltri, ct


def kernel(x, p, norm_ffn1, ffn1_w1, ffn1_w3, ffn1_w2, norm_mix, w_in, cmp_pe_k, cmp_pe_v, cmp_wk1, cmp_wk2,
           cmp_wv1, cmp_wv2, s5_a_re, s5_a_im, s5_log_dt, s5_b_re, s5_b_im, s5_c_re, s5_c_im, s5_d, s5_w_glu,
           s5_b_glu, w_out, norm_ffn2, ffn2_w1, ffn2_w3, ffn2_w2, norm_ple, w_ple_gate, w_ple, norm_final):
    batch, seq, _ = x.shape
    depth = p.shape[0]
    t = batch * seq
    assert seq % TK_SEL == 0 and TK_SEL % TQ_SEL == 0 and seq % TQ_SEL == 0 and seq % S5_TILE == 0 and seq >= WINDOW + TQ and seq % (CMP_STRIDE * CMP_STEP) == 0
    assert seq // SEL_BLOCK <= MAX_SEL_BLOCKS and seq // SEL_BLOCK >= SEL_TOPK
    bf = lambda w: w.astype(BF16)
    row = lambda v: v.reshape(1, -1)

    cos, sin_signed = _rope_tables(seq)
    overlap_t = _overlap_matrix(seq // CMP_STRIDE)
    h = x.reshape(t, D_MODEL)
    for i in range(depth):
        h, qt, kc, vc, ksel, vsel_t, kwin, vwin_t, gates_t, u = _ffn_proj(
            h, row(norm_ffn1[i]), bf(ffn1_w1[i]), bf(ffn1_w3[i]), bf(ffn1_w2[i]),
            row(norm_mix[i]), _rearrange_w_in(w_in[i]), cos, sin_signed, batch, seq)
        pe_k2, wkt, wkb, wk2 = _compress_weights(cmp_pe_k[i], cmp_wk1[i], cmp_wk2[i])
        pe_v2, wvt, wvb, wv2 = _compress_weights(cmp_pe_v[i], cmp_wv1[i], cmp_wv2[i])
        kcmp, vcmp_t = _compress(kc, vc, pe_k2, pe_v2, wkt, wkb, wvt, wvb, wk2, wv2)
        o_part_t, bias_t = _cmpwin(qt, kcmp, vcmp_t, overlap_t, kwin, vwin_t, gates_t)
        o_nsa = _select(qt, bias_t, ksel, vsel_t, gates_t, o_part_t)

        bt, powers, a1, ltri, ct = _s5_tables(s5_a_re[i], s5_a_im[i], s5_log_dt[i], s5_b_re[i], s5_b_im[i],
                                              s5_c_re[i], s5_c_im[i])
        o_s5 = _s5(u, bt, powers, a1, ltri, ct, row(s5_d[i]), bf(s5_w_glu[i]), row(s5_b_glu[i]))

        h = _ffn_out(h, o_nsa.reshape(t, NSA_WIDTH), o_s5.reshape(t, S5_WIDTH),
                     bf(w_out[i][:NSA_WIDTH]), bf(w_out[i][NSA_WIDTH:]),
                     row(norm_ffn2[i]), bf(ffn2_w1[i]), bf(ffn2_w3[i]), bf(ffn2_w2[i]),
                     row(norm_ple[i]), bf(w_ple_gate[i]), p[i].reshape(t, PLE_DIM), bf(w_ple[i]),
                     row(norm_final) if i == depth - 1 else None)
    return h.reshape(batch, seq, D_MODEL)
```

```python
import functools

import jax
import jax.numpy as jnp
import numpy as np
from jax import lax
from jax.experimental import pallas as pl
from jax.experimental.pallas import tpu as pltpu

D_MODEL = 1024
PLE_DIM = 256
D_FF = 2816
N_HEADS = 8
N_KV_HEADS = 2
HEAD_DIM = 64
HEADS_PER_GROUP = N_HEADS // N_KV_HEADS
NSA_WIDTH = N_HEADS * HEAD_DIM
KV_WIDTH = N_KV_HEADS * HEAD_DIM
S5_WIDTH = D_MODEL - NSA_WIDTH
S5_GROUP = 16
S5_GROUPS = S5_WIDTH // S5_GROUP
S5_STATE = 64
S5_COLS = S5_GROUPS * S5_STATE
CMP_LEN = 32
CMP_STRIDE = 16
CMP_HIDDEN = 256
SEL_BLOCK = 64
SEL_TOPK = 16
WINDOW = 512
ROPE_THETA = 10000.0
RMS_EPS = 1e-6
NEG = -1e30
BIG = 1e9

LANES = 128
V_ROWS = HEAD_DIM + 16
GATE_ROWS = 16
MAX_SEL_BLOCKS = LANES
VMEM_LIMIT = 52 * 1024 * 1024

TM_DENSE = 512
FF_CHUNK = 256
TQ = 256
WIN_SUB = 128
CMP_STEP = 128
TQ_SEL = 512
TK_SEL = 512
S5_TILE = 256
S5_CHUNK = 64
S5_SPLIT = 2

F32 = jnp.float32
BF16 = jnp.bfloat16


def _dot(a, b):
    return jnp.dot(a, b, preferred_element_type=F32)


def _rmsnorm(x, g):
    ms = jnp.mean(x * x, axis=-1, keepdims=True)
    return x * lax.rsqrt(ms + RMS_EPS) * g


def _gelu_tanh(x):
    return 0.5 * x * (1.0 + jnp.tanh(np.sqrt(2.0 / np.pi).astype(np.float32) * (x + 0.044715 * (x * x * x))))


def _params(n_grid):
    return pltpu.CompilerParams(dimension_semantics=("arbitrary",) * n_grid, vmem_limit_bytes=VMEM_LIMIT)


def _resident():
    return pl.BlockSpec(memory_space=pltpu.VMEM)


def _swiglu_half_step(h, g_ref, w1_ref, w3_ref, w2_ref):
    xn = _rmsnorm(h, g_ref[...]).astype(BF16)
    acc = jnp.zeros(h.shape, F32)
    for j in range(D_FF // FF_CHUNK):
        sl = slice(j * FF_CHUNK, (j + 1) * FF_CHUNK)
        a = _dot(xn, w1_ref[:, sl])
        b = _dot(xn, w3_ref[:, sl])
        act = (a * jax.nn.sigmoid(a) * b).astype(BF16)
        acc = acc + _dot(act, w2_ref[sl, :])
    return h + 0.5 * acc


_C_Q = 0
_C_CMP = NSA_WIDTH
_C_SLC = _C_CMP + 2 * KV_WIDTH
_C_WIN = _C_SLC + 2 * KV_WIDTH
_C_U = _C_WIN + 2 * KV_WIDTH
_C_G = _C_U + S5_WIDTH
_C_END = _C_G + N_KV_HEADS * LANES


def _rope_pair(x, cos, sin_signed, first_half):
    fwd = pltpu.roll(x, HEAD_DIM // 2, 1)
    bwd = pltpu.roll(x, LANES - HEAD_DIM // 2, 1)
    return x * cos + jnp.where(first_half, bwd, fwd) * sin_signed


def _ffn_proj_body(x_ref, g1_ref, w1_ref, w3_ref, w2_ref, g_ref, w_ref, cos_ref, sin_ref,
                   h_ref, qt_ref, kc_ref, vc_ref, ks_ref, vst_ref, kw_ref, vwt_ref, gtt_ref, u_ref, kv_ref,
                   *, tiles_per_seq):
    tm = x_ref.shape[0]
    h = _swiglu_half_step(x_ref[...], g1_ref, w1_ref, w3_ref, w2_ref)
    h_ref[...] = h
    xn = _rmsnorm(h, g_ref[...]).astype(BF16)
    cos = cos_ref[...]
    sin = sin_ref[...]
    lane = lax.broadcasted_iota(jnp.int32, (tm, LANES), 1)
    first_half = (lane % HEAD_DIM) < (HEAD_DIM // 2)
    rope = lambda x: _rope_pair(x, cos, sin, first_half)

    scale = HEAD_DIM ** -0.5
    for j in range(NSA_WIDTH // 256):
        z = _dot(xn, w_ref[:, _C_Q + 256 * j:_C_Q + 256 * (j + 1)])
        for k in range(2):
            rt = (rope(z[:, LANES * k:LANES * (k + 1)]) * scale).T.astype(BF16)
            for e in range(2):
                qt_ref[4 * j + 2 * k + e] = rt[e * HEAD_DIM:(e + 1) * HEAD_DIM, :]

    z = _dot(xn, w_ref[:, _C_CMP:_C_CMP + 256])
    kv_ref[0] = rope(z[:, :LANES])
    kv_ref[1] = z[:, LANES:]
    for j in range(CMP_STRIDE):
        every = pl.ds(j, tm // CMP_STRIDE, stride=CMP_STRIDE)
        kc_ref[:, j * KV_WIDTH:(j + 1) * KV_WIDTH] = kv_ref[0, every, :]
        vc_ref[:, j * KV_WIDTH:(j + 1) * KV_WIDTH] = kv_ref[1, every, :]

    zeros = jnp.zeros((tm, HEAD_DIM), BF16)
    pos = (pl.program_id(0) % tiles_per_seq) * tm + lax.broadcasted_iota(jnp.int32, (tm, LANES), 0)
    onehot = jnp.where(lane == pos // SEL_BLOCK, 1.0, 0.0).astype(BF16)

    z = _dot(xn, w_ref[:, _C_SLC:_C_SLC + 256])
    k = rope(z[:, :LANES]).astype(BF16)
    vt = z[:, LANES:].T.astype(BF16)
    for g in range(N_KV_HEADS):
        gs = slice(g * HEAD_DIM, (g + 1) * HEAD_DIM)
        ks_ref[g, :, 0:LANES] = onehot
        ks_ref[g, :, LANES:LANES + HEAD_DIM] = k[:, gs]
        ks_ref[g, :, LANES + HEAD_DIM:] = zeros
        vst_ref[g, 0:HEAD_DIM, :] = vt[gs, :]
        vst_ref[g, HEAD_DIM:, :] = jnp.ones((V_ROWS - HEAD_DIM, tm), BF16)

    z = _dot(xn, w_ref[:, _C_WIN:_C_WIN + 256])
    k = rope(z[:, :LANES]).astype(BF16)
    vt = z[:, LANES:].T.astype(BF16)
    for g in range(N_KV_HEADS):
        gs = slice(g * HEAD_DIM, (g + 1) * HEAD_DIM)
        kw_ref[g] = k[:, gs]
        for c in range(tm // WIN_SUB):
            vwt_ref[g, c, 0:HEAD_DIM, :] = vt[gs, c * WIN_SUB:(c + 1) * WIN_SUB]
            vwt_ref[g, c, HEAD_DIM:, :] = jnp.ones((V_ROWS - HEAD_DIM, WIN_SUB), BF16)

    for j in range(S5_WIDTH // 256):
        u_ref[:, 256 * j:256 * (j + 1)] = _dot(xn, w_ref[:, _C_U + 256 * j:_C_U + 256 * (j + 1)])
    gates_t = jax.nn.sigmoid(_dot(xn, w_ref[:, _C_G:_C_END])).T
    for g in range(N_KV_HEADS):
        gtt_ref[g] = gates_t[g * LANES:g * LANES + GATE_ROWS, :]


def _ffn_proj(x, g_ffn, w1, w3, w2, g, w_in_r, cos, sin_signed, batch, seq):
    t = batch * seq
    tm = TK_SEL
    n = seq // tm
    bl = lambda w: pl.BlockSpec((None, tm, w), lambda i: (i // n, i % n, 0))
    bgl = lambda w: pl.BlockSpec((None, N_KV_HEADS, tm, w), lambda i: (i // n, 0, i % n, 0))
    tab = pl.BlockSpec((tm, LANES), lambda i: (i % n, 0))
    row = pl.BlockSpec((tm, D_MODEL), lambda i: (i, 0))
    strided = pl.BlockSpec((None, tm // CMP_STRIDE, CMP_STRIDE * KV_WIDTH), lambda i: (i // n, i % n, 0))
    sds = jax.ShapeDtypeStruct
    return pl.pallas_call(
        functools.partial(_ffn_proj_body, tiles_per_seq=n),
        grid=(t // tm,),
        in_specs=[row] + [_resident()] * 6 + [tab, tab],
        out_specs=[
            row,
            pl.BlockSpec((None, N_HEADS, HEAD_DIM, tm), lambda i: (i // n, 0, 0, i % n)),
            strided, strided,
            bgl(2 * LANES),
            pl.BlockSpec((None, N_KV_HEADS, None, V_ROWS, tm), lambda i: (i // n, 0, i % n, 0, 0)),
            bgl(HEAD_DIM),
            pl.BlockSpec((None, N_KV_HEADS, tm // WIN_SUB, V_ROWS, WIN_SUB), lambda i: (i // n, 0, i % n, 0, 0)),
            pl.BlockSpec((None, N_KV_HEADS, GATE_ROWS, tm), lambda i: (i // n, 0, 0, i % n)),
            bl(S5_WIDTH),
        ],
        out_shape=[
            sds((t, D_MODEL), F32),
            sds((batch, N_HEADS, HEAD_DIM, seq), BF16),
            sds((batch, seq // CMP_STRIDE, CMP_STRIDE * KV_WIDTH), F32),
            sds((batch, seq // CMP_STRIDE, CMP_STRIDE * KV_WIDTH), F32),
            sds((batch, N_KV_HEADS, seq, 2 * LANES), BF16), sds((batch, N_KV_HEADS, n, V_ROWS, tm), BF16),
            sds((batch, N_KV_HEADS, seq, HEAD_DIM), BF16), sds((batch, N_KV_HEADS, seq // WIN_SUB, V_ROWS, WIN_SUB), BF16),
            sds((batch, N_KV_HEADS, GATE_ROWS, seq), F32), sds((batch, seq, S5_WIDTH), F32),
        ],
        scratch_shapes=[pltpu.VMEM((2, tm, KV_WIDTH), F32)],
        compiler_params=_params(1),
        name="ffn_proj",
    )(x, g_ffn, w1, w3, w2, g, w_in_r, cos, sin_signed)


def _compress_body(kc_ref, vc_ref, pek_ref, pev_ref, wkt_ref, wkb_ref, wvt_ref, wvb_ref, wk2_ref, wv2_ref,
                   ko_ref, vot_ref, v_ref):
    nc = kc_ref.shape[0]

    def hidden(x, pe_ref, wt_ref, wb_ref):
        top = _dot((x + pe_ref[0:1, :]).astype(BF16), wt_ref[...])
        bot = _dot((x + pe_ref[1:2, :]).astype(BF16), wb_ref[...])
        pre = top + pltpu.roll(bot, nc - 1, 0)
        return _gelu_tanh(pre).astype(BF16)

    hk = hidden(kc_ref[...], pek_ref, wkt_ref, wkb_ref)
    hv = hidden(vc_ref[...], pev_ref, wvt_ref, wvb_ref)
    for g in range(N_KV_HEADS):
        gs = slice(g * CMP_HIDDEN, (g + 1) * CMP_HIDDEN)
        ko_ref[g] = _dot(hk[:, gs], wk2_ref[...]).astype(BF16)
        v_ref[...] = _dot(hv[:, gs], wv2_ref[...])
        vot_ref[g] = v_ref[...].T.astype(BF16)


def _compress(kc, vc, pe_k2, pe_v2, wkt, wkb, wvt, wvb, wk2, wv2):
    batch, nc, width = kc.shape
    x_spec = pl.BlockSpec((None, nc, width), lambda b: (b, 0, 0))
    sds = jax.ShapeDtypeStruct
    return pl.pallas_call(
        _compress_body,
        grid=(batch,),
        in_specs=[x_spec, x_spec] + [_resident()] * 8,
        out_specs=[pl.BlockSpec((None, N_KV_HEADS, nc, HEAD_DIM), lambda b: (b, 0, 0, 0)),
                   pl.BlockSpec((None, N_KV_HEADS, HEAD_DIM, nc), lambda b: (b, 0, 0, 0))],
        out_shape=[sds((batch, N_KV_HEADS, nc, HEAD_DIM), BF16), sds((batch, N_KV_HEADS, HEAD_DIM, nc), BF16)],
        scratch_shapes=[pltpu.VMEM((nc, HEAD_DIM), F32)],
        compiler_params=_params(1),
        name="compress",
    )(kc, vc, pe_k2, pe_v2, wkt, wkb, wvt, wvb, wk2, wv2)


def _cmpwin_body(qt_ref, kc_ref, vct_ref, ovt_ref, edge_ref, kw_ref, vwt_ref, gtt_ref, opt_ref, bias_ref,
                 ocmp_ref, imp_ref):
    tq = qt_ref.shape[2]
    cols = HEADS_PER_GROUP * tq
    nc = kc_ref.shape[0]
    i = pl.program_id(2)
    q_t = jnp.concatenate([qt_ref[h] for h in range(HEADS_PER_GROUP)], axis=1)
    t_lane = i * tq + lax.broadcasted_iota(jnp.int32, (1, cols), 1) % tq

    def compressed(n):
        s = _dot(kc_ref[0:n, :], q_t)
        c_end = lax.broadcasted_iota(jnp.int32, (n, cols), 0) * CMP_STRIDE + (CMP_LEN - 1)
        s = jnp.where(c_end <= t_lane, s, NEG)
        e = jnp.exp(s - jnp.max(s, axis=0, keepdims=True))
        den = jnp.maximum(jnp.sum(e, axis=0, keepdims=True), 1e-30)
        p = e * jnp.where(t_lane >= CMP_LEN - 1, 1.0 / den, 0.0)
        ocmp_ref[...] = _dot(vct_ref[:, 0:n], p.astype(BF16))
        psum = p[:, 0:tq]
        for h in range(1, HEADS_PER_GROUP):
            psum = psum + p[:, h * tq:(h + 1) * tq]
        p_hi = psum.astype(BF16)
        p_lo = (psum - p_hi.astype(F32)).astype(BF16)
        imp_ref[...] = _dot(ovt_ref[:, 0:n], p_hi) + _dot(ovt_ref[:, 0:n], p_lo)

    finished = (i * tq + tq - CMP_LEN) // CMP_STRIDE + 1
    n_steps = nc // CMP_STEP
    for k in range(1, n_steps + 1):
        in_step = finished > (k - 1) * CMP_STEP
        if k < n_steps:
            in_step = jnp.logical_and(in_step, finished <= k * CMP_STEP)
        pl.when(in_step)(functools.partial(compressed, k * CMP_STEP))
    o_cmp = ocmp_ref[...]
    imp = imp_ref[...]

    blk = lax.broadcasted_iota(jnp.int32, (MAX_SEL_BLOCKS, tq), 0)
    behind = t_lane[:, 0:tq] // SEL_BLOCK - blk
    score = jnp.where(blk == 0, BIG, jnp.where(behind == 0, BIG, jnp.where(behind == 1, BIG, imp)))
    score = jnp.where(behind >= 0, score, -BIG)
    chosen = jnp.zeros((MAX_SEL_BLOCKS, tq), F32)
    for _ in range(SEL_TOPK):
        best = jnp.max(score, axis=0, keepdims=True)
        first = jnp.min(jnp.where(score == best, blk, MAX_SEL_BLOCKS), axis=0, keepdims=True)
        hit = blk == first
        chosen = jnp.where(hit, 1.0, chosen)
        score = jnp.where(hit, -jnp.inf, score)
    bias_ref[...] = jnp.where(chosen > 0.0, jnp.where(behind >= 0, 0.0, NEG), NEG).astype(BF16)

    span = WINDOW + tq
    start = pl.multiple_of(jnp.maximum(i * tq - WINDOW, 0), tq)
    edge = edge_ref[jnp.minimum(i, edge_ref.shape[0] - 1)]
    s = edge + _dot(kw_ref[pl.ds(start, span), :], q_t)
    e = jnp.exp(s - jnp.max(s, axis=0, keepdims=True)).astype(BF16)
    c0 = start // WIN_SUB
    vw_t = jnp.concatenate([vwt_ref[c0 + j] for j in range(span // WIN_SUB)], axis=1)
    acc = _dot(vw_t, e)
    o_win = acc[0:HEAD_DIM, :] / acc[HEAD_DIM:HEAD_DIM + 1, :]

    gates = gtt_ref[...]
    for h in range(HEADS_PER_GROUP):
        hs = slice(h * tq, (h + 1) * tq)
        opt_ref[h * HEAD_DIM:(h + 1) * HEAD_DIM, :] = (gates[3 * h:3 * h + 1, :] * o_cmp[:, hs]
                                                       + gates[3 * h + 2:3 * h + 3, :] * o_win[:, hs])


def _window_edge():
    r = np.arange(WINDOW + TQ)[None, :, None]
    c = np.arange(HEADS_PER_GROUP * TQ)[None, None, :] % TQ
    offset = np.minimum(np.arange(WINDOW // TQ + 1) * TQ, WINDOW)[:, None, None]
    dist = offset + c - r
    return jnp.asarray(np.where((dist >= 0) & (dist < WINDOW), 0.0, NEG), dtype=F32)


def _cmpwin(qt, kcmp, vcmp_t, overlap_t, kwin, vwin_t, gates_t):
    batch, _, _, seq = qt.shape
    nc = kcmp.shape[2]
    tq = TQ
    per_bg = lambda *shape: pl.BlockSpec((None, None) + shape, lambda b, g, i: (b, g) + (0,) * len(shape))
    sds = jax.ShapeDtypeStruct
    return pl.pallas_call(
        _cmpwin_body,
        grid=(batch, N_KV_HEADS, seq // tq),
        in_specs=[
            pl.BlockSpec((None, HEADS_PER_GROUP, HEAD_DIM, tq), lambda b, g, i: (b, g, 0, i)),
            per_bg(nc, HEAD_DIM), per_bg(HEAD_DIM, nc), _resident(), _resident(),
            per_bg(seq, HEAD_DIM), per_bg(seq // WIN_SUB, V_ROWS, WIN_SUB),
            pl.BlockSpec((None, None, GATE_ROWS, tq), lambda b, g, i: (b, g, 0, i)),
        ],
        out_specs=[pl.BlockSpec((None, None, HEADS_PER_GROUP * HEAD_DIM, tq), lambda b, g, i: (b, g, 0, i)),
                   pl.BlockSpec((None, None, MAX_SEL_BLOCKS, tq), lambda b, g, i: (b, g, 0, i))],
        out_shape=[sds((batch, N_KV_HEADS, HEADS_PER_GROUP * HEAD_DIM, seq), F32),
                   sds((batch, N_KV_HEADS, MAX_SEL_BLOCKS, seq), BF16)],
        scratch_shapes=[pltpu.VMEM((HEAD_DIM, HEADS_PER_GROUP * tq), F32), pltpu.VMEM((MAX_SEL_BLOCKS, tq), F32)],
        compiler_params=_params(3),
        name="cmpwin",
    )(qt, kcmp, vcmp_t, overlap_t, _window_edge(), kwin, vwin_t, gates_t)


def _select_body(qt_ref, bias_ref, ks_ref, vst_ref, causal_ref, gtt_ref, opt_ref, o_ref, qa_ref, m_ref, acc_ref, *head_refs):
    tq = qt_ref.shape[2]
    cols = HEADS_PER_GROUP * tq
    tk = TK_SEL
    i = pl.program_id(2)

    bias = bias_ref[...]
    for h in range(HEADS_PER_GROUP):
        qa_ref[0:LANES, h * tq:(h + 1) * tq] = bias
        qa_ref[LANES:LANES + HEAD_DIM, h * tq:(h + 1) * tq] = qt_ref[h]
    qa_ref[LANES + HEAD_DIM:, :] = jnp.zeros((LANES - HEAD_DIM, cols), BF16)
    m_ref[...] = jnp.full(m_ref.shape, NEG, F32)
    acc_ref[...] = jnp.zeros(acc_ref.shape, F32)

    s_refs, mx_refs = head_refs[:HEADS_PER_GROUP], head_refs[HEADS_PER_GROUP:]

    def scores(c, causal, h):
        hs = slice(h * tq, (h + 1) * tq)
        off = pl.multiple_of(c * tk, tk)
        s = _dot(ks_ref[pl.ds(off, tk), :], qa_ref[:, hs])
        if causal:
            s = causal_ref[...] + s
        s_refs[h][c % 2] = s
        mx_refs[h][c % 2] = jnp.max(s, axis=0, keepdims=True)

    def accumulate(c, h):
        hs = slice(h * tq, (h + 1) * tq)
        m_old = m_ref[:, hs]
        m_new = jnp.maximum(m_old, mx_refs[h][c % 2])
        p = jnp.exp(s_refs[h][c % 2] - m_new).astype(BF16)
        acc_ref[:, hs] = jnp.exp(m_old - m_new) * acc_ref[:, hs] + _dot(vst_ref[c], p)
        m_ref[:, hs] = m_new

    heads = range(HEADS_PER_GROUP)

    n_full = (i * tq) // tk

    def pair(c, causal):
        scores(c + 1, causal, HEADS_PER_GROUP - 1)
        for h in heads:
            accumulate(c, h)
            if h > 0:
                scores(c + 1, causal, h - 1)

    @pl.when(n_full > 0)
    def _():
        for h in heads:
            scores(0, False, h)

    def two_pairs(k, carry):
        pair(2 * k, False)
        pair(2 * k + 1, False)
        return carry

    n_plain = jnp.maximum(n_full - 1, 0)
    lax.fori_loop(0, n_plain // 2, two_pairs, 0)

    @pl.when(n_plain % 2 == 1)
    def _():
        pair(n_plain - 1, False)

    @pl.when(n_full > 0)
    def _():
        pair(n_full - 1, True)

    @pl.when(n_full == 0)
    def _():
        for h in heads:
            scores(n_full, True, h)

    for h in heads:
        accumulate(n_full, h)

    acc = acc_ref[...]
    o_slc = acc[0:HEAD_DIM, :] / acc[HEAD_DIM:HEAD_DIM + 1, :]
    gates = gtt_ref[...]
    o_t = jnp.concatenate([gates[3 * h + 1:3 * h + 2, :] * o_slc[:, h * tq:(h + 1) * tq]
                           for h in range(HEADS_PER_GROUP)], axis=0)
    o_ref[...] = (opt_ref[...] + o_t).T.astype(BF16)


def _select(qt, bias_t, ksel, vsel_t, gates_t, o_part_t):
    batch, _, _, seq = qt.shape
    tq = TQ_SEL
    causal = jnp.asarray(np.where(np.arange(TK_SEL)[:, None] <= np.arange(tq)[None, :], 0.0, NEG), dtype=F32)
    cols = HEADS_PER_GROUP * tq
    n_chunks = seq // TK_SEL
    tile_t = lambda r: pl.BlockSpec((None, None, r, tq), lambda b, g, i: (b, g, 0, i))
    return pl.pallas_call(
        _select_body,
        grid=(batch, N_KV_HEADS, seq // tq),
        in_specs=[
            pl.BlockSpec((None, HEADS_PER_GROUP, HEAD_DIM, tq), lambda b, g, i: (b, g, 0, i)),
            tile_t(MAX_SEL_BLOCKS),
            pl.BlockSpec((None, None, seq, 2 * LANES), lambda b, g, i: (b, g, 0, 0)),
            pl.BlockSpec((None, None, n_chunks, V_ROWS, TK_SEL), lambda b, g, i: (b, g, 0, 0, 0)),
            _resident(),
            tile_t(GATE_ROWS), tile_t(HEADS_PER_GROUP * HEAD_DIM),
        ],
        out_specs=pl.BlockSpec((None, tq, HEADS_PER_GROUP * HEAD_DIM), lambda b, g, i: (b, i, g)),
        out_shape=jax.ShapeDtypeStruct((batch, seq, NSA_WIDTH), BF16),
        scratch_shapes=([pltpu.VMEM((2 * LANES, cols), BF16), pltpu.VMEM((1, cols), F32), pltpu.VMEM((V_ROWS, cols), F32)]
                        + [pltpu.VMEM((2, TK_SEL, tq), F32)] * HEADS_PER_GROUP
                        + [pltpu.VMEM((2, 1, tq), F32)] * HEADS_PER_GROUP),
        compiler_params=_params(3),
        name="select",
    )(qt, bias_t, ksel, vsel_t, causal, gates_t, o_part_t)


def _s5_body(u_ref, bt_ref, pw_ref, pwb_ref, a1_ref, ltri_ref, ct_ref, d_ref, wg_ref, bg_ref, o_ref, c_ref, x_ref, h_ref):
    @pl.when(pl.program_id(1) == 0)
    def _():
        c_ref[...] = jnp.zeros(c_ref.shape, F32)

    n = S5_COLS // S5_SPLIT
    u = u_ref[...]
    last = S5_CHUNK - 1
    ys = []
    for hf in range(S5_SPLIT):
        cs = slice(hf * n, (hf + 1) * n)
        p_re, p_im, q_re, q_im = pwb_ref[0, :, cs], pwb_ref[1, :, cs], pwb_ref[2, :, cs], pwb_ref[3, :, cs]
        pl_re, pl_im = pw_ref[0, last:last + 1, cs], pw_ref[1, last:last + 1, cs]
        a_re, a_im = a1_ref[0:1, cs], a1_ref[1:2, cs]
        width = S5_WIDTH // S5_SPLIT
        bu = _dot(u[:, hf * width:(hf + 1) * width].astype(BF16), bt_ref[hf])
        for j in range(u.shape[0] // S5_CHUNK):
            r = slice(j * S5_CHUNK, (j + 1) * S5_CHUNK)
            b_re, b_im = bu[r, 0:n].astype(BF16), bu[r, n:2 * n].astype(BF16)
            x_ref[r, 0:n] = q_re * b_re - q_im * b_im
            x_ref[r, n:2 * n] = q_re * b_im + q_im * b_re
        local = _dot(ltri_ref[...], x_ref[...])
        c_re, c_im = c_ref[0:1, cs], c_ref[1:2, cs]
        for j in range(u.shape[0] // S5_CHUNK):
            r = slice(j * S5_CHUNK, (j + 1) * S5_CHUNK)
            s_re, s_im = local[r, 0:n] + c_re, local[r, n:2 * n] + c_im
            t_re, t_im = s_re.astype(BF16), s_im.astype(BF16)
            h_ref[r, 0:n] = p_re * t_re - p_im * t_im
            h_ref[r, n:2 * n] = p_re * t_im + p_im * t_re
            e_re, e_im = s_re[last:last + 1], s_im[last:last + 1]
            l_re = pl_re * e_re - pl_im * e_im
            l_im = pl_re * e_im + pl_im * e_re
            c_re = a_re * l_re - a_im * l_im
            c_im = a_re * l_im + a_im * l_re
        c_ref[0:1, cs] = c_re
        c_ref[1:2, cs] = c_im
        ys.append(_dot(h_ref[...], ct_ref[hf]))
    y = _gelu_tanh(jnp.concatenate(ys, axis=1) + d_ref[...] * u)
    o_ref[...] = (y * jax.nn.sigmoid(_dot(y.astype(BF16), wg_ref[...]) + bg_ref[...])).astype(BF16)


def _s5(u, bt, powers, a1, ltri, ct, d, w_glu, b_glu):
    batch, seq, _ = u.shape
    ts = S5_TILE
    tile = pl.BlockSpec((None, ts, S5_WIDTH), lambda b, j: (b, j, 0))
    cols = 2 * S5_COLS // S5_SPLIT
    return pl.pallas_call(
        _s5_body,
        grid=(batch, seq // ts),
        in_specs=[tile] + [_resident()] * 9,
        out_specs=tile,
        out_shape=jax.ShapeDtypeStruct((batch, seq, S5_WIDTH), BF16),
        scratch_shapes=[pltpu.VMEM((2, S5_COLS), F32), pltpu.VMEM((ts, cols), BF16), pltpu.VMEM((ts, cols), BF16)],
        compiler_params=_params(2),
        name="s5",
    )(u, bt, powers, powers.astype(BF16), a1, ltri, ct, d, w_glu, b_glu)


def _ffn_out_body(*refs, final_norm):
    (h_ref, on_ref, os_ref, won_ref, wos_ref, g2_ref, w1_ref, w3_ref, w2_ref,
     gp_ref, wg_ref, p_ref, wp_ref) = refs[:13]
    o_ref = refs[-1]
    h = h_ref[...] + _dot(on_ref[...], won_ref[...]) + _dot(os_ref[...], wos_ref[...])
    h = _swiglu_half_step(h, g2_ref, w1_ref, w3_ref, w2_ref)
    gate = jax.nn.sigmoid(_dot(_rmsnorm(h, gp_ref[...]).astype(BF16), wg_ref[...]))
    h = h + gate * _dot(p_ref[...].astype(BF16), wp_ref[...])
    o_ref[...] = _rmsnorm(h, refs[13][...]) if final_norm else h


def _ffn_out(h, o_nsa, o_s5, wo_n, wo_s, g_ffn, w1, w3, w2, g_ple, w_gate, p, w_ple, g_final=None):
    t = h.shape[0]
    tm = min(TM_DENSE, t)
    row = lambda w: pl.BlockSpec((tm, w), lambda i: (i, 0))
    args = [h, o_nsa, o_s5, wo_n, wo_s, g_ffn, w1, w3, w2, g_ple, w_gate, p, w_ple]
    specs = [row(D_MODEL), row(NSA_WIDTH), row(S5_WIDTH)] + [_resident()] * 8 + [row(PLE_DIM), _resident()]
    if g_final is not None:
        args.append(g_final)
        specs.append(_resident())
    return pl.pallas_call(
        functools.partial(_ffn_out_body, final_norm=g_final is not None),
        grid=(t // tm,),
        in_specs=specs,
        out_specs=row(D_MODEL),
        out_shape=jax.ShapeDtypeStruct((t, D_MODEL), F32),
        compiler_params=_params(1),
        name="ffn_out",
    )(*args)


def _rearrange_w_in(w_in):
    g0 = NSA_WIDTH + 6 * KV_WIDTH
    per_group = 3 * HEADS_PER_GROUP
    pad = jnp.zeros((D_MODEL, LANES - per_group), w_in.dtype)
    gate_cols = []
    for g in range(N_KV_HEADS):
        gate_cols += [w_in[:, g0 + g * per_group:g0 + (g + 1) * per_group], pad]
    u0 = g0 + 3 * N_HEADS
    return jnp.concatenate([w_in[:, :g0], w_in[:, u0:]] + gate_cols, axis=1).astype(BF16)


def _rope_tables(seq):
    half = HEAD_DIM // 2
    inv = ROPE_THETA ** (-jnp.arange(half, dtype=F32) / half)
    ang = jnp.arange(seq, dtype=F32)[:, None] * inv[None, :]
    cos = jnp.tile(jnp.cos(ang), (1, LANES // half))
    sin = jnp.tile(jnp.concatenate([-jnp.sin(ang), jnp.sin(ang)], axis=1), (1, LANES // HEAD_DIM))
    return cos, sin


def _compress_weights(pe, w1, w2):
    w = w1.reshape(2, CMP_STRIDE, HEAD_DIM, CMP_HIDDEN).astype(BF16)
    zero = jnp.zeros_like(w)
    rows = [jnp.concatenate([w if e == g else zero for e in range(N_KV_HEADS)], axis=-1) for g in range(N_KV_HEADS)]
    big = jnp.stack(rows, axis=2).reshape(2, CMP_STRIDE * KV_WIDTH, N_KV_HEADS * CMP_HIDDEN)
    pe2 = jnp.broadcast_to(pe.reshape(2, CMP_STRIDE, 1, HEAD_DIM), (2, CMP_STRIDE, N_KV_HEADS, HEAD_DIM))
    return pe2.reshape(2, CMP_STRIDE * KV_WIDTH), big[0], big[1], w2.astype(BF16)


def _overlap_matrix(n_cmp_rows):
    c_start = np.arange(n_cmp_rows)[:, None] * CMP_STRIDE
    s_start = np.arange(MAX_SEL_BLOCKS)[None, :] * SEL_BLOCK
    ov = (c_start < s_start + SEL_BLOCK) & (c_start + CMP_LEN > s_start)
    return jnp.asarray(ov.T, dtype=BF16)


def _s5_tables(a_re, a_im, log_dt, b_re, b_im, c_re, c_im):
    dt = jnp.exp(log_dt)[:, None]
    lr, li = a_re * dt, a_im * dt
    mag = jnp.exp(lr)
    ab_re, ab_im = mag * jnp.cos(li), mag * jnp.sin(li)
    den = a_re * a_re + a_im * a_im
    nr, ni = ab_re - 1.0, ab_im
    bc_re = (nr * a_re + ni * a_im) / den
    bc_im = (ni * a_re - nr * a_im) / den
    bt_re = b_re * bc_re[..., None] - b_im * bc_im[..., None]
    bt_im = b_re * bc_im[..., None] + b_im * bc_re[..., None]
    gh = S5_GROUPS // S5_SPLIT
    eye = jnp.eye(gh, dtype=F32)
    split = lambda w: w.reshape((S5_SPLIT, gh) + w.shape[1:])
    blockdiag_in = lambda w: jnp.einsum("hgni,ge->hgien", split(w), eye).reshape(S5_SPLIT, gh * S5_GROUP, gh * S5_STATE)
    bt = jnp.concatenate([blockdiag_in(bt_re), blockdiag_in(bt_im)], axis=2).astype(BF16)
    blockdiag_out = lambda w: jnp.einsum("hgon,ge->hgneo", split(w), eye).reshape(S5_SPLIT, gh * S5_STATE, gh * S5_GROUP)
    ct = jnp.concatenate([blockdiag_out(c_re), -blockdiag_out(c_im)], axis=1).astype(BF16)
    k = jnp.arange(S5_CHUNK, dtype=F32)[:, None]
    lr, li = lr.reshape(1, S5_COLS), li.reshape(1, S5_COLS)
    powers = jnp.stack([jnp.exp(k * lr) * jnp.cos(k * li), jnp.exp(k * lr) * jnp.sin(k * li),
                        jnp.exp(-k * lr) * jnp.cos(k * li), -jnp.exp(-k * lr) * jnp.sin(k * li)])
    a1 = jnp.concatenate([ab_re.reshape(1, S5_COLS), ab_im.reshape(1, S5_COLS)], axis=0)
    ltri = jnp.asarray(np.kron(np.eye(S5_TILE // S5_CHUNK), np.tril(np.ones((S5_CHUNK, S5_CHUNK)))), dtype=BF16)
    return bt, powers, a1, ltri, ct


def kernel(x, p, norm_ffn1, ffn1_w1, ffn1_w3, ffn1_w2, norm_mix, w_in, cmp_pe_k, cmp_pe_v, cmp_wk1, cmp_wk2,
           cmp_wv1, cmp_wv2, s5_a_re, s5_a_im, s5_log_dt, s5_b_re, s5_b_im, s5_c_re, s5_c_im, s5_d, s5_w_glu,
           s5_b_glu, w_out, norm_ffn2, ffn2_w1, ffn2_w3, ffn2_w2, norm_ple, w_ple_gate, w_ple, norm_final):
    batch, seq, _ = x.shape
    depth = p.shape[0]
    t = batch * seq
    assert seq % TK_SEL == 0 and TK_SEL == TQ_SEL and seq % S5_TILE == 0 and seq >= WINDOW + TQ and seq % (CMP_STRIDE * CMP_STEP) == 0
    assert seq // SEL_BLOCK <= MAX_SEL_BLOCKS and seq // SEL_BLOCK >= SEL_TOPK
    bf = lambda w: w.astype(BF16)
    row = lambda v: v.reshape(1, -1)

    cos, sin_signed = _rope_tables(seq)
    overlap_t = _overlap_matrix(seq // CMP_STRIDE)
    h = x.reshape(t, D_MODEL)
    for i in range(depth):
        h, qt, kc, vc, ksel, vsel_t, kwin, vwin_t, gates_t, u = _ffn_proj(
            h, row(norm_ffn1[i]), bf(ffn1_w1[i]), bf(ffn1_w3[i]), bf(ffn1_w2[i]),
            row(norm_mix[i]), _rearrange_w_in(w_in[i]), cos, sin_signed, batch, seq)
        pe_k2, wkt, wkb, wk2 = _compress_weights(cmp_pe_k[i], cmp_wk1[i], cmp_wk2[i])
        pe_v2, wvt, wvb, wv2 = _compress_weights(cmp_pe_v[i], cmp_wv1[i], cmp_wv2[i])
        kcmp, vcmp_t = _compress(kc, vc, pe_k2, pe_v2, wkt, wkb, wvt, wvb, wk2, wv2)
        o_part_t, bias_t = _cmpwin(qt, kcmp, vcmp_t, overlap_t, kwin, vwin_t, gates_t)
        o_nsa = _select(qt, bias_t, ksel, vsel_t, gates_t, o_part_t)

        bt, powers, a1, ltri, ct = _s5_tables(s5_a_re[i], s5_a_im[i], s5_log_dt[i], s5_b_re[i], s5_b_im[i],
                                              s5_c_re[i], s5_c_im[i])
        o_s5 = _s5(u, bt, powers, a1, ltri, ct, row(s5_d[i]), bf(s5_w_glu[i]), row(s5_b_glu[i]))

        h = _ffn_out(h, o_nsa.reshape(t, NSA_WIDTH), o_s5.reshape(t, S5_WIDTH),
                     bf(w_out[i][:NSA_WIDTH]), bf(w_out[i][NSA_WIDTH:]),
                     row(norm_ffn2[i]), bf(ffn2_w1[i]), bf(ffn2_w3[i]), bf(ffn2_w2[i]),
                     row(norm_ple[i]), bf(w_ple_gate[i]), p[i].reshape(t, PLE_DIM), bf(w_ple[i]),
                     row(norm_final) if i == depth - 1 else None)
    return h.reshape(batch, seq, D_MODEL)
```

```python
import functools

import jax
import jax.numpy as jnp
import numpy as np
from jax import lax
from jax.experimental import pallas as pl
from jax.experimental.pallas import tpu as pltpu

D_MODEL = 1024
PLE_DIM = 256
D_FF = 2816
N_HEADS = 8
N_KV_HEADS = 2
HEAD_DIM = 64
HEADS_PER_GROUP = N_HEADS // N_KV_HEADS
NSA_WIDTH = N_HEADS * HEAD_DIM
KV_WIDTH = N_KV_HEADS * HEAD_DIM
S5_WIDTH = D_MODEL - NSA_WIDTH
S5_GROUP = 16
S5_GROUPS = S5_WIDTH // S5_GROUP
S5_STATE = 64
S5_COLS = S5_GROUPS * S5_STATE
CMP_LEN = 32
CMP_STRIDE = 16
CMP_HIDDEN = 256
SEL_BLOCK = 64
SEL_TOPK = 16
WINDOW = 512
ROPE_THETA = 10000.0
RMS_EPS = 1e-6
NEG = -1e30
BIG = 1e9

LANES = 128
V_ROWS = HEAD_DIM + 16
GATE_ROWS = 16
MAX_SEL_BLOCKS = LANES
VMEM_LIMIT = 52 * 1024 * 1024

TM_DENSE = 512
FF_CHUNK = 256
TQ = 256
WIN_SUB = 128
CMP_STEP = 128
TQ_SEL = 512
TK_SEL = 512
S5_TILE = 256
S5_CHUNK = 64
S5_SPLIT = 2

F32 = jnp.float32
BF16 = jnp.bfloat16


def _dot(a, b):
    return jnp.dot(a, b, preferred_element_type=F32)


def _rmsnorm(x, g):
    ms = jnp.mean(x * x, axis=-1, keepdims=True)
    return x * lax.rsqrt(ms + RMS_EPS) * g


def _gelu_tanh(x):
    return 0.5 * x * (1.0 + jnp.tanh(np.sqrt(2.0 / np.pi).astype(np.float32) * (x + 0.044715 * (x * x * x))))


def _params(n_grid):
    return pltpu.CompilerParams(dimension_semantics=("arbitrary",) * n_grid, vmem_limit_bytes=VMEM_LIMIT)


def _resident():
    return pl.BlockSpec(memory_space=pltpu.VMEM)


def _swiglu_half_step(h, g_ref, w1_ref, w3_ref, w2_ref):
    xn = _rmsnorm(h, g_ref[...]).astype(BF16)
    acc = jnp.zeros(h.shape, F32)
    for j in range(D_FF // FF_CHUNK):
        sl = slice(j * FF_CHUNK, (j + 1) * FF_CHUNK)
        a = _dot(xn, w1_ref[:, sl])
        b = _dot(xn, w3_ref[:, sl])
        act = (a * jax.nn.sigmoid(a) * b).astype(BF16)
        acc = acc + _dot(act, w2_ref[sl, :])
    return h + 0.5 * acc


_C_Q = 0
_C_CMP = NSA_WIDTH
_C_SLC = _C_CMP + 2 * KV_WIDTH
_C_WIN = _C_SLC + 2 * KV_WIDTH
_C_U = _C_WIN + 2 * KV_WIDTH
_C_G = _C_U + S5_WIDTH
_C_END = _C_G + N_KV_HEADS * LANES


def _rope_pair(x, cos, sin_signed, first_half):
    fwd = pltpu.roll(x, HEAD_DIM // 2, 1)
    bwd = pltpu.roll(x, LANES - HEAD_DIM // 2, 1)
    return x * cos + jnp.where(first_half, bwd, fwd) * sin_signed


def _ffn_proj_body(x_ref, g1_ref, w1_ref, w3_ref, w2_ref, g_ref, w_ref, cos_ref, sin_ref,
                   h_ref, qt_ref, kc_ref, vc_ref, ks_ref, vst_ref, kw_ref, vwt_ref, gtt_ref, u_ref, kv_ref,
                   *, tiles_per_seq):
    tm = x_ref.shape[0]
    h = _swiglu_half_step(x_ref[...], g1_ref, w1_ref, w3_ref, w2_ref)
    h_ref[...] = h
    xn = _rmsnorm(h, g_ref[...]).astype(BF16)
    cos = cos_ref[...]
    sin = sin_ref[...]
    lane = lax.broadcasted_iota(jnp.int32, (tm, LANES), 1)
    first_half = (lane % HEAD_DIM) < (HEAD_DIM // 2)
    rope = lambda x: _rope_pair(x, cos, sin, first_half)

    scale = HEAD_DIM ** -0.5
    for j in range(NSA_WIDTH // 256):
        z = _dot(xn, w_ref[:, _C_Q + 256 * j:_C_Q + 256 * (j + 1)])
        for k in range(2):
            rt = (rope(z[:, LANES * k:LANES * (k + 1)]) * scale).T.astype(BF16)
            for e in range(2):
                qt_ref[4 * j + 2 * k + e] = rt[e * HEAD_DIM:(e + 1) * HEAD_DIM, :]

    z = _dot(xn, w_ref[:, _C_CMP:_C_CMP + 256])
    kv_ref[0] = rope(z[:, :LANES])
    kv_ref[1] = z[:, LANES:]
    for j in range(CMP_STRIDE):
        every = pl.ds(j, tm // CMP_STRIDE, stride=CMP_STRIDE)
        kc_ref[:, j * KV_WIDTH:(j + 1) * KV_WIDTH] = kv_ref[0, every, :]
        vc_ref[:, j * KV_WIDTH:(j + 1) * KV_WIDTH] = kv_ref[1, every, :]

    zeros = jnp.zeros((tm, HEAD_DIM), BF16)
    pos = (pl.program_id(0) % tiles_per_seq) * tm + lax.broadcasted_iota(jnp.int32, (tm, LANES), 0)
    onehot = jnp.where(lane == pos // SEL_BLOCK, 1.0, 0.0).astype(BF16)

    z = _dot(xn, w_ref[:, _C_SLC:_C_SLC + 256])
    k = rope(z[:, :LANES]).astype(BF16)
    vt = z[:, LANES:].T.astype(BF16)
    for g in range(N_KV_HEADS):
        gs = slice(g * HEAD_DIM, (g + 1) * HEAD_DIM)
        ks_ref[g, :, 0:LANES] = onehot
        ks_ref[g, :, LANES:LANES + HEAD_DIM] = k[:, gs]
        ks_ref[g, :, LANES + HEAD_DIM:] = zeros
        vst_ref[g, 0:HEAD_DIM, :] = vt[gs, :]
        vst_ref[g, HEAD_DIM:, :] = jnp.ones((V_ROWS - HEAD_DIM, tm), BF16)

    z = _dot(xn, w_ref[:, _C_WIN:_C_WIN + 256])
    k = rope(z[:, :LANES]).astype(BF16)
    vt = z[:, LANES:].T.astype(BF16)
    for g in range(N_KV_HEADS):
        gs = slice(g * HEAD_DIM, (g + 1) * HEAD_DIM)
        kw_ref[g] = k[:, gs]
        for c in range(tm // WIN_SUB):
            vwt_ref[g, c, 0:HEAD_DIM, :] = vt[gs, c * WIN_SUB:(c + 1) * WIN_SUB]
            vwt_ref[g, c, HEAD_DIM:, :] = jnp.ones((V_ROWS - HEAD_DIM, WIN_SUB), BF16)

    for j in range(S5_WIDTH // 256):
        u_ref[:, 256 * j:256 * (j + 1)] = _dot(xn, w_ref[:, _C_U + 256 * j:_C_U + 256 * (j + 1)])
    gates_t = jax.nn.sigmoid(_dot(xn, w_ref[:, _C_G:_C_END])).T
    for g in range(N_KV_HEADS):
        gtt_ref[g] = gates_t[g * LANES:g * LANES + GATE_ROWS, :]


def _ffn_proj(x, g_ffn, w1, w3, w2, g, w_in_r, cos, sin_signed, batch, seq):
    t = batch * seq
    tm = TK_SEL
    n = seq // tm
    bl = lambda w: pl.BlockSpec((None, tm, w), lambda i: (i // n, i % n, 0))
    bgl = lambda w: pl.BlockSpec((None, N_KV_HEADS, tm, w), lambda i: (i // n, 0, i % n, 0))
    tab = pl.BlockSpec((tm, LANES), lambda i: (i % n, 0))
    row = pl.BlockSpec((tm, D_MODEL), lambda i: (i, 0))
    strided = pl.BlockSpec((None, tm // CMP_STRIDE, CMP_STRIDE * KV_WIDTH), lambda i: (i // n, i % n, 0))
    sds = jax.ShapeDtypeStruct
    return pl.pallas_call(
        functools.partial(_ffn_proj_body, tiles_per_seq=n),
        grid=(t // tm,),
        in_specs=[row] + [_resident()] * 6 + [tab, tab],
        out_specs=[
            row,
            pl.BlockSpec((None, N_HEADS, HEAD_DIM, tm), lambda i: (i // n, 0, 0, i % n)),
            strided, strided,
            bgl(2 * LANES),
            pl.BlockSpec((None, N_KV_HEADS, None, V_ROWS, tm), lambda i: (i // n, 0, i % n, 0, 0)),
            bgl(HEAD_DIM),
            pl.BlockSpec((None, N_KV_HEADS, tm // WIN_SUB, V_ROWS, WIN_SUB), lambda i: (i // n, 0, i % n, 0, 0)),
            pl.BlockSpec((None, N_KV_HEADS, GATE_ROWS, tm), lambda i: (i // n, 0, 0, i % n)),
            bl(S5_WIDTH),
        ],
        out_shape=[
            sds((t, D_MODEL), F32),
            sds((batch, N_HEADS, HEAD_DIM, seq), BF16),
            sds((batch, seq // CMP_STRIDE, CMP_STRIDE * KV_WIDTH), F32),
            sds((batch, seq // CMP_STRIDE, CMP_STRIDE * KV_WIDTH), F32),
            sds((batch, N_KV_HEADS, seq, 2 * LANES), BF16), sds((batch, N_KV_HEADS, n, V_ROWS, tm), BF16),
            sds((batch, N_KV_HEADS, seq, HEAD_DIM), BF16), sds((batch, N_KV_HEADS, seq // WIN_SUB, V_ROWS, WIN_SUB), BF16),
            sds((batch, N_KV_HEADS, GATE_ROWS, seq), F32), sds((batch, seq, S5_WIDTH), F32),
        ],
        scratch_shapes=[pltpu.VMEM((2, tm, KV_WIDTH), F32)],
        compiler_params=_params(1),
        name="ffn_proj",
    )(x, g_ffn, w1, w3, w2, g, w_in_r, cos, sin_signed)


def _compress_body(kc_ref, vc_ref, pek_ref, pev_ref, wkt_ref, wkb_ref, wvt_ref, wvb_ref, wk2_ref, wv2_ref,
                   ko_ref, vot_ref, v_ref):
    nc = kc_ref.shape[0]

    def hidden(x, pe_ref, wt_ref, wb_ref):
        top = _dot((x + pe_ref[0:1, :]).astype(BF16), wt_ref[...])
        bot = _dot((x + pe_ref[1:2, :]).astype(BF16), wb_ref[...])
        pre = top + pltpu.roll(bot, nc - 1, 0)
        return _gelu_tanh(pre).astype(BF16)

    hk = hidden(kc_ref[...], pek_ref, wkt_ref, wkb_ref)
    hv = hidden(vc_ref[...], pev_ref, wvt_ref, wvb_ref)
    for g in range(N_KV_HEADS):
        gs = slice(g * CMP_HIDDEN, (g + 1) * CMP_HIDDEN)
        ko_ref[g] = _dot(hk[:, gs], wk2_ref[...]).astype(BF16)
        v_ref[...] = _dot(hv[:, gs], wv2_ref[...])
        vot_ref[g] = v_ref[...].T.astype(BF16)


def _compress(kc, vc, pe_k2, pe_v2, wkt, wkb, wvt, wvb, wk2, wv2):
    batch, nc, width = kc.shape
    x_spec = pl.BlockSpec((None, nc, width), lambda b: (b, 0, 0))
    sds = jax.ShapeDtypeStruct
    return pl.pallas_call(
        _compress_body,
        grid=(batch,),
        in_specs=[x_spec, x_spec] + [_resident()] * 8,
        out_specs=[pl.BlockSpec((None, N_KV_HEADS, nc, HEAD_DIM), lambda b: (b, 0, 0, 0)),
                   pl.BlockSpec((None, N_KV_HEADS, HEAD_DIM, nc), lambda b: (b, 0, 0, 0))],
        out_shape=[sds((batch, N_KV_HEADS, nc, HEAD_DIM), BF16), sds((batch, N_KV_HEADS, HEAD_DIM, nc), BF16)],
        scratch_shapes=[pltpu.VMEM((nc, HEAD_DIM), F32)],
        compiler_params=_params(1),
        name="compress",
    )(kc, vc, pe_k2, pe_v2, wkt, wkb, wvt, wvb, wk2, wv2)


def _cmpwin_body(qt_ref, kc_ref, vct_ref, ovt_ref, cmask_ref, kw_ref, vwt_ref, gtt_ref, opt_ref, bias_ref,
                 s_ref, ocmp_ref, imp_ref):
    tq = qt_ref.shape[2]
    cols = HEADS_PER_GROUP * tq
    nc = kc_ref.shape[0]
    i = pl.program_id(2)
    q_t = jnp.concatenate([qt_ref[h] for h in range(HEADS_PER_GROUP)], axis=1)
    t_lane = i * tq + lax.broadcasted_iota(jnp.int32, (1, cols), 1) % tq

    def compressed(n):
        first = (i * tq) // CMP_STRIDE
        s = cmask_ref[pl.ds(pl.multiple_of(nc - first, CMP_STRIDE), n), :] + _dot(kc_ref[0:n, :], q_t)
        e = jnp.exp(s - jnp.max(s, axis=0, keepdims=True))
        den = jnp.maximum(jnp.sum(e, axis=0, keepdims=True), 1e-30)
        p = e * jnp.where(t_lane >= CMP_LEN - 1, 1.0 / den, 0.0)
        ocmp_ref[...] = _dot(vct_ref[:, 0:n], p.astype(BF16))
        psum = p[:, 0:tq]
        for h in range(1, HEADS_PER_GROUP):
            psum = psum + p[:, h * tq:(h + 1) * tq]
        p_hi = psum.astype(BF16)
        p_lo = (psum - p_hi.astype(F32)).astype(BF16)
        imp_ref[...] = _dot(ovt_ref[:, 0:n], p_hi) + _dot(ovt_ref[:, 0:n], p_lo)

    finished = (i * tq + tq - CMP_LEN) // CMP_STRIDE + 1
    n_steps = nc // CMP_STEP
    for k in range(1, n_steps + 1):
        in_step = finished > (k - 1) * CMP_STEP
        if k < n_steps:
            in_step = jnp.logical_and(in_step, finished <= k * CMP_STEP)
        pl.when(in_step)(functools.partial(compressed, k * CMP_STEP))
    o_cmp = ocmp_ref[...]
    imp = imp_ref[...]

    blk = lax.broadcasted_iota(jnp.int32, (MAX_SEL_BLOCKS, tq), 0)
    behind = t_lane[:, 0:tq] // SEL_BLOCK - blk
    score = jnp.where(blk == 0, BIG, jnp.where(behind == 0, BIG, jnp.where(behind == 1, BIG, imp)))
    score = jnp.where(behind >= 0, score, -BIG)
    chosen = jnp.zeros((MAX_SEL_BLOCKS, tq), F32)
    for _ in range(SEL_TOPK):
        best = jnp.max(score, axis=0, keepdims=True)
        first = jnp.min(jnp.where(score == best, blk, MAX_SEL_BLOCKS), axis=0, keepdims=True)
        hit = blk == first
        chosen = jnp.where(hit, 1.0, chosen)
        score = jnp.where(hit, -jnp.inf, score)
    bias_ref[...] = jnp.where(chosen > 0.0, jnp.where(behind >= 0, 0.0, NEG), NEG).astype(BF16)

    span = WINDOW + tq
    start = pl.multiple_of(jnp.maximum(i * tq - WINDOW, 0), tq)
    s_ref[...] = _dot(kw_ref[pl.ds(start, span), :], q_t)

    def band(rows):
        kpos = start + rows.start + lax.broadcasted_iota(jnp.int32, (rows.stop - rows.start, cols), 0)
        dist = lax.bitcast_convert_type(t_lane - kpos, jnp.uint32)
        s_ref[rows, :] = jnp.where(dist < WINDOW, s_ref[rows, :], NEG)

    @pl.when(i * tq >= WINDOW)
    def _():
        band(slice(0, tq))
        band(slice(span - tq, span))

    @pl.when(i * tq < WINDOW)
    def _():
        band(slice(0, span))

    s = s_ref[...]
    e = jnp.exp(s - jnp.max(s, axis=0, keepdims=True)).astype(BF16)
    c0 = start // WIN_SUB
    vw_t = jnp.concatenate([vwt_ref[c0 + j] for j in range(span // WIN_SUB)], axis=1)
    acc = _dot(vw_t, e)
    o_win = acc[0:HEAD_DIM, :] / acc[HEAD_DIM:HEAD_DIM + 1, :]

    gates = gtt_ref[...]
    for h in range(HEADS_PER_GROUP):
        hs = slice(h * tq, (h + 1) * tq)
        opt_ref[h * HEAD_DIM:(h + 1) * HEAD_DIM, :] = (gates[3 * h:3 * h + 1, :] * o_cmp[:, hs]
                                                       + gates[3 * h + 2:3 * h + 3, :] * o_win[:, hs])


def _compressed_mask(nc):
    r = np.arange(2 * nc)[:, None] - nc
    c = np.arange(HEADS_PER_GROUP * TQ)[None, :] % TQ
    return jnp.asarray(np.where(CMP_STRIDE * r + CMP_LEN - 1 <= c, 0.0, NEG), dtype=F32)


def _cmpwin(qt, kcmp, vcmp_t, overlap_t, kwin, vwin_t, gates_t):
    batch, _, _, seq = qt.shape
    nc = kcmp.shape[2]
    tq = TQ
    per_bg = lambda *shape: pl.BlockSpec((None, None) + shape, lambda b, g, i: (b, g) + (0,) * len(shape))
    sds = jax.ShapeDtypeStruct
    return pl.pallas_call(
        _cmpwin_body,
        grid=(batch, N_KV_HEADS, seq // tq),
        in_specs=[
            pl.BlockSpec((None, HEADS_PER_GROUP, HEAD_DIM, tq), lambda b, g, i: (b, g, 0, i)),
            per_bg(nc, HEAD_DIM), per_bg(HEAD_DIM, nc), _resident(), _resident(),
            per_bg(seq, HEAD_DIM), per_bg(seq // WIN_SUB, V_ROWS, WIN_SUB),
            pl.BlockSpec((None, None, GATE_ROWS, tq), lambda b, g, i: (b, g, 0, i)),
        ],
        out_specs=[pl.BlockSpec((None, None, HEADS_PER_GROUP * HEAD_DIM, tq), lambda b, g, i: (b, g, 0, i)),
                   pl.BlockSpec((None, None, MAX_SEL_BLOCKS, tq), lambda b, g, i: (b, g, 0, i))],
        out_shape=[sds((batch, N_KV_HEADS, HEADS_PER_GROUP * HEAD_DIM, seq), F32),
                   sds((batch, N_KV_HEADS, MAX_SEL_BLOCKS, seq), BF16)],
        scratch_shapes=[pltpu.VMEM((WINDOW + tq, HEADS_PER_GROUP * tq), F32),
                        pltpu.VMEM((HEAD_DIM, HEADS_PER_GROUP * tq), F32), pltpu.VMEM((MAX_SEL_BLOCKS, tq), F32)],
        compiler_params=_params(3),
        name="cmpwin",
    )(qt, kcmp, vcmp_t, overlap_t, _compressed_mask(nc), kwin, vwin_t, gates_t)


def _select_body(qt_ref, bias_ref, ks_ref, vst_ref, causal_ref, gtt_ref, opt_ref, o_ref, qa_ref, m_ref, acc_ref, *head_refs):
    tq = qt_ref.shape[2]
    cols = HEADS_PER_GROUP * tq
    tk = TK_SEL
    i = pl.program_id(2)

    bias = bias_ref[...]
    for h in range(HEADS_PER_GROUP):
        qa_ref[0:LANES, h * tq:(h + 1) * tq] = bias
        qa_ref[LANES:LANES + HEAD_DIM, h * tq:(h + 1) * tq] = qt_ref[h]
    qa_ref[LANES + HEAD_DIM:, :] = jnp.zeros((LANES - HEAD_DIM, cols), BF16)
    m_ref[...] = jnp.full(m_ref.shape, NEG, F32)
    acc_ref[...] = jnp.zeros(acc_ref.shape, F32)

    s_refs, mx_refs = head_refs[:HEADS_PER_GROUP], head_refs[HEADS_PER_GROUP:]

    def scores(c, causal, h):
        hs = slice(h * tq, (h + 1) * tq)
        off = pl.multiple_of(c * tk, tk)
        s = _dot(ks_ref[pl.ds(off, tk), :], qa_ref[:, hs])
        if causal:
            s = causal_ref[...] + s
        s_refs[h][c % 2] = s
        mx_refs[h][c % 2] = jnp.max(s, axis=0, keepdims=True)

    def accumulate(c, h):
        hs = slice(h * tq, (h + 1) * tq)
        m_old = m_ref[:, hs]
        m_new = jnp.maximum(m_old, mx_refs[h][c % 2])
        p = jnp.exp(s_refs[h][c % 2] - m_new).astype(BF16)
        acc_ref[:, hs] = jnp.exp(m_old - m_new) * acc_ref[:, hs] + _dot(vst_ref[c], p)
        m_ref[:, hs] = m_new

    heads = range(HEADS_PER_GROUP)

    n_full = (i * tq) // tk

    def pair(c, causal):
        scores(c + 1, causal, HEADS_PER_GROUP - 1)
        for h in heads:
            accumulate(c, h)
            if h > 0:
                scores(c + 1, causal, h - 1)

    @pl.when(n_full > 0)
    def _():
        for h in heads:
            scores(0, False, h)

    def two_pairs(k, carry):
        pair(2 * k, False)
        pair(2 * k + 1, False)
        return carry

    n_plain = jnp.maximum(n_full - 1, 0)
    lax.fori_loop(0, n_plain // 2, two_pairs, 0)

    @pl.when(n_plain % 2 == 1)
    def _():
        pair(n_plain - 1, False)

    @pl.when(n_full > 0)
    def _():
        pair(n_full - 1, True)

    @pl.when(n_full == 0)
    def _():
        for h in heads:
            scores(n_full, True, h)

    for h in heads:
        accumulate(n_full, h)

    acc = acc_ref[...]
    o_slc = acc[0:HEAD_DIM, :] / acc[HEAD_DIM:HEAD_DIM + 1, :]
    gates = gtt_ref[...]
    o_t = jnp.concatenate([gates[3 * h + 1:3 * h + 2, :] * o_slc[:, h * tq:(h + 1) * tq]
                           for h in range(HEADS_PER_GROUP)], axis=0)
    o_ref[...] = (opt_ref[...] + o_t).T.astype(BF16)


def _select(qt, bias_t, ksel, vsel_t, gates_t, o_part_t):
    batch, _, _, seq = qt.shape
    tq = TQ_SEL
    causal = jnp.asarray(np.where(np.arange(TK_SEL)[:, None] <= np.arange(tq)[None, :], 0.0, NEG), dtype=F32)
    cols = HEADS_PER_GROUP * tq
    n_chunks = seq // TK_SEL
    tile_t = lambda r: pl.BlockSpec((None, None, r, tq), lambda b, g, i: (b, g, 0, i))
    return pl.pallas_call(
        _select_body,
        grid=(batch, N_KV_HEADS, seq // tq),
        in_specs=[
            pl.BlockSpec((None, HEADS_PER_GROUP, HEAD_DIM, tq), lambda b, g, i: (b, g, 0, i)),
            tile_t(MAX_SEL_BLOCKS),
            pl.BlockSpec((None, None, seq, 2 * LANES), lambda b, g, i: (b, g, 0, 0)),
            pl.BlockSpec((None, None, n_chunks, V_ROWS, TK_SEL), lambda b, g, i: (b, g, 0, 0, 0)),
            _resident(),
            tile_t(GATE_ROWS), tile_t(HEADS_PER_GROUP * HEAD_DIM),
        ],
        out_specs=pl.BlockSpec((None, tq, HEADS_PER_GROUP * HEAD_DIM), lambda b, g, i: (b, i, g)),
        out_shape=jax.ShapeDtypeStruct((batch, seq, NSA_WIDTH), BF16),
        scratch_shapes=([pltpu.VMEM((2 * LANES, cols), BF16), pltpu.VMEM((1, cols), F32), pltpu.VMEM((V_ROWS, cols), F32)]
                        + [pltpu.VMEM((2, TK_SEL, tq), F32)] * HEADS_PER_GROUP
                        + [pltpu.VMEM((2, 1, tq), F32)] * HEADS_PER_GROUP),
        compiler_params=_params(3),
        name="select",
    )(qt, bias_t, ksel, vsel_t, causal, gates_t, o_part_t)


def _s5_body(u_ref, bt_ref, pw_ref, pwb_ref, a1_ref, ltri_ref, ct_ref, d_ref, wg_ref, bg_ref, o_ref, c_ref, x_ref, h_ref):
    @pl.when(pl.program_id(1) == 0)
    def _():
        c_ref[...] = jnp.zeros(c_ref.shape, F32)

    n = S5_COLS // S5_SPLIT
    u = u_ref[...]
    last = S5_CHUNK - 1
    ys = []
    for hf in range(S5_SPLIT):
        cs = slice(hf * n, (hf + 1) * n)
        p_re, p_im, q_re, q_im = pwb_ref[0, :, cs], pwb_ref[1, :, cs], pwb_ref[2, :, cs], pwb_ref[3, :, cs]
        pl_re, pl_im = pw_ref[0, last:last + 1, cs], pw_ref[1, last:last + 1, cs]
        a_re, a_im = a1_ref[0:1, cs], a1_ref[1:2, cs]
        width = S5_WIDTH // S5_SPLIT
        bu = _dot(u[:, hf * width:(hf + 1) * width].astype(BF16), bt_ref[hf])
        for j in range(u.shape[0] // S5_CHUNK):
            r = slice(j * S5_CHUNK, (j + 1) * S5_CHUNK)
            b_re, b_im = bu[r, 0:n].astype(BF16), bu[r, n:2 * n].astype(BF16)
            x_ref[r, 0:n] = q_re * b_re - q_im * b_im
            x_ref[r, n:2 * n] = q_re * b_im + q_im * b_re
        local = _dot(ltri_ref[...], x_ref[...])
        c_re, c_im = c_ref[0:1, cs], c_ref[1:2, cs]
        for j in range(u.shape[0] // S5_CHUNK):
            r = slice(j * S5_CHUNK, (j + 1) * S5_CHUNK)
            s_re, s_im = local[r, 0:n] + c_re, local[r, n:2 * n] + c_im
            t_re, t_im = s_re.astype(BF16), s_im.astype(BF16)
            h_ref[r, 0:n] = p_re * t_re - p_im * t_im
            h_ref[r, n:2 * n] = p_re * t_im + p_im * t_re
            e_re, e_im = s_re[last:last + 1], s_im[last:last + 1]
            l_re = pl_re * e_re - pl_im * e_im
            l_im = pl_re * e_im + pl_im * e_re
            c_re = a_re * l_re - a_im * l_im
            c_im = a_re * l_im + a_im * l_re
        c_ref[0:1, cs] = c_re
        c_ref[1:2, cs] = c_im
        ys.append(_dot(h_ref[...], ct_ref[hf]))
    y = _gelu_tanh(jnp.concatenate(ys, axis=1) + d_ref[...] * u)
    o_ref[...] = (y * jax.nn.sigmoid(_dot(y.astype(BF16), wg_ref[...]) + bg_ref[...])).astype(BF16)


def _s5(u, bt, powers, a1, ltri, ct, d, w_glu, b_glu):
    batch, seq, _ = u.shape
    ts = S5_TILE
    tile = pl.BlockSpec((None, ts, S5_WIDTH), lambda b, j: (b, j, 0))
    cols = 2 * S5_COLS // S5_SPLIT
    return pl.pallas_call(
        _s5_body,
        grid=(batch, seq // ts),
        in_specs=[tile] + [_resident()] * 9,
        out_specs=tile,
        out_shape=jax.ShapeDtypeStruct((batch, seq, S5_WIDTH), BF16),
        scratch_shapes=[pltpu.VMEM((2, S5_COLS), F32), pltpu.VMEM((ts, cols), BF16), pltpu.VMEM((ts, cols), BF16)],
        compiler_params=_params(2),
        name="s5",
    )(u, bt, powers, powers.astype(BF16), a1, ltri, ct, d, w_glu, b_glu)


def _ffn_out_body(*refs, final_norm):
    (h_ref, on_ref, os_ref, won_ref, wos_ref, g2_ref, w1_ref, w3_ref, w2_ref,
     gp_ref, wg_ref, p_ref, wp_ref) = refs[:13]
    o_ref = refs[-1]
    h = h_ref[...] + _dot(on_ref[...], won_ref[...]) + _dot(os_ref[...], wos_ref[...])
    h = _swiglu_half_step(h, g2_ref, w1_ref, w3_ref, w2_ref)
    gate = jax.nn.sigmoid(_dot(_rmsnorm(h, gp_ref[...]).astype(BF16), wg_ref[...]))
    h = h + gate * _dot(p_ref[...].astype(BF16), wp_ref[...])
    o_ref[...] = _rmsnorm(h, refs[13][...]) if final_norm else h


def _ffn_out(h, o_nsa, o_s5, wo_n, wo_s, g_ffn, w1, w3, w2, g_ple, w_gate, p, w_ple, g_final=None):
    t = h.shape[0]
    tm = min(TM_DENSE, t)
    row = lambda w: pl.BlockSpec((tm, w), lambda i: (i, 0))
    args = [h, o_nsa, o_s5, wo_n, wo_s, g_ffn, w1, w3, w2, g_ple, w_gate, p, w_ple]
    specs = [row(D_MODEL), row(NSA_WIDTH), row(S5_WIDTH)] + [_resident()] * 8 + [row(PLE_DIM), _resident()]
    if g_final is not None:
        args.append(g_final)
        specs.append(_resident())
    return pl.pallas_call(
        functools.partial(_ffn_out_body, final_norm=g_final is not None),
        grid=(t // tm,),
        in_specs=specs,
        out_specs=row(D_MODEL),
        out_shape=jax.ShapeDtypeStruct((t, D_MODEL), F32),
        compiler_params=_params(1),
        name="ffn_out",
    )(*args)


def _rearrange_w_in(w_in):
    g0 = NSA_WIDTH + 6 * KV_WIDTH
    per_group = 3 * HEADS_PER_GROUP
    pad = jnp.zeros((D_MODEL, LANES - per_group), w_in.dtype)
    gate_cols = []
    for g in range(N_KV_HEADS):
        gate_cols += [w_in[:, g0 + g * per_group:g0 + (g + 1) * per_group], pad]
    u0 = g0 + 3 * N_HEADS
    return jnp.concatenate([w_in[:, :g0], w_in[:, u0:]] + gate_cols, axis=1).astype(BF16)


def _rope_tables(seq):
    half = HEAD_DIM // 2
    inv = ROPE_THETA ** (-jnp.arange(half, dtype=F32) / half)
    ang = jnp.arange(seq, dtype=F32)[:, None] * inv[None, :]
    cos = jnp.tile(jnp.cos(ang), (1, LANES // half))
    sin = jnp.tile(jnp.concatenate([-jnp.sin(ang), jnp.sin(ang)], axis=1), (1, LANES // HEAD_DIM))
    return cos, sin


def _compress_weights(pe, w1, w2):
    w = w1.reshape(2, CMP_STRIDE, HEAD_DIM, CMP_HIDDEN).astype(BF16)
    zero = jnp.zeros_like(w)
    rows = [jnp.concatenate([w if e == g else zero for e in range(N_KV_HEADS)], axis=-1) for g in range(N_KV_HEADS)]
    big = jnp.stack(rows, axis=2).reshape(2, CMP_STRIDE * KV_WIDTH, N_KV_HEADS * CMP_HIDDEN)
    pe2 = jnp.broadcast_to(pe.reshape(2, CMP_STRIDE, 1, HEAD_DIM), (2, CMP_STRIDE, N_KV_HEADS, HEAD_DIM))
    return pe2.reshape(2, CMP_STRIDE * KV_WIDTH), big[0], big[1], w2.astype(BF16)


def _overlap_matrix(n_cmp_rows):
    c_start = np.arange(n_cmp_rows)[:, None] * CMP_STRIDE
    s_start = np.arange(MAX_SEL_BLOCKS)[None, :] * SEL_BLOCK
    ov = (c_start < s_start + SEL_BLOCK) & (c_start + CMP_LEN > s_start)
    return jnp.asarray(ov.T, dtype=BF16)


def _s5_tables(a_re, a_im, log_dt, b_re, b_im, c_re, c_im):
    dt = jnp.exp(log_dt)[:, None]
    lr, li = a_re * dt, a_im * dt
    mag = jnp.exp(lr)
    ab_re, ab_im = mag * jnp.cos(li), mag * jnp.sin(li)
    den = a_re * a_re + a_im * a_im
    nr, ni = ab_re - 1.0, ab_im
    bc_re = (nr * a_re + ni * a_im) / den
    bc_im = (ni * a_re - nr * a_im) / den
    bt_re = b_re * bc_re[..., None] - b_im * bc_im[..., None]
    bt_im = b_re * bc_im[..., None] + b_im * bc_re[..., None]
    gh = S5_GROUPS // S5_SPLIT
    eye = jnp.eye(gh, dtype=F32)
    split = lambda w: w.reshape((S5_SPLIT, gh) + w.shape[1:])
    blockdiag_in = lambda w: jnp.einsum("hgni,ge->hgien", split(w), eye).reshape(S5_SPLIT, gh * S5_GROUP, gh * S5_STATE)
    bt = jnp.concatenate([blockdiag_in(bt_re), blockdiag_in(bt_im)], axis=2).astype(BF16)
    blockdiag_out = lambda w: jnp.einsum("hgon,ge->hgneo", split(w), eye).reshape(S5_SPLIT, gh * S5_STATE, gh * S5_GROUP)
    ct = jnp.concatenate([blockdiag_out(c_re), -blockdiag_out(c_im)], axis=1).astype(BF16)
    k = jnp.arange(S5_CHUNK, dtype=F32)[:, None]
    lr, li = lr.reshape(1, S5_COLS), li.reshape(1, S5_COLS)
    powers = jnp.stack([jnp.exp(k * lr) * jnp.cos(k * li), jnp.exp(k * lr) * jnp.sin(k * li),
                        jnp.exp(-k * lr) * jnp.cos(k * li), -jnp.exp(-k * lr) * jnp.sin(k * li)])
    a1 = jnp.concatenate([ab_re.reshape(1, S5_COLS), ab_im.reshape(1, S5_COLS)], axis=0)
    ltri = jnp.asarray(np.kron(np.eye(S5_TILE // S5_CHUNK), np.tril(np.ones((S5_CHUNK, S5_CHUNK)))), dtype=BF16)
    return bt, powers, a1, ltri, ct


def kernel(x, p, norm_ffn1, ffn1_w1, ffn1_w3, ffn1_w2, norm_mix, w_in, cmp_pe_k, cmp_pe_v, cmp_wk1, cmp_wk2,
           cmp_wv1, cmp_wv2, s5_a_re, s5_a_im, s5_log_dt, s5_b_re, s5_b_im, s5_c_re, s5_c_im, s5_d, s5_w_glu,
           s5_b_glu, w_out, norm_ffn2, ffn2_w1, ffn2_w3, ffn2_w2, norm_ple, w_ple_gate, w_ple, norm_final):
    batch, seq, _ = x.shape
    depth = p.shape[0]
    t = batch * seq
    assert seq % TK_SEL == 0 and TK_SEL == TQ_SEL and seq % S5_TILE == 0 and seq >= WINDOW + TQ and seq % (CMP_STRIDE * CMP_STEP) == 0
    assert seq // SEL_BLOCK <= MAX_SEL_BLOCKS and seq // SEL_BLOCK >= SEL_TOPK
    bf = lambda w: w.astype(BF16)
    row = lambda v: v.reshape(1, -1)

    cos, sin_signed = _rope_tables(seq)
    overlap_t = _overlap_matrix(seq // CMP_STRIDE)
    h = x.reshape(t, D_MODEL)
    for i in range(depth):
        h, qt, kc, vc, ksel, vsel_t, kwin, vwin_t, gates_t, u = _ffn_proj(
            h, row(norm_ffn1[i]), bf(ffn1_w1[i]), bf(ffn1_w3[i]), bf(ffn1_w2[i]),
            row(norm_mix[i]), _rearrange_w_in(w_in[i]), cos, sin_signed, batch, seq)
        pe_k2, wkt, wkb, wk2 = _compress_weights(cmp_pe_k[i], cmp_wk1[i], cmp_wk2[i])
        pe_v2, wvt, wvb, wv2 = _compress_weights(cmp_pe_v[i], cmp_wv1[i], cmp_wv2[i])
        kcmp, vcmp_t = _compress(kc, vc, pe_k2, pe_v2, wkt, wkb, wvt, wvb, wk2, wv2)
        o_part_t, bias_t = _cmpwin(qt, kcmp, vcmp_t, overlap_t, kwin, vwin_t, gates_t)
        o_nsa = _select(qt, bias_t, ksel, vsel_t, gates_t, o_part_t)

        bt, powers, a1, ltri, ct = _s5_tables(s5_a_re[i], s5_a_im[i], s5_log_dt[i], s5_b_re[i], s5_b_im[i],
                                              s5_c_re[i], s5_c_im[i])
        o_s5 = _s5(u, bt, powers, a1, ltri, ct, row(s5_d[i]), bf(s5_w_glu[i]), row(s5_b_glu[i]))

        h = _ffn_out(h, o_nsa.reshape(t, NSA_WIDTH), o_s5.reshape(t, S5_WIDTH),
                     bf(w_out[i][:NSA_WIDTH]), bf(w_out[i][NSA_WIDTH:]),
                     row(norm_ffn2[i]), bf(ffn2_w1[i]), bf(ffn2_w3[i]), bf(ffn2_w2[i]),
                     row(norm_ple[i]), bf(w_ple_gate[i]), p[i].reshape(t, PLE_DIM), bf(w_ple[i]),
                     row(norm_final) if i == depth - 1 else None)
    return h.reshape(batch, seq, D_MODEL)
```

```python
import functools

import jax
import jax.numpy as jnp
import numpy as np
from jax import lax
from jax.experimental import pallas as pl
from jax.experimental.pallas import tpu as pltpu

D_MODEL = 1024
PLE_DIM = 256
D_FF = 2816
N_HEADS = 8
N_KV_HEADS = 2
HEAD_DIM = 64
HEADS_PER_GROUP = N_HEADS // N_KV_HEADS
NSA_WIDTH = N_HEADS * HEAD_DIM
KV_WIDTH = N_KV_HEADS * HEAD_DIM
S5_WIDTH = D_MODEL - NSA_WIDTH
S5_GROUP = 16
S5_GROUPS = S5_WIDTH // S5_GROUP
S5_STATE = 64
S5_COLS = S5_GROUPS * S5_STATE
CMP_LEN = 32
CMP_STRIDE = 16
CMP_HIDDEN = 256
SEL_BLOCK = 64
SEL_TOPK = 16
WINDOW = 512
ROPE_THETA = 10000.0
RMS_EPS = 1e-6
NEG = -1e30
BIG = 1e9

LANES = 128
V_ROWS = HEAD_DIM + 16
GATE_ROWS = 16
MAX_SEL_BLOCKS = LANES
VMEM_LIMIT = 52 * 1024 * 1024

TM_DENSE = 512
FF_CHUNK = 256
TQ = 256
WIN_SUB = 128
CMP_STEP = 128
TQ_SEL = 512
TK_SEL = 512
S5_TILE = 256
S5_CHUNK = 64
S5_SPLIT = 2

F32 = jnp.float32
BF16 = jnp.bfloat16


def _dot(a, b):
    return jnp.dot(a, b, preferred_element_type=F32)


def _rmsnorm(x, g):
    ms = jnp.mean(x * x, axis=-1, keepdims=True)
    return x * lax.rsqrt(ms + RMS_EPS) * g


def _gelu_tanh(x):
    return 0.5 * x * (1.0 + jnp.tanh(np.sqrt(2.0 / np.pi).astype(np.float32) * (x + 0.044715 * (x * x * x))))


def _params(n_grid):
    return pltpu.CompilerParams(dimension_semantics=("arbitrary",) * n_grid, vmem_limit_bytes=VMEM_LIMIT)


def _resident():
    return pl.BlockSpec(memory_space=pltpu.VMEM)


def _swiglu_half_step(h, g_ref, w1_ref, w3_ref, w2_ref):
    xn = _rmsnorm(h, g_ref[...]).astype(BF16)
    acc = jnp.zeros(h.shape, F32)
    for j in range(D_FF // FF_CHUNK):
        sl = slice(j * FF_CHUNK, (j + 1) * FF_CHUNK)
        a = _dot(xn, w1_ref[:, sl])
        b = _dot(xn, w3_ref[:, sl])
        act = (a * jax.nn.sigmoid(a) * b).astype(BF16)
        acc = acc + _dot(act, w2_ref[sl, :])
    return h + 0.5 * acc


_C_Q = 0
_C_CMP = NSA_WIDTH
_C_SLC = _C_CMP + 2 * KV_WIDTH
_C_WIN = _C_SLC + 2 * KV_WIDTH
_C_U = _C_WIN + 2 * KV_WIDTH
_C_G = _C_U + S5_WIDTH
_C_END = _C_G + N_KV_HEADS * LANES


def _rope_pair(x, cos, sin_signed, first_half):
    fwd = pltpu.roll(x, HEAD_DIM // 2, 1)
    bwd = pltpu.roll(x, LANES - HEAD_DIM // 2, 1)
    return x * cos + jnp.where(first_half, bwd, fwd) * sin_signed


def _ffn_proj_body(x_ref, g1_ref, w1_ref, w3_ref, w2_ref, g_ref, w_ref, cos_ref, sin_ref,
                   h_ref, qt_ref, kc_ref, vc_ref, ks_ref, vst_ref, kw_ref, vwt_ref, gtt_ref, u_ref, kv_ref,
                   *, tiles_per_seq):
    tm = x_ref.shape[0]
    h = _swiglu_half_step(x_ref[...], g1_ref, w1_ref, w3_ref, w2_ref)
    h_ref[...] = h
    xn = _rmsnorm(h, g_ref[...]).astype(BF16)
    cos = cos_ref[...]
    sin = sin_ref[...]
    lane = lax.broadcasted_iota(jnp.int32, (tm, LANES), 1)
    first_half = (lane % HEAD_DIM) < (HEAD_DIM // 2)
    rope = lambda x: _rope_pair(x, cos, sin, first_half)

    scale = HEAD_DIM ** -0.5
    for j in range(NSA_WIDTH // 256):
        z = _dot(xn, w_ref[:, _C_Q + 256 * j:_C_Q + 256 * (j + 1)])
        for k in range(2):
            rt = (rope(z[:, LANES * k:LANES * (k + 1)]) * scale).T.astype(BF16)
            for e in range(2):
                qt_ref[4 * j + 2 * k + e] = rt[e * HEAD_DIM:(e + 1) * HEAD_DIM, :]

    z = _dot(xn, w_ref[:, _C_CMP:_C_CMP + 256])
    kv_ref[0] = rope(z[:, :LANES])
    kv_ref[1] = z[:, LANES:]
    for j in range(CMP_STRIDE):
        every = pl.ds(j, tm // CMP_STRIDE, stride=CMP_STRIDE)
        kc_ref[:, j * KV_WIDTH:(j + 1) * KV_WIDTH] = kv_ref[0, every, :]
        vc_ref[:, j * KV_WIDTH:(j + 1) * KV_WIDTH] = kv_ref[1, every, :]

    zeros = jnp.zeros((tm, HEAD_DIM), BF16)
    pos = (pl.program_id(0) % tiles_per_seq) * tm + lax.broadcasted_iota(jnp.int32, (tm, LANES), 0)
    onehot = jnp.where(lane == pos // SEL_BLOCK, 1.0, 0.0).astype(BF16)

    z = _dot(xn, w_ref[:, _C_SLC:_C_SLC + 256])
    k = rope(z[:, :LANES]).astype(BF16)
    vt = z[:, LANES:].T.astype(BF16)
    for g in range(N_KV_HEADS):
        gs = slice(g * HEAD_DIM, (g + 1) * HEAD_DIM)
        ks_ref[g, :, 0:LANES] = onehot
        ks_ref[g, :, LANES:LANES + HEAD_DIM] = k[:, gs]
        ks_ref[g, :, LANES + HEAD_DIM:] = zeros
        vst_ref[g, 0:HEAD_DIM, :] = vt[gs, :]
        vst_ref[g, HEAD_DIM:, :] = jnp.ones((V_ROWS - HEAD_DIM, tm), BF16)

    z = _dot(xn, w_ref[:, _C_WIN:_C_WIN + 256])
    k = rope(z[:, :LANES]).astype(BF16)
    vt = z[:, LANES:].T.astype(BF16)
    for g in range(N_KV_HEADS):
        gs = slice(g * HEAD_DIM, (g + 1) * HEAD_DIM)
        kw_ref[g] = k[:, gs]
        for c in range(tm // WIN_SUB):
            vwt_ref[g, c, 0:HEAD_DIM, :] = vt[gs, c * WIN_SUB:(c + 1) * WIN_SUB]
            vwt_ref[g, c, HEAD_DIM:, :] = jnp.ones((V_ROWS - HEAD_DIM, WIN_SUB), BF16)

    for j in range(S5_WIDTH // 256):
        u_ref[:, 256 * j:256 * (j + 1)] = _dot(xn, w_ref[:, _C_U + 256 * j:_C_U + 256 * (j + 1)])
    gates_t = jax.nn.sigmoid(_dot(xn, w_ref[:, _C_G:_C_END])).T
    for g in range(N_KV_HEADS):
        gtt_ref[g] = gates_t[g * LANES:g * LANES + GATE_ROWS, :]


def _ffn_proj(x, g_ffn, w1, w3, w2, g, w_in_r, cos, sin_signed, batch, seq):
    t = batch * seq
    tm = TK_SEL
    n = seq // tm
    bl = lambda w: pl.BlockSpec((None, tm, w), lambda i: (i // n, i % n, 0))
    bgl = lambda w: pl.BlockSpec((None, N_KV_HEADS, tm, w), lambda i: (i // n, 0, i % n, 0))
    tab = pl.BlockSpec((tm, LANES), lambda i: (i % n, 0))
    row = pl.BlockSpec((tm, D_MODEL), lambda i: (i, 0))
    strided = pl.BlockSpec((None, tm // CMP_STRIDE, CMP_STRIDE * KV_WIDTH), lambda i: (i // n, i % n, 0))
    sds = jax.ShapeDtypeStruct
    return pl.pallas_call(
        functools.partial(_ffn_proj_body, tiles_per_seq=n),
        grid=(t // tm,),
        in_specs=[row] + [_resident()] * 6 + [tab, tab],
        out_specs=[
            row,
            pl.BlockSpec((None, N_HEADS, HEAD_DIM, tm), lambda i: (i // n, 0, 0, i % n)),
            strided, strided,
            bgl(2 * LANES),
            pl.BlockSpec((None, N_KV_HEADS, None, V_ROWS, tm), lambda i: (i // n, 0, i % n, 0, 0)),
            bgl(HEAD_DIM),
            pl.BlockSpec((None, N_KV_HEADS, tm // WIN_SUB, V_ROWS, WIN_SUB), lambda i: (i // n, 0, i % n, 0, 0)),
            pl.BlockSpec((None, N_KV_HEADS, GATE_ROWS, tm), lambda i: (i // n, 0, 0, i % n)),
            bl(S5_WIDTH),
        ],
        out_shape=[
            sds((t, D_MODEL), F32),
            sds((batch, N_HEADS, HEAD_DIM, seq), BF16),
            sds((batch, seq // CMP_STRIDE, CMP_STRIDE * KV_WIDTH), F32),
            sds((batch, seq // CMP_STRIDE, CMP_STRIDE * KV_WIDTH), F32),
            sds((batch, N_KV_HEADS, seq, 2 * LANES), BF16), sds((batch, N_KV_HEADS, n, V_ROWS, tm), BF16),
            sds((batch, N_KV_HEADS, seq, HEAD_DIM), BF16), sds((batch, N_KV_HEADS, seq // WIN_SUB, V_ROWS, WIN_SUB), BF16),
            sds((batch, N_KV_HEADS, GATE_ROWS, seq), F32), sds((batch, seq, S5_WIDTH), F32),
        ],
        scratch_shapes=[pltpu.VMEM((2, tm, KV_WIDTH), F32)],
        compiler_params=_params(1),
        name="ffn_proj",
    )(x, g_ffn, w1, w3, w2, g, w_in_r, cos, sin_signed)


def _compress_body(kc_ref, vc_ref, pek_ref, pev_ref, wkt_ref, wkb_ref, wvt_ref, wvb_ref, wk2_ref, wv2_ref,
                   ko_ref, vot_ref, v_ref):
    nc = kc_ref.shape[0]

    def hidden(x, pe_ref, wt_ref, wb_ref):
        top = _dot((x + pe_ref[0:1, :]).astype(BF16), wt_ref[...])
        bot = _dot((x + pe_ref[1:2, :]).astype(BF16), wb_ref[...])
        pre = top + pltpu.roll(bot, nc - 1, 0)
        return _gelu_tanh(pre).astype(BF16)

    hk = hidden(kc_ref[...], pek_ref, wkt_ref, wkb_ref)
    hv = hidden(vc_ref[...], pev_ref, wvt_ref, wvb_ref)
    for g in range(N_KV_HEADS):
        gs = slice(g * CMP_HIDDEN, (g + 1) * CMP_HIDDEN)
        ko_ref[g] = _dot(hk[:, gs], wk2_ref[...]).astype(BF16)
        v_ref[...] = _dot(hv[:, gs], wv2_ref[...])
        vot_ref[g] = v_ref[...].T.astype(BF16)


def _compress(kc, vc, pe_k2, pe_v2, wkt, wkb, wvt, wvb, wk2, wv2):
    batch, nc, width = kc.shape
    x_spec = pl.BlockSpec((None, nc, width), lambda b: (b, 0, 0))
    sds = jax.ShapeDtypeStruct
    return pl.pallas_call(
        _compress_body,
        grid=(batch,),
        in_specs=[x_spec, x_spec] + [_resident()] * 8,
        out_specs=[pl.BlockSpec((None, N_KV_HEADS, nc, HEAD_DIM), lambda b: (b, 0, 0, 0)),
                   pl.BlockSpec((None, N_KV_HEADS, HEAD_DIM, nc), lambda b: (b, 0, 0, 0))],
        out_shape=[sds((batch, N_KV_HEADS, nc, HEAD_DIM), BF16), sds((batch, N_KV_HEADS, HEAD_DIM, nc), BF16)],
        scratch_shapes=[pltpu.VMEM((nc, HEAD_DIM), F32)],
        compiler_params=_params(1),
        name="compress",
    )(kc, vc, pe_k2, pe_v2, wkt, wkb, wvt, wvb, wk2, wv2)


def _cmpwin_body(qt_ref, kc_ref, vct_ref, ovt_ref, cmask_ref, kw_ref, vwt_ref, gtt_ref, opt_ref, bias_ref,
                 s_ref, ocmp_ref, imp_ref):
    tq = qt_ref.shape[2]
    cols = HEADS_PER_GROUP * tq
    nc = kc_ref.shape[0]
    i = pl.program_id(2)
    q_t = jnp.concatenate([qt_ref[h] for h in range(HEADS_PER_GROUP)], axis=1)
    t_lane = i * tq + lax.broadcasted_iota(jnp.int32, (1, cols), 1) % tq

    def compressed(n):
        first = (i * tq) // CMP_STRIDE
        s = cmask_ref[pl.ds(pl.multiple_of(nc - first, CMP_STRIDE), n), :] + _dot(kc_ref[0:n, :], q_t)
        e = jnp.exp(s - jnp.max(s, axis=0, keepdims=True))
        den = jnp.maximum(jnp.sum(e, axis=0, keepdims=True), 1e-30)
        p = e * jnp.where(t_lane >= CMP_LEN - 1, 1.0 / den, 0.0)
        ocmp_ref[...] = _dot(vct_ref[:, 0:n], p.astype(BF16))
        psum = p[:, 0:tq]
        for h in range(1, HEADS_PER_GROUP):
            psum = psum + p[:, h * tq:(h + 1) * tq]
        p_hi = psum.astype(BF16)
        p_lo = (psum - p_hi.astype(F32)).astype(BF16)
        imp_ref[...] = _dot(ovt_ref[:, 0:n], p_hi) + _dot(ovt_ref[:, 0:n], p_lo)

    finished = (i * tq + tq - CMP_LEN) // CMP_STRIDE + 1
    n_steps = nc // CMP_STEP
    for k in range(1, n_steps + 1):
        in_step = finished > (k - 1) * CMP_STEP
        if k < n_steps:
            in_step = jnp.logical_and(in_step, finished <= k * CMP_STEP)
        pl.when(in_step)(functools.partial(compressed, k * CMP_STEP))
    o_cmp = ocmp_ref[...]
    imp = imp_ref[...]

    blk = lax.broadcasted_iota(jnp.int32, (MAX_SEL_BLOCKS, tq), 0)
    behind = t_lane[:, 0:tq] // SEL_BLOCK - blk
    score = jnp.where(blk == 0, BIG, jnp.where(behind == 0, BIG, jnp.where(behind == 1, BIG, imp)))
    score = jnp.where(behind >= 0, score, -BIG)
    chosen = jnp.zeros((MAX_SEL_BLOCKS, tq), F32)
    for _ in range(SEL_TOPK):
        best = jnp.max(score, axis=0, keepdims=True)
        first = jnp.min(jnp.where(score == best, blk, MAX_SEL_BLOCKS), axis=0, keepdims=True)
        hit = blk == first
        chosen = jnp.where(hit, 1.0, chosen)
        score = jnp.where(hit, -jnp.inf, score)
    bias_ref[...] = jnp.where(chosen > 0.0, jnp.where(behind >= 0, 0.0, NEG), NEG).astype(BF16)

    span = WINDOW + tq
    start = pl.multiple_of(jnp.maximum(i * tq - WINDOW, 0), tq)
    s_ref[...] = _dot(kw_ref[pl.ds(start, span), :], q_t)

    def band(rows):
        kpos = start + rows.start + lax.broadcasted_iota(jnp.int32, (rows.stop - rows.start, cols), 0)
        dist = lax.bitcast_convert_type(t_lane - kpos, jnp.uint32)
        s_ref[rows, :] = jnp.where(dist < WINDOW, s_ref[rows, :], NEG)

    @pl.when(i * tq >= WINDOW)
    def _():
        band(slice(0, tq))
        band(slice(span - tq, span))

    @pl.when(i * tq < WINDOW)
    def _():
        band(slice(0, span))

    s = s_ref[...]
    e = jnp.exp((s - jnp.max(s, axis=0, keepdims=True)).astype(BF16))
    c0 = start // WIN_SUB
    vw_t = jnp.concatenate([vwt_ref[c0 + j] for j in range(span // WIN_SUB)], axis=1)
    acc = _dot(vw_t, e)
    o_win = acc[0:HEAD_DIM, :] / acc[HEAD_DIM:HEAD_DIM + 1, :]

    gates = gtt_ref[...]
    for h in range(HEADS_PER_GROUP):
        hs = slice(h * tq, (h + 1) * tq)
        opt_ref[h * HEAD_DIM:(h + 1) * HEAD_DIM, :] = (gates[3 * h:3 * h + 1, :] * o_cmp[:, hs]
                                                       + gates[3 * h + 2:3 * h + 3, :] * o_win[:, hs])


def _compressed_mask(nc):
    r = np.arange(2 * nc)[:, None] - nc
    c = np.arange(HEADS_PER_GROUP * TQ)[None, :] % TQ
    return jnp.asarray(np.where(CMP_STRIDE * r + CMP_LEN - 1 <= c, 0.0, NEG), dtype=F32)


def _cmpwin(qt, kcmp, vcmp_t, overlap_t, kwin, vwin_t, gates_t):
    batch, _, _, seq = qt.shape
    nc = kcmp.shape[2]
    tq = TQ
    per_bg = lambda *shape: pl.BlockSpec((None, None) + shape, lambda b, g, i: (b, g) + (0,) * len(shape))
    sds = jax.ShapeDtypeStruct
    return pl.pallas_call(
        _cmpwin_body,
        grid=(batch, N_KV_HEADS, seq // tq),
        in_specs=[
            pl.BlockSpec((None, HEADS_PER_GROUP, HEAD_DIM, tq), lambda b, g, i: (b, g, 0, i)),
            per_bg(nc, HEAD_DIM), per_bg(HEAD_DIM, nc), _resident(), _resident(),
            per_bg(seq, HEAD_DIM), per_bg(seq // WIN_SUB, V_ROWS, WIN_SUB),
            pl.BlockSpec((None, None, GATE_ROWS, tq), lambda b, g, i: (b, g, 0, i)),
        ],
        out_specs=[pl.BlockSpec((None, None, HEADS_PER_GROUP * HEAD_DIM, tq), lambda b, g, i: (b, g, 0, i)),
                   pl.BlockSpec((None, None, MAX_SEL_BLOCKS, tq), lambda b, g, i: (b, g, 0, i))],
        out_shape=[sds((batch, N_KV_HEADS, HEADS_PER_GROUP * HEAD_DIM, seq), F32),
                   sds((batch, N_KV_HEADS, MAX_SEL_BLOCKS, seq), BF16)],
        scratch_shapes=[pltpu.VMEM((WINDOW + tq, HEADS_PER_GROUP * tq), F32),
                        pltpu.VMEM((HEAD_DIM, HEADS_PER_GROUP * tq), F32), pltpu.VMEM((MAX_SEL_BLOCKS, tq), F32)],
        compiler_params=_params(3),
        name="cmpwin",
    )(qt, kcmp, vcmp_t, overlap_t, _compressed_mask(nc), kwin, vwin_t, gates_t)


def _select_body(qt_ref, bias_ref, ks_ref, vst_ref, causal_ref, gtt_ref, opt_ref, o_ref, qa_ref, m_ref, acc_ref, *head_refs):
    tq = qt_ref.shape[2]
    cols = HEADS_PER_GROUP * tq
    tk = TK_SEL
    i = pl.program_id(2)

    bias = bias_ref[...]
    for h in range(HEADS_PER_GROUP):
        qa_ref[0:LANES, h * tq:(h + 1) * tq] = bias
        qa_ref[LANES:LANES + HEAD_DIM, h * tq:(h + 1) * tq] = qt_ref[h]
    qa_ref[LANES + HEAD_DIM:, :] = jnp.zeros((LANES - HEAD_DIM, cols), BF16)
    m_ref[...] = jnp.full(m_ref.shape, NEG, F32)
    acc_ref[...] = jnp.zeros(acc_ref.shape, F32)

    s_refs, mx_refs = head_refs[:HEADS_PER_GROUP], head_refs[HEADS_PER_GROUP:]

    def scores(c, causal, h):
        hs = slice(h * tq, (h + 1) * tq)
        off = pl.multiple_of(c * tk, tk)
        s = _dot(ks_ref[pl.ds(off, tk), :], qa_ref[:, hs])
        if causal:
            s = causal_ref[...] + s
        s_refs[h][c % 2] = s
        mx_refs[h][c % 2] = jnp.max(s, axis=0, keepdims=True)

    def accumulate(c, h):
        hs = slice(h * tq, (h + 1) * tq)
        m_old = m_ref[:, hs]
        m_new = jnp.maximum(m_old, mx_refs[h][c % 2])
        p = jnp.exp((s_refs[h][c % 2] - m_new).astype(BF16))
        acc_ref[:, hs] = jnp.exp(m_old - m_new) * acc_ref[:, hs] + _dot(vst_ref[c], p)
        m_ref[:, hs] = m_new

    heads = range(HEADS_PER_GROUP)

    n_full = (i * tq) // tk

    def pair(c, causal):
        scores(c + 1, causal, HEADS_PER_GROUP - 1)
        for h in heads:
            accumulate(c, h)
            if h > 0:
                scores(c + 1, causal, h - 1)

    @pl.when(n_full > 0)
    def _():
        for h in heads:
            scores(0, False, h)

    def two_pairs(k, carry):
        pair(2 * k, False)
        pair(2 * k + 1, False)
        return carry

    n_plain = jnp.maximum(n_full - 1, 0)
    lax.fori_loop(0, n_plain // 2, two_pairs, 0)

    @pl.when(n_plain % 2 == 1)
    def _():
        pair(n_plain - 1, False)

    @pl.when(n_full > 0)
    def _():
        pair(n_full - 1, True)

    @pl.when(n_full == 0)
    def _():
        for h in heads:
            scores(n_full, True, h)

    for h in heads:
        accumulate(n_full, h)

    acc = acc_ref[...]
    o_slc = acc[0:HEAD_DIM, :] / acc[HEAD_DIM:HEAD_DIM + 1, :]
    gates = gtt_ref[...]
    o_t = jnp.concatenate([gates[3 * h + 1:3 * h + 2, :] * o_slc[:, h * tq:(h + 1) * tq]
                           for h in range(HEADS_PER_GROUP)], axis=0)
    o_ref[...] = (opt_ref[...] + o_t).T.astype(BF16)


def _select(qt, bias_t, ksel, vsel_t, gates_t, o_part_t):
    batch, _, _, seq = qt.shape
    tq = TQ_SEL
    causal = jnp.asarray(np.where(np.arange(TK_SEL)[:, None] <= np.arange(tq)[None, :], 0.0, NEG), dtype=F32)
    cols = HEADS_PER_GROUP * tq
    n_chunks = seq // TK_SEL
    tile_t = lambda r: pl.BlockSpec((None, None, r, tq), lambda b, g, i: (b, g, 0, i))
    return pl.pallas_call(
        _select_body,
        grid=(batch, N_KV_HEADS, seq // tq),
        in_specs=[
            pl.BlockSpec((None, HEADS_PER_GROUP, HEAD_DIM, tq), lambda b, g, i: (b, g, 0, i)),
            tile_t(MAX_SEL_BLOCKS),
            pl.BlockSpec((None, None, seq, 2 * LANES), lambda b, g, i: (b, g, 0, 0)),
            pl.BlockSpec((None, None, n_chunks, V_ROWS, TK_SEL), lambda b, g, i: (b, g, 0, 0, 0)),
            _resident(),
            tile_t(GATE_ROWS), tile_t(HEADS_PER_GROUP * HEAD_DIM),
        ],
        out_specs=pl.BlockSpec((None, tq, HEADS_PER_GROUP * HEAD_DIM), lambda b, g, i: (b, i, g)),
        out_shape=jax.ShapeDtypeStruct((batch, seq, NSA_WIDTH), BF16),
        scratch_shapes=([pltpu.VMEM((2 * LANES, cols), BF16), pltpu.VMEM((1, cols), F32), pltpu.VMEM((V_ROWS, cols), F32)]
                        + [pltpu.VMEM((2, TK_SEL, tq), F32)] * HEADS_PER_GROUP
                        + [pltpu.VMEM((2, 1, tq), F32)] * HEADS_PER_GROUP),
        compiler_params=_params(3),
        name="select",
    )(qt, bias_t, ksel, vsel_t, causal, gates_t, o_part_t)


def _s5_body(u_ref, bt_ref, pw_ref, pwb_ref, a1_ref, ltri_ref, ct_ref, d_ref, wg_ref, bg_ref, o_ref, c_ref, x_ref, h_ref):
    @pl.when(pl.program_id(1) == 0)
    def _():
        c_ref[...] = jnp.zeros(c_ref.shape, F32)

    n = S5_COLS // S5_SPLIT
    u = u_ref[...]
    last = S5_CHUNK - 1
    ys = []
    for hf in range(S5_SPLIT):
        cs = slice(hf * n, (hf + 1) * n)
        p_re, p_im, q_re, q_im = pwb_ref[0, :, cs], pwb_ref[1, :, cs], pwb_ref[2, :, cs], pwb_ref[3, :, cs]
        pl_re, pl_im = pw_ref[0, last:last + 1, cs], pw_ref[1, last:last + 1, cs]
        a_re, a_im = a1_ref[0:1, cs], a1_ref[1:2, cs]
        width = S5_WIDTH // S5_SPLIT
        bu = _dot(u[:, hf * width:(hf + 1) * width].astype(BF16), bt_ref[hf])
        for j in range(u.shape[0] // S5_CHUNK):
            r = slice(j * S5_CHUNK, (j + 1) * S5_CHUNK)
            b_re, b_im = bu[r, 0:n].astype(BF16), bu[r, n:2 * n].astype(BF16)
            x_ref[r, 0:n] = q_re * b_re - q_im * b_im
            x_ref[r, n:2 * n] = q_re * b_im + q_im * b_re
        local = _dot(ltri_ref[...], x_ref[...])
        c_re, c_im = c_ref[0:1, cs], c_ref[1:2, cs]
        for j in range(u.shape[0] // S5_CHUNK):
            r = slice(j * S5_CHUNK, (j + 1) * S5_CHUNK)
            s_re, s_im = local[r, 0:n] + c_re, local[r, n:2 * n] + c_im
            t_re, t_im = s_re.astype(BF16), s_im.astype(BF16)
            h_ref[r, 0:n] = p_re * t_re - p_im * t_im
            h_ref[r, n:2 * n] = p_re * t_im + p_im * t_re
            e_re, e_im = s_re[last:last + 1], s_im[last:last + 1]
            l_re = pl_re * e_re - pl_im * e_im
            l_im = pl_re * e_im + pl_im * e_re
            c_re = a_re * l_re - a_im * l_im
            c_im = a_re * l_im + a_im * l_re
        c_ref[0:1, cs] = c_re
        c_ref[1:2, cs] = c_im
        ys.append(_dot(h_ref[...], ct_ref[hf]))
    y = _gelu_tanh(jnp.concatenate(ys, axis=1) + d_ref[...] * u)
    o_ref[...] = (y * jax.nn.sigmoid(_dot(y.astype(BF16), wg_ref[...]) + bg_ref[...])).astype(BF16)


def _s5(u, bt, powers, a1, ltri, ct, d, w_glu, b_glu):
    batch, seq, _ = u.shape
    ts = S5_TILE
    tile = pl.BlockSpec((None, ts, S5_WIDTH), lambda b, j: (b, j, 0))
    cols = 2 * S5_COLS // S5_SPLIT
    return pl.pallas_call(
        _s5_body,
        grid=(batch, seq // ts),
        in_specs=[tile] + [_resident()] * 9,
        out_specs=tile,
        out_shape=jax.ShapeDtypeStruct((batch, seq, S5_WIDTH), BF16),
        scratch_shapes=[pltpu.VMEM((2, S5_COLS), F32), pltpu.VMEM((ts, cols), BF16), pltpu.VMEM((ts, cols), BF16)],
        compiler_params=_params(2),
        name="s5",
    )(u, bt, powers, powers.astype(BF16), a1, ltri, ct, d, w_glu, b_glu)


def _ffn_out_body(*refs, final_norm):
    (h_ref, on_ref, os_ref, won_ref, wos_ref, g2_ref, w1_ref, w3_ref, w2_ref,
     gp_ref, wg_ref, p_ref, wp_ref) = refs[:13]
    o_ref = refs[-1]
    h = h_ref[...] + _dot(on_ref[...], won_ref[...]) + _dot(os_ref[...], wos_ref[...])
    h = _swiglu_half_step(h, g2_ref, w1_ref, w3_ref, w2_ref)
    gate = jax.nn.sigmoid(_dot(_rmsnorm(h, gp_ref[...]).astype(BF16), wg_ref[...]))
    h = h + gate * _dot(p_ref[...].astype(BF16), wp_ref[...])
    o_ref[...] = _rmsnorm(h, refs[13][...]) if final_norm else h


def _ffn_out(h, o_nsa, o_s5, wo_n, wo_s, g_ffn, w1, w3, w2, g_ple, w_gate, p, w_ple, g_final=None):
    t = h.shape[0]
    tm = min(TM_DENSE, t)
    row = lambda w: pl.BlockSpec((tm, w), lambda i: (i, 0))
    args = [h, o_nsa, o_s5, wo_n, wo_s, g_ffn, w1, w3, w2, g_ple, w_gate, p, w_ple]
    specs = [row(D_MODEL), row(NSA_WIDTH), row(S5_WIDTH)] + [_resident()] * 8 + [row(PLE_DIM), _resident()]
    if g_final is not None:
        args.append(g_final)
        specs.append(_resident())
    return pl.pallas_call(
        functools.partial(_ffn_out_body, final_norm=g_final is not None),
        grid=(t // tm,),
        in_specs=specs,
        out_specs=row(D_MODEL),
        out_shape=jax.ShapeDtypeStruct((t, D_MODEL), F32),
        compiler_params=_params(1),
        name="ffn_out",
    )(*args)


def _rearrange_w_in(w_in):
    g0 = NSA_WIDTH + 6 * KV_WIDTH
    per_group = 3 * HEADS_PER_GROUP
    pad = jnp.zeros((D_MODEL, LANES - per_group), w_in.dtype)
    gate_cols = []
    for g in range(N_KV_HEADS):
        gate_cols += [w_in[:, g0 + g * per_group:g0 + (g + 1) * per_group], pad]
    u0 = g0 + 3 * N_HEADS
    return jnp.concatenate([w_in[:, :g0], w_in[:, u0:]] + gate_cols, axis=1).astype(BF16)


def _rope_tables(seq):
    half = HEAD_DIM // 2
    inv = ROPE_THETA ** (-jnp.arange(half, dtype=F32) / half)
    ang = jnp.arange(seq, dtype=F32)[:, None] * inv[None, :]
    cos = jnp.tile(jnp.cos(ang), (1, LANES // half))
    sin = jnp.tile(jnp.concatenate([-jnp.sin(ang), jnp.sin(ang)], axis=1), (1, LANES // HEAD_DIM))
    return cos, sin


def _compress_weights(pe, w1, w2):
    w = w1.reshape(2, CMP_STRIDE, HEAD_DIM, CMP_HIDDEN).astype(BF16)
    zero = jnp.zeros_like(w)
    rows = [jnp.concatenate([w if e == g else zero for e in range(N_KV_HEADS)], axis=-1) for g in range(N_KV_HEADS)]
    big = jnp.stack(rows, axis=2).reshape(2, CMP_STRIDE * KV_WIDTH, N_KV_HEADS * CMP_HIDDEN)
    pe2 = jnp.broadcast_to(pe.reshape(2, CMP_STRIDE, 1, HEAD_DIM), (2, CMP_STRIDE, N_KV_HEADS, HEAD_DIM))
    return pe2.reshape(2, CMP_STRIDE * KV_WIDTH), big[0], big[1], w2.astype(BF16)


def _overlap_matrix(n_cmp_rows):
    c_start = np.arange(n_cmp_rows)[:, None] * CMP_STRIDE
    s_start = np.arange(MAX_SEL_BLOCKS)[None, :] * SEL_BLOCK
    ov = (c_start < s_start + SEL_BLOCK) & (c_start + CMP_LEN > s_start)
    return jnp.asarray(ov.T, dtype=BF16)


def _s5_tables(a_re, a_im, log_dt, b_re, b_im, c_re, c_im):
    dt = jnp.exp(log_dt)[:, None]
    lr, li = a_re * dt, a_im * dt
    mag = jnp.exp(lr)
    ab_re, ab_im = mag * jnp.cos(li), mag * jnp.sin(li)
    den = a_re * a_re + a_im * a_im
    nr, ni = ab_re - 1.0, ab_im
    bc_re = (nr * a_re + ni * a_im) / den
    bc_im = (ni * a_re - nr * a_im) / den
    bt_re = b_re * bc_re[..., None] - b_im * bc_im[..., None]
    bt_im = b_re * bc_im[..., None] + b_im * bc_re[..., None]
    gh = S5_GROUPS // S5_SPLIT
    eye = jnp.eye(gh, dtype=F32)
    split = lambda w: w.reshape((S5_SPLIT, gh) + w.shape[1:])
    blockdiag_in = lambda w: jnp.einsum("hgni,ge->hgien", split(w), eye).reshape(S5_SPLIT, gh * S5_GROUP, gh * S5_STATE)
    bt = jnp.concatenate([blockdiag_in(bt_re), blockdiag_in(bt_im)], axis=2).astype(BF16)
    blockdiag_out = lambda w: jnp.einsum("hgon,ge->hgneo", split(w), eye).reshape(S5_SPLIT, gh * S5_STATE, gh * S5_GROUP)
    ct = jnp.concatenate([blockdiag_out(c_re), -blockdiag_out(c_im)], axis=1).astype(BF16)
    k = jnp.arange(S5_CHUNK, dtype=F32)[:, None]
    lr, li = lr.reshape(1, S5_COLS), li.reshape(1, S5_COLS)
    powers = jnp.stack([jnp.exp(k * lr) * jnp.cos(k * li), jnp.exp(k * lr) * jnp.sin(k * li),
                        jnp.exp(-k * lr) * jnp.cos(k * li), -jnp.exp(-k * lr) * jnp.sin(k * li)])
    a1 = jnp.concatenate([ab_re.reshape(1, S5_COLS), ab_im.reshape(1, S5_COLS)], axis=0)
    ltri = jnp.asarray(np.kron(np.eye(S5_TILE // S5_CHUNK), np.tril(np.ones((S5_CHUNK, S5_CHUNK)))), dtype=BF16)
    return bt, powers, a1, ltri, ct


def kernel(x, p, norm_ffn1, ffn1_w1, ffn1_w3, ffn1_w2, norm_mix, w_in, cmp_pe_k, cmp_pe_v, cmp_wk1, cmp_wk2,
           cmp_wv1, cmp_wv2, s5_a_re, s5_a_im, s5_log_dt, s5_b_re, s5_b_im, s5_c_re, s5_c_im, s5_d, s5_w_glu,
           s5_b_glu, w_out, norm_ffn2, ffn2_w1, ffn2_w3, ffn2_w2, norm_ple, w_ple_gate, w_ple, norm_final):
    batch, seq, _ = x.shape
    depth = p.shape[0]
    t = batch * seq
    assert seq % TK_SEL == 0 and TK_SEL == TQ_SEL and seq % S5_TILE == 0 and seq >= WINDOW + TQ and seq % (CMP_STRIDE * CMP_STEP) == 0
    assert seq // SEL_BLOCK <= MAX_SEL_BLOCKS and seq // SEL_BLOCK >= SEL_TOPK
    bf = lambda w: w.astype(BF16)
    row = lambda v: v.reshape(1, -1)

    cos, sin_signed = _rope_tables(seq)
    overlap_t = _overlap_matrix(seq // CMP_STRIDE)
    h = x.reshape(t, D_MODEL)
    for i in range(depth):
        h, qt, kc, vc, ksel, vsel_t, kwin, vwin_t, gates_t, u = _ffn_proj(
            h, row(norm_ffn1[i]), bf(ffn1_w1[i]), bf(ffn1_w3[i]), bf(ffn1_w2[i]),
            row(norm_mix[i]), _rearrange_w_in(w_in[i]), cos, sin_signed, batch, seq)
        pe_k2, wkt, wkb, wk2 = _compress_weights(cmp_pe_k[i], cmp_wk1[i], cmp_wk2[i])
        pe_v2, wvt, wvb, wv2 = _compress_weights(cmp_pe_v[i], cmp_wv1[i], cmp_wv2[i])
        kcmp, vcmp_t = _compress(kc, vc, pe_k2, pe_v2, wkt, wkb, wvt, wvb, wk2, wv2)
        o_part_t, bias_t = _cmpwin(qt, kcmp, vcmp_t, overlap_t, kwin, vwin_t, gates_t)
        o_nsa = _select(qt, bias_t, ksel, vsel_t, gates_t, o_part_t)

        bt, powers, a1, ltri, ct = _s5_tables(s5_a_re[i], s5_a_im[i], s5_log_dt[i], s5_b_re[i], s5_b_im[i],
                                              s5_c_re[i], s5_c_im[i])
        o_s5 = _s5(u, bt, powers, a1, ltri, ct, row(s5_d[i]), bf(s5_w_glu[i]), row(s5_b_glu[i]))

        h = _ffn_out(h, o_nsa.reshape(t, NSA_WIDTH), o_s5.reshape(t, S5_WIDTH),
                     bf(w_out[i][:NSA_WIDTH]), bf(w_out[i][NSA_WIDTH:]),
                     row(norm_ffn2[i]), bf(ffn2_w1[i]), bf(ffn2_w3[i]), bf(ffn2_w2[i]),
                     row(norm_ple[i]), bf(w_ple_gate[i]), p[i].reshape(t, PLE_DIM), bf(w_ple[i]),
                     row(norm_final) if i == depth - 1 else None)
    return h.reshape(batch, seq, D_MODEL)
```
